```python
import numpy as np
import jax, jax.numpy as jnp
from jax import lax

D_MODEL = 1024
BATCH = 4
SEQ = 8192
DEPTH = 1

N_META = 16
CHUNK = 128
PAD = CHUNK - N_META
RET_HEADS = 4
RET_QK_DIM = 128
RET_V_DIM = 256
SWA_Q_HEADS = 16
SWA_KV_HEADS = 4
SWA_GROUP = SWA_Q_HEADS // SWA_KV_HEADS
SWA_HEAD_DIM = 64
SWA_WINDOW = 128
D_FF = 4 * D_MODEL
EPS = 1e-6
RET_QK_W = RET_HEADS * RET_QK_DIM
RET_V_W = RET_HEADS * RET_V_DIM
SWA_Q_W = SWA_Q_HEADS * SWA_HEAD_DIM
SWA_KV_W = SWA_KV_HEADS * SWA_HEAD_DIM
IN_SPLITS = (RET_QK_W, RET_QK_W, RET_V_W, RET_V_W, SWA_Q_W, SWA_KV_W, SWA_KV_W, D_MODEL, D_MODEL)
D_IN = 2 * RET_QK_W + 2 * RET_V_W + SWA_Q_W + 2 * SWA_KV_W + 2 * D_MODEL

kernel_name = 'hybrid_retention_swa_block'


def rms_norm(x, g):
    xf = x.astype(jnp.float32)
    y = xf * lax.rsqrt(jnp.mean(xf * xf, axis=-1, keepdims=True) + EPS)
    return (y * g.astype(jnp.float32)).astype(x.dtype)


def retention(q, k, v):
    B, Lp = q.shape[0], q.shape[1]
    nb = Lp // CHUNK
    log_gamma = jnp.log1p(-(2.0 ** (-5.0 - jnp.arange(RET_HEADS, dtype=jnp.float32))))
    idx = jnp.arange(CHUNK, dtype=jnp.float32)
    diff = idx[:, None] - idx[None, :]
    intra = jnp.where(diff >= 0, jnp.exp(log_gamma[:, None, None] * jnp.maximum(diff, 0.0)), 0.0)
    q_decay = jnp.exp(log_gamma[:, None] * (idx + 1.0))[:, :, None]
    k_decay = jnp.exp(log_gamma[:, None] * (CHUNK - 1.0 - idx))[:, :, None]
    chunk_decay = jnp.exp(log_gamma * CHUNK)[:, None, None]
    k = k * (RET_QK_DIM ** -0.5)
    to_chunks = lambda t: t.reshape(B, nb, CHUNK, RET_HEADS, t.shape[-1]).transpose(1, 0, 3, 2, 4)
    qc, kc, vc = to_chunks(q), to_chunks(k), to_chunks(v)

    def step(state, inp):
        qn, kn, vn = inp
        s = jnp.einsum('bhqd,bhkd->bhqk', qn, kn) * intra
        o = (jnp.einsum('bhqk,bhkv->bhqv', s, vn)
             + jnp.einsum('bhqd,bhdv->bhqv', qn * q_decay, state))
        state = chunk_decay * state + jnp.einsum('bhkd,bhkv->bhdv', kn * k_decay, vn)
        return state, o

    state0 = jnp.zeros((B, RET_HEADS, RET_QK_DIM, RET_V_DIM), jnp.float32)
    _, o = lax.scan(step, state0, (qc, kc, vc))
    return o.transpose(1, 0, 3, 2, 4).reshape(B, Lp, RET_HEADS, RET_V_DIM).astype(q.dtype)


def sliding_window_attention(q, k, v, q_norm_g, k_norm_g, sinks):
    B, Lp = q.shape[0], q.shape[1]
    nb = Lp // CHUNK
    q = rms_norm(q, q_norm_g) * (SWA_HEAD_DIM ** -0.5)
    k = rms_norm(k, k_norm_g)
    qb = jnp.moveaxis(q.reshape(B, nb, CHUNK, SWA_KV_HEADS, SWA_GROUP, SWA_HEAD_DIM), 1, 0)
    kb = k.reshape(B, nb, CHUNK, SWA_KV_HEADS, SWA_HEAD_DIM)
    vb = v.reshape(B, nb, CHUNK, SWA_KV_HEADS, SWA_HEAD_DIM)
    shift = lambda t: jnp.pad(t, ((0, 0), (1, 0), (0, 0), (0, 0), (0, 0)))[:, :-1]
    k_band = jnp.moveaxis(jnp.concatenate([shift(kb), kb], axis=2), 1, 0)
    v_band = jnp.moveaxis(jnp.concatenate([shift(vb), vb], axis=2), 1, 0)
    k_meta = k[:, PAD:CHUNK]
    v_meta = v[:, PAD:CHUNK]
    slopes = (2.0 ** (-8.0 * jnp.arange(1, SWA_Q_HEADS + 1, dtype=jnp.float32) / SWA_Q_HEADS)
              ).reshape(SWA_KV_HEADS, SWA_GROUP)[:, :, None, None]
    sink = sinks.astype(jnp.float32).reshape(SWA_KV_HEADS, SWA_GROUP)[:, :, None, None]
    is_meta = jnp.arange(N_META + 2 * CHUNK) < N_META

    def block(args):
        qn, kn, vn, n = args
        t = n * CHUNK + jnp.arange(CHUNK)
        s = jnp.concatenate([PAD + jnp.arange(N_META), (n - 1) * CHUNK + jnp.arange(2 * CHUNK)])
        dist = t[:, None] - s[None, :]
        valid = (dist >= 0) & (is_meta[None, :] | ((dist < SWA_WINDOW) & (s[None, :] >= CHUNK)))
        keys = jnp.concatenate([k_meta, kn], axis=1)
        vals = jnp.concatenate([v_meta, vn], axis=1)
        scores = (jnp.einsum('bqhgd,bkhd->bhgqk', qn, keys).astype(jnp.float32)
                  - slopes * dist.astype(jnp.float32))
        scores = jnp.where(valid, scores, -jnp.inf)
        m = jnp.maximum(jnp.max(scores, axis=-1, keepdims=True), sink)
        p = jnp.exp(scores - m)
        probs = p / (jnp.sum(p, axis=-1, keepdims=True) + jnp.exp(sink - m))
        return jnp.einsum('bhgqk,bkhd->bqhgd', probs.astype(vals.dtype), vals)

    out = lax.map(block, (qb, k_band, v_band, jnp.arange(nb, dtype=jnp.int32)))
    return jnp.moveaxis(out, 0, 1).reshape(B, Lp, SWA_Q_W)


def mixer_layer(h, mix_norm_g, w_in, ret_norm_g, q_norm_g, k_norm_g, sinks, w_ret_branch, w_swa_branch, w_out):
    B, L = h.shape[0], h.shape[1]
    Lp = L + PAD
    hn = rms_norm(h, mix_norm_g)
    proj = hn @ w_in
    split_points = np.cumsum(IN_SPLITS)[:-1].tolist()
    rq, rk, rv, rg, sq, sk, sv, gr, gs = jnp.split(proj, split_points, axis=-1)
    pad = lambda t, nh: jnp.pad(t, ((0, 0), (PAD, 0), (0, 0))).reshape(B, Lp, nh, -1)
    y_ret = retention(pad(rq, RET_HEADS), pad(rk, RET_HEADS), pad(rv, RET_HEADS))[:, PAD:]
    y_ret = rms_norm(y_ret, ret_norm_g).reshape(B, L, RET_V_W) * jax.nn.silu(rg)
    y_swa = sliding_window_attention(pad(sq, SWA_Q_HEADS), pad(sk, SWA_KV_HEADS), pad(sv, SWA_KV_HEADS),
                                     q_norm_g, k_norm_g, sinks)[:, PAD:]
    merged = (jax.nn.sigmoid(gr) * (y_ret @ w_ret_branch)
              + jax.nn.sigmoid(gs) * (y_swa @ w_swa_branch))
    return merged @ w_out


def setup_inputs(seed: int = 0) -> dict:
    key = jax.random.key(seed)
    ks = jax.random.split(key, 16)
    f = jnp.float32
    nrm = lambda k, shape, scale: jax.random.normal(k, shape, f) * scale
    return {
        'x': nrm(ks[0], (BATCH, SEQ, D_MODEL), 1.0),
        'meta_tokens': nrm(ks[1], (N_META, D_MODEL), 1.0),
        'mix_norm_g': 1.0 + nrm(ks[2], (DEPTH, D_MODEL), 0.02),
        'w_in': nrm(ks[3], (DEPTH, D_MODEL, D_IN), D_MODEL ** -0.5),
        'ret_norm_g': 1.0 + nrm(ks[4], (DEPTH, RET_HEADS, RET_V_DIM), 0.02),
        'q_norm_g': 1.0 + nrm(ks[5], (DEPTH, SWA_HEAD_DIM), 0.02),
        'k_norm_g': 1.0 + nrm(ks[6], (DEPTH, SWA_HEAD_DIM), 0.02),
        'sinks': nrm(ks[7], (DEPTH, SWA_Q_HEADS), 0.5),
        'w_ret_branch': nrm(ks[8], (DEPTH, RET_V_W, D_MODEL), RET_V_W ** -0.5),
        'w_swa_branch': nrm(ks[9], (DEPTH, SWA_Q_W, D_MODEL), SWA_Q_W ** -0.5),
        'w_out': nrm(ks[10], (DEPTH, D_MODEL, D_MODEL), D_MODEL ** -0.5),
        'mlp_norm_g': 1.0 + nrm(ks[11], (DEPTH, D_MODEL), 0.02),
        'w_up': nrm(ks[12], (DEPTH, D_MODEL, D_FF), D_MODEL ** -0.5),
        'w_down': nrm(ks[13], (DEPTH, D_FF, D_MODEL), D_FF ** -0.5),
    }


def reference(x, meta_tokens, mix_norm_g, w_in, ret_norm_g, q_norm_g, k_norm_g, sinks,
              w_ret_branch, w_swa_branch, w_out, mlp_norm_g, w_up, w_down):
    B = x.shape[0]
    meta = jnp.broadcast_to(meta_tokens[None].astype(x.dtype), (B, N_META, D_MODEL))
    h = jnp.concatenate([meta, x], axis=1)
    for layer in range(DEPTH):
        h = h + mixer_layer(h, mix_norm_g[layer], w_in[layer], ret_norm_g[layer], q_norm_g[layer],
                            k_norm_g[layer], sinks[layer], w_ret_branch[layer], w_swa_branch[layer],
                            w_out[layer])
        hn = rms_norm(h, mlp_norm_g[layer])
        h = h + jnp.square(jax.nn.relu(hn @ w_up[layer])) @ w_down[layer]
    return h[:, N_META:]
```

```python
import functools

import numpy as np
import jax
import jax.numpy as jnp
from jax import lax
from jax.experimental import pallas as pl
from jax.experimental.pallas import tpu as pltpu

D_MODEL = 1024
N_META = 16
CHUNK = 128
PAD = CHUNK - N_META
RET_HEADS = 4
RET_QK_DIM = 128
RET_V_DIM = 256
SWA_Q_HEADS = 16
SWA_KV_HEADS = 4
SWA_GROUP = SWA_Q_HEADS // SWA_KV_HEADS
SWA_HEAD_DIM = 64
D_FF = 4 * D_MODEL
EPS = 1e-6
RET_QK_W = RET_HEADS * RET_QK_DIM
RET_V_W = RET_HEADS * RET_V_DIM
SWA_Q_W = SWA_Q_HEADS * SWA_HEAD_DIM
SWA_KV_W = SWA_KV_HEADS * SWA_HEAD_DIM
D_IN = 2 * RET_QK_W + 2 * RET_V_W + SWA_Q_W + 2 * SWA_KV_W + 2 * D_MODEL

OFF_RQ = 0
OFF_RK = OFF_RQ + RET_QK_W
OFF_RV = OFF_RK + RET_QK_W
OFF_RG = OFF_RV + RET_V_W
OFF_SQ = OFF_RG + RET_V_W
OFF_SK = OFF_SQ + SWA_Q_W
OFF_SV = OFF_SK + SWA_KV_W
OFF_GR = OFF_SV + SWA_KV_W
OFF_GS = OFF_GR + D_MODEL

V7X_VMEM_LIMIT_BYTES = 56 * 1024 * 1024

F32 = jnp.float32
BF16 = jnp.bfloat16
NEG_INF = float("-inf")


def _sigmoid(x):
    return 1.0 / (1.0 + jnp.exp(-x))


def _dot(a, b):
    return jnp.dot(a, b, preferred_element_type=F32)


def _dot_nt(a, b):
    return lax.dot_general(a, b, (((1,), (1,)), ((), ())), preferred_element_type=F32)


IN_N_CHUNK = 512
MIX_W = OFF_GR
GATE_W = 2 * D_MODEL
assert MIX_W % IN_N_CHUNK == 0 and GATE_W % IN_N_CHUNK == 0


def _inproj_kernel(x_ref, g_ref, w_ref, mix_ref, gate_ref):
    x = x_ref[...]
    ms = jnp.mean(x * x, axis=-1, keepdims=True)
    hn = (x * lax.rsqrt(ms + EPS) * g_ref[...]).astype(BF16)
    for c in range(0, MIX_W, IN_N_CHUNK):
        mix_ref[:, c:c + IN_N_CHUNK] = _dot(hn, w_ref[:, c:c + IN_N_CHUNK]).astype(BF16)
    for c in range(0, GATE_W, IN_N_CHUNK):
        gate_ref[:, c:c + IN_N_CHUNK] = _dot(
            hn, w_ref[:, MIX_W + c:MIX_W + c + IN_N_CHUNK]).astype(BF16)


def _inproj(x2d, g, w_bf16, block_rows):
    rows = x2d.shape[0]
    assert rows % block_rows == 0
    return pl.pallas_call(
        _inproj_kernel,
        out_shape=(jax.ShapeDtypeStruct((rows, MIX_W), BF16),
                   jax.ShapeDtypeStruct((rows, GATE_W), BF16)),
        grid=(rows // block_rows,),
        in_specs=[
            pl.BlockSpec((block_rows, D_MODEL), lambda i: (i, 0)),
            pl.BlockSpec((1, D_MODEL), lambda i: (0, 0)),
            pl.BlockSpec((D_MODEL, D_IN), lambda i: (0, 0), pipeline_mode=pl.Buffered(1)),
        ],
        out_specs=(pl.BlockSpec((block_rows, MIX_W), lambda i: (i, 0)),
                   pl.BlockSpec((block_rows, GATE_W), lambda i: (i, 0))),
        compiler_params=pltpu.CompilerParams(
            dimension_semantics=("arbitrary",),
            vmem_limit_bytes=V7X_VMEM_LIMIT_BYTES,
        ),
        name="inproj",
    )(x2d, g, w_bf16)


def _retention_consts():
    log_gamma = np.log1p(-(2.0 ** (-5.0 - np.arange(RET_HEADS, dtype=np.float64))))
    idx = np.arange(CHUNK, dtype=np.float64)
    diff = idx[:, None] - idx[None, :]
    scale = RET_QK_DIM ** -0.5
    intra = np.where(diff >= 0, np.exp(log_gamma[:, None, None] * np.maximum(diff, 0.0)), 0.0)
    q_decay = np.exp(log_gamma[:, None] * (idx + 1.0))
    k_decay = np.exp(log_gamma[:, None] * (CHUNK - 1.0 - idx))
    chunk_decay = np.exp(log_gamma * CHUNK)
    intra_s = (intra * scale).astype(np.float32)
    qd = np.broadcast_to(q_decay[:, :, None], (RET_HEADS, CHUNK, RET_QK_DIM)).astype(np.float32)
    kd = np.broadcast_to((k_decay * scale)[:, :, None], (RET_HEADS, CHUNK, RET_QK_DIM)).astype(np.float32)
    return intra_s, qd, kd, [float(c) for c in chunk_decay]


def _retention_kernel(chunk_decay, rq_ref, rk_ref, rv_ref, rg_ref, mk_ref, mv_ref,
                      intra_ref, qd_ref, kd_ref, g_ref, o_ref, state_ref):
    n = pl.program_id(1)

    @pl.when(n == 0)
    def _():
        for h in range(RET_HEADS):
            km = mk_ref[:, h * RET_QK_DIM:(h + 1) * RET_QK_DIM].astype(F32) * kd_ref[h, PAD:, :]
            km = jnp.concatenate([jnp.zeros((PAD, RET_QK_DIM), F32), km], axis=0)
            vm = mv_ref[:, h * RET_V_DIM:(h + 1) * RET_V_DIM]
            vm = jnp.concatenate([jnp.zeros((PAD, RET_V_DIM), BF16), vm], axis=0)
            state_ref[h] = _dot(km.T.astype(BF16), vm)

    for h in range(RET_HEADS):
        q = rq_ref[:, h * RET_QK_DIM:(h + 1) * RET_QK_DIM]
        k = rk_ref[:, h * RET_QK_DIM:(h + 1) * RET_QK_DIM]
        v = rv_ref[:, h * RET_V_DIM:(h + 1) * RET_V_DIM]
        state = state_ref[h]
        s = _dot_nt(q, k) * intra_ref[h]
        qd = (q.astype(F32) * qd_ref[h]).astype(BF16)
        o = _dot(s.astype(BF16), v) + _dot(qd, state.astype(BF16))
        kd = k.astype(F32) * kd_ref[h]
        state_ref[h] = chunk_decay[h] * state + _dot(kd.T.astype(BF16), v)
        ms = jnp.mean(o * o, axis=-1, keepdims=True)
        gate = rg_ref[:, h * RET_V_DIM:(h + 1) * RET_V_DIM].astype(F32)
        y = o * lax.rsqrt(ms + EPS) * g_ref[:, h * RET_V_DIM:(h + 1) * RET_V_DIM]
        o_ref[:, h * RET_V_DIM:(h + 1) * RET_V_DIM] = (y * (gate * _sigmoid(gate))).astype(BF16)


def _retention(proj, meta_proj, ret_norm_g):
    b, l, _ = proj.shape
    nc = l // CHUNK
    intra_s, qd, kd, chunk_decay = _retention_consts()
    const3 = lambda shape: pl.BlockSpec(shape, lambda bi, ni: (0, 0, 0))
    return pl.pallas_call(
        functools.partial(_retention_kernel, chunk_decay),
        out_shape=jax.ShapeDtypeStruct((b, l, RET_V_W), BF16),
        grid=(b, nc),
        in_specs=[
            pl.BlockSpec((None, CHUNK, RET_QK_W), lambda bi, ni: (bi, ni, OFF_RQ // RET_QK_W)),
            pl.BlockSpec((None, CHUNK, RET_QK_W), lambda bi, ni: (bi, ni, OFF_RK // RET_QK_W)),
            pl.BlockSpec((None, CHUNK, RET_V_W), lambda bi, ni: (bi, ni, OFF_RV // RET_V_W)),
            pl.BlockSpec((None, CHUNK, RET_V_W), lambda bi, ni: (bi, ni, OFF_RG // RET_V_W)),
            pl.BlockSpec((N_META, RET_QK_W), lambda bi, ni: (0, OFF_RK // RET_QK_W)),
            pl.BlockSpec((N_META, RET_V_W), lambda bi, ni: (0, OFF_RV // RET_V_W)),
            const3((RET_HEADS, CHUNK, CHUNK)),
            const3((RET_HEADS, CHUNK, RET_QK_DIM)),
            const3((RET_HEADS, CHUNK, RET_QK_DIM)),
            pl.BlockSpec((1, RET_V_W), lambda bi, ni: (0, 0)),
        ],
        out_specs=pl.BlockSpec((None, CHUNK, RET_V_W), lambda bi, ni: (bi, ni, 0)),
        scratch_shapes=[pltpu.VMEM((RET_HEADS, RET_QK_DIM, RET_V_DIM), F32)],
        compiler_params=pltpu.CompilerParams(
            dimension_semantics=("arbitrary", "arbitrary"),
            vmem_limit_bytes=V7X_VMEM_LIMIT_BYTES,
        ),
        name="retention",
    )(proj, proj, proj, proj, meta_proj, meta_proj,
      jnp.asarray(intra_s), jnp.asarray(qd), jnp.asarray(kd), ret_norm_g.reshape(1, RET_V_W))


GROUP_ROWS = SWA_GROUP * CHUNK


def _swa_consts():
    slopes = 2.0 ** (-8.0 * np.arange(1, SWA_Q_HEADS + 1, dtype=np.float64) / SWA_Q_HEADS)
    slopes = slopes.astype(np.float32)
    i = np.arange(CHUNK)[:, None]
    j = np.arange(CHUNK)[None, :]
    band_dist = np.where(j <= i, i - j, i - j + CHUNK).astype(np.float32)
    meta_dist = (i - j + N_META).astype(np.float32)
    band_bias = np.zeros((SWA_KV_HEADS, GROUP_ROWS, CHUNK), np.float32)
    meta_bias = np.zeros((SWA_KV_HEADS, GROUP_ROWS, CHUNK), np.float32)
    slope_rows = np.zeros((SWA_KV_HEADS, GROUP_ROWS, CHUNK), np.float32)
    for h in range(SWA_Q_HEADS):
        g, r = divmod(h, SWA_GROUP)
        rows = slice(r * CHUNK, (r + 1) * CHUNK)
        band_bias[g, rows] = slopes[h] * band_dist
        mb = slopes[h] * meta_dist
        meta_bias[g, rows] = np.where(j < N_META, mb, np.inf)
        slope_rows[g, rows] = slopes[h] * np.float32(CHUNK)
    return band_bias, meta_bias, slope_rows


def _head_rms(x, g):
    ms = jnp.mean(x * x, axis=-1, keepdims=True)
    return x * lax.rsqrt(ms + EPS) * g


def _swa_kernel(sq_ref, sk_ref, sv_ref, mk_ref, mv_ref, qg_ref, kg_ref, sink_ref,
                band_bias_ref, meta_bias_ref, slope_rows_ref, o_ref,
                kprev_ref, vprev_ref, kmeta_ref, vmeta_ref):
    n = pl.program_id(1)
    kg = kg_ref[...]
    qg = qg_ref[...] * (SWA_HEAD_DIM ** -0.5)

    @pl.when(n == 0)
    def _():
        kprev_ref[...] = jnp.zeros_like(kprev_ref)
        vprev_ref[...] = jnp.zeros_like(vprev_ref)
        kmeta_ref[...] = jnp.zeros_like(kmeta_ref)
        vmeta_ref[...] = jnp.zeros_like(vmeta_ref)
        for g in range(SWA_KV_HEADS):
            cols = slice(g * SWA_HEAD_DIM, (g + 1) * SWA_HEAD_DIM)
            kmeta_ref[0:N_META, cols] = _head_rms(mk_ref[:, cols].astype(F32), kg).astype(BF16)
        vmeta_ref[0:N_META, :] = mv_ref[...]

    row = lax.broadcasted_iota(jnp.int32, (GROUP_ROWS, CHUNK), 0) & (CHUNK - 1)
    col = lax.broadcasted_iota(jnp.int32, (GROUP_ROWS, CHUNK), 1)
    own = col <= row
    prev_off = jnp.where(n > 0, 0.0, NEG_INF).astype(F32)
    chunk_idx = n.astype(F32)

    k_new = []
    for g in range(SWA_KV_HEADS):
        cols = slice(g * SWA_HEAD_DIM, (g + 1) * SWA_HEAD_DIM)
        kn = _head_rms(sk_ref[:, cols].astype(F32), kg).astype(BF16)
        k_new.append(kn)
        vn = sv_ref[:, cols]
        q = jnp.concatenate(
            [_head_rms(sq_ref[:, (g * SWA_GROUP + r) * SWA_HEAD_DIM:(g * SWA_GROUP + r + 1) * SWA_HEAD_DIM]
                       .astype(F32), qg).astype(BF16) for r in range(SWA_GROUP)], axis=0)
        s_own = _dot_nt(q, kn)
        s_prev = _dot_nt(q, kprev_ref[:, cols]) + prev_off
        s = jnp.where(own, s_own, s_prev) - band_bias_ref[g]
        s_meta = _dot_nt(q, kmeta_ref[:, cols]) - (meta_bias_ref[g] + slope_rows_ref[g] * chunk_idx)
        sink = sink_ref[g]
        m = jnp.maximum(jnp.maximum(jnp.max(s, axis=-1, keepdims=True),
                                    jnp.max(s_meta, axis=-1, keepdims=True)), sink)
        p = jnp.exp(s - m)
        pm = jnp.exp(s_meta - m)
        denom = (jnp.sum(p, axis=-1, keepdims=True) + jnp.sum(pm, axis=-1, keepdims=True)
                 + jnp.exp(sink - m))
        acc = (_dot(jnp.where(own, p, 0.0).astype(BF16), vn)
               + _dot(jnp.where(own, 0.0, p).astype(BF16), vprev_ref[:, cols])
               + _dot(pm.astype(BF16), vmeta_ref[:, cols]))
        out = acc / denom
        for r in range(SWA_GROUP):
            h = g * SWA_GROUP + r
            o_ref[:, h * SWA_HEAD_DIM:(h + 1) * SWA_HEAD_DIM] = out[r * CHUNK:(r + 1) * CHUNK].astype(BF16)

    kprev_ref[...] = jnp.concatenate(k_new, axis=1)
    vprev_ref[...] = sv_ref[...]


def _swa(proj, meta_proj, q_norm_g, k_norm_g, sinks):
    b, l, _ = proj.shape
    nc = l // CHUNK
    band_bias, meta_bias, slope_rows = _swa_consts()
    sink_rows = jnp.broadcast_to(
        sinks.astype(F32).reshape(SWA_KV_HEADS, SWA_GROUP, 1, 1),
        (SWA_KV_HEADS, SWA_GROUP, CHUNK, 1)).reshape(SWA_KV_HEADS, GROUP_ROWS, 1)
    const3 = lambda shape: pl.BlockSpec(shape, lambda bi, ni: (0, 0, 0))
    return pl.pallas_call(
        _swa_kernel,
        out_shape=jax.ShapeDtypeStruct((b, l, SWA_Q_W), BF16),
        grid=(b, nc),
        in_specs=[
            pl.BlockSpec((None, CHUNK, SWA_Q_W), lambda bi, ni: (bi, ni, OFF_SQ // SWA_Q_W)),
            pl.BlockSpec((None, CHUNK, SWA_KV_W), lambda bi, ni: (bi, ni, OFF_SK // SWA_KV_W)),
            pl.BlockSpec((None, CHUNK, SWA_KV_W), lambda bi, ni: (bi, ni, OFF_SV // SWA_KV_W)),
            pl.BlockSpec((N_META, SWA_KV_W), lambda bi, ni: (0, OFF_SK // SWA_KV_W)),
            pl.BlockSpec((N_META, SWA_KV_W), lambda bi, ni: (0, OFF_SV // SWA_KV_W)),
            pl.BlockSpec((1, SWA_HEAD_DIM), lambda bi, ni: (0, 0)),
            pl.BlockSpec((1, SWA_HEAD_DIM), lambda bi, ni: (0, 0)),
            const3((SWA_KV_HEADS, GROUP_ROWS, 1)),
            const3((SWA_KV_HEADS, GROUP_ROWS, CHUNK)),
            const3((SWA_KV_HEADS, GROUP_ROWS, CHUNK)),
            const3((SWA_KV_HEADS, GROUP_ROWS, CHUNK)),
        ],
        out_specs=pl.BlockSpec((None, CHUNK, SWA_Q_W), lambda bi, ni: (bi, ni, 0)),
        scratch_shapes=[
            pltpu.VMEM((CHUNK, SWA_KV_W), BF16),
            pltpu.VMEM((CHUNK, SWA_KV_W), BF16),
            pltpu.VMEM((CHUNK, SWA_KV_W), BF16),
            pltpu.VMEM((CHUNK, SWA_KV_W), BF16),
        ],
        compiler_params=pltpu.CompilerParams(
            dimension_semantics=("arbitrary", "arbitrary"),
            vmem_limit_bytes=V7X_VMEM_LIMIT_BYTES,
        ),
        name="swa",
    )(proj, proj, proj, meta_proj, meta_proj,
      q_norm_g.reshape(1, SWA_HEAD_DIM).astype(F32), k_norm_g.reshape(1, SWA_HEAD_DIM).astype(F32),
      sink_rows, jnp.asarray(band_bias), jnp.asarray(meta_bias), jnp.asarray(slope_rows))


FF_CHUNK = 1024
MERGE_ROWS = 512


def _merge_mlp_kernel(x_ref, yr_ref, ys_ref, gr_ref, gs_ref, wr_ref, ws_ref, wo_ref,
                      g_ref, wu_ref, wd_ref, o_ref):
    a = _dot(yr_ref[...], wr_ref[...])
    b = _dot(ys_ref[...], ws_ref[...])
    merged = _sigmoid(gr_ref[...].astype(F32)) * a + _sigmoid(gs_ref[...].astype(F32)) * b
    h1 = x_ref[...] + _dot(merged.astype(BF16), wo_ref[...])
    ms = jnp.mean(h1 * h1, axis=-1, keepdims=True)
    hn = (h1 * lax.rsqrt(ms + EPS) * g_ref[...]).astype(BF16)
    acc = h1
    for c in range(0, D_FF, FF_CHUNK):
        u = jnp.maximum(_dot(hn, wu_ref[:, c:c + FF_CHUNK]), 0.0)
        acc = acc + _dot((u * u).astype(BF16), wd_ref[c:c + FF_CHUNK, :])
    o_ref[...] = acc


def _merge_mlp(x2d, y_ret, y_swa, gates2d, w_ret, w_swa, w_out, mlp_g, w_up, w_down):
    rows = x2d.shape[0]
    assert rows % MERGE_ROWS == 0
    row_spec = lambda w: pl.BlockSpec((MERGE_ROWS, w), lambda i: (i, 0))
    resident = lambda shape: pl.BlockSpec(shape, lambda i: (0, 0), pipeline_mode=pl.Buffered(1))
    return pl.pallas_call(
        _merge_mlp_kernel,
        out_shape=jax.ShapeDtypeStruct((rows, D_MODEL), F32),
        grid=(rows // MERGE_ROWS,),
        in_specs=[
            row_spec(D_MODEL),
            row_spec(RET_V_W),
            row_spec(SWA_Q_W),
            pl.BlockSpec((MERGE_ROWS, D_MODEL), lambda i: (i, 0)),
            pl.BlockSpec((MERGE_ROWS, D_MODEL), lambda i: (i, 1)),
            resident((RET_V_W, D_MODEL)),
            resident((SWA_Q_W, D_MODEL)),
            resident((D_MODEL, D_MODEL)),
            pl.BlockSpec((1, D_MODEL), lambda i: (0, 0)),
            resident((D_MODEL, D_FF)),
            resident((D_FF, D_MODEL)),
        ],
        out_specs=row_spec(D_MODEL),
        compiler_params=pltpu.CompilerParams(
            dimension_semantics=("arbitrary",),
            vmem_limit_bytes=V7X_VMEM_LIMIT_BYTES,
        ),
        name="merge_mlp",
    )(x2d, y_ret, y_swa, gates2d, gates2d, w_ret, w_swa, w_out, mlp_g, w_up, w_down)


INPROJ_ROWS = 512


def kernel(x, meta_tokens, mix_norm_g, w_in, ret_norm_g, q_norm_g, k_norm_g, sinks,
           w_ret_branch, w_swa_branch, w_out, mlp_norm_g, w_up, w_down):
    b, l, d = x.shape
    assert d == D_MODEL and l % CHUNK == 0 and mix_norm_g.shape[0] == 1
    x2d = x.reshape(b * l, d)
    mix_g = mix_norm_g[0].reshape(1, d).astype(F32)
    w_in_b = w_in[0].astype(BF16)

    mix2d, gates2d = _inproj(x2d, mix_g, w_in_b, INPROJ_ROWS)
    meta_proj, _ = _inproj(meta_tokens.astype(F32), mix_g, w_in_b, N_META)
    proj = mix2d.reshape(b, l, MIX_W)

    y_ret = _retention(proj, meta_proj, ret_norm_g[0].astype(F32))
    y_swa = _swa(proj, meta_proj, q_norm_g[0], k_norm_g[0], sinks[0])

    out = _merge_mlp(
        x2d, y_ret.reshape(b * l, RET_V_W), y_swa.reshape(b * l, SWA_Q_W), gates2d,
        w_ret_branch[0].astype(BF16), w_swa_branch[0].astype(BF16), w_out[0].astype(BF16),
        mlp_norm_g[0].reshape(1, d).astype(F32), w_up[0].astype(BF16), w_down[0].astype(BF16))
    return out.reshape(b, l, d)
```

```python
import functools

import numpy as np
import jax
import jax.numpy as jnp
from jax import lax
from jax.experimental import pallas as pl
from jax.experimental.pallas import tpu as pltpu

D_MODEL = 1024
N_META = 16
CHUNK = 128
PAD = CHUNK - N_META
RET_HEADS = 4
RET_QK_DIM = 128
RET_V_DIM = 256
SWA_Q_HEADS = 16
SWA_KV_HEADS = 4
SWA_GROUP = SWA_Q_HEADS // SWA_KV_HEADS
SWA_HEAD_DIM = 64
D_FF = 4 * D_MODEL
EPS = 1e-6
RET_QK_W = RET_HEADS * RET_QK_DIM
RET_V_W = RET_HEADS * RET_V_DIM
SWA_Q_W = SWA_Q_HEADS * SWA_HEAD_DIM
SWA_KV_W = SWA_KV_HEADS * SWA_HEAD_DIM
D_IN = 2 * RET_QK_W + 2 * RET_V_W + SWA_Q_W + 2 * SWA_KV_W + 2 * D_MODEL

OFF_RQ = 0
OFF_RK = OFF_RQ + RET_QK_W
OFF_RV = OFF_RK + RET_QK_W
OFF_RG = OFF_RV + RET_V_W
OFF_SQ = OFF_RG + RET_V_W
OFF_SK = OFF_SQ + SWA_Q_W
OFF_SV = OFF_SK + SWA_KV_W
OFF_GR = OFF_SV + SWA_KV_W
OFF_GS = OFF_GR + D_MODEL

V7X_VMEM_LIMIT_BYTES = 56 * 1024 * 1024
LANES = 128
assert LANES == 2 * SWA_HEAD_DIM and CHUNK == LANES

F32 = jnp.float32
BF16 = jnp.bfloat16


def _sigmoid(x):
    return 1.0 / (1.0 + jnp.exp(-x))


def _dot(a, b):
    return jnp.dot(a, b, preferred_element_type=F32)


def _dot_nt(a, b):
    return lax.dot_general(a, b, (((1,), (1,)), ((), ())), preferred_element_type=F32)


IN_N_CHUNK = 512
MIX_W = OFF_GR
GATE_W = 2 * D_MODEL
assert OFF_SQ % IN_N_CHUNK == 0 and SWA_Q_W % IN_N_CHUNK == 0 and GATE_W % IN_N_CHUNK == 0


def _head_pair_rms(y, g2):
    rows, width = y.shape
    low = lax.broadcasted_iota(jnp.int32, (rows, LANES), 1) < SWA_HEAD_DIM
    out = []
    for c in range(0, width, LANES):
        yb = y[:, c:c + LANES]
        y2 = yb * yb
        ms_lo = jnp.sum(jnp.where(low, y2, 0.0), axis=-1, keepdims=True) * (1.0 / SWA_HEAD_DIM)
        ms_hi = jnp.sum(jnp.where(low, 0.0, y2), axis=-1, keepdims=True) * (1.0 / SWA_HEAD_DIM)
        scale = jnp.where(low, lax.rsqrt(ms_lo + EPS), lax.rsqrt(ms_hi + EPS))
        out.append(yb * scale * g2)
    return jnp.concatenate(out, axis=1) if len(out) > 1 else out[0]


def _inproj_kernel(x_ref, g_ref, qg_ref, kg_ref, w_ref, mix_ref, gate_ref):
    x = x_ref[...]
    ms = jnp.mean(x * x, axis=-1, keepdims=True)
    hn = (x * lax.rsqrt(ms + EPS) * g_ref[...]).astype(BF16)

    def proj(lo, hi):
        return _dot(hn, w_ref[:, lo:hi])

    for c in range(0, OFF_SQ, IN_N_CHUNK):
        mix_ref[:, c:c + IN_N_CHUNK] = proj(c, c + IN_N_CHUNK).astype(BF16)
    for c in range(OFF_SQ, OFF_SK, IN_N_CHUNK):
        mix_ref[:, c:c + IN_N_CHUNK] = _head_pair_rms(proj(c, c + IN_N_CHUNK), qg_ref[...]).astype(BF16)
    mix_ref[:, OFF_SK:OFF_SV] = _head_pair_rms(proj(OFF_SK, OFF_SV), kg_ref[...]).astype(BF16)
    mix_ref[:, OFF_SV:MIX_W] = proj(OFF_SV, MIX_W).astype(BF16)
    for c in range(0, GATE_W, IN_N_CHUNK):
        gate_ref[:, c:c + IN_N_CHUNK] = proj(MIX_W + c, MIX_W + c + IN_N_CHUNK).astype(BF16)


def _inproj(x2d, g, qg2, kg2, w_bf16, block_rows):
    rows = x2d.shape[0]
    assert rows % block_rows == 0
    return pl.pallas_call(
        _inproj_kernel,
        out_shape=(jax.ShapeDtypeStruct((rows, MIX_W), BF16),
                   jax.ShapeDtypeStruct((rows, GATE_W), BF16)),
        grid=(rows // block_rows,),
        in_specs=[
            pl.BlockSpec((block_rows, D_MODEL), lambda i: (i, 0)),
            pl.BlockSpec((1, D_MODEL), lambda i: (0, 0)),
            pl.BlockSpec((1, LANES), lambda i: (0, 0)),
            pl.BlockSpec((1, LANES), lambda i: (0, 0)),
            pl.BlockSpec((D_MODEL, D_IN), lambda i: (0, 0), pipeline_mode=pl.Buffered(1)),
        ],
        out_specs=(pl.BlockSpec((block_rows, MIX_W), lambda i: (i, 0)),
                   pl.BlockSpec((block_rows, GATE_W), lambda i: (i, 0))),
        compiler_params=pltpu.CompilerParams(
            dimension_semantics=("arbitrary",),
            vmem_limit_bytes=V7X_VMEM_LIMIT_BYTES,
        ),
        name="inproj",
    )(x2d, g, qg2, kg2, w_bf16)


def _retention_consts():
    log_gamma = np.log1p(-(2.0 ** (-5.0 - np.arange(RET_HEADS, dtype=np.float64))))
    idx = np.arange(CHUNK, dtype=np.float64)
    diff = idx[:, None] - idx[None, :]
    scale = RET_QK_DIM ** -0.5
    intra = np.where(diff >= 0, np.exp(log_gamma[:, None, None] * np.maximum(diff, 0.0)), 0.0)
    q_decay = np.exp(log_gamma[:, None] * (idx + 1.0))
    k_decay = np.exp(log_gamma[:, None] * (CHUNK - 1.0 - idx))
    chunk_decay = np.exp(log_gamma * CHUNK)
    intra_s = (intra * scale).astype(np.float32)
    qd = np.broadcast_to(q_decay[:, :, None], (RET_HEADS, CHUNK, RET_QK_DIM)).astype(np.float32)
    kd = np.broadcast_to((k_decay * scale)[:, :, None], (RET_HEADS, CHUNK, RET_QK_DIM)).astype(np.float32)
    return intra_s, qd, kd, [float(c) for c in chunk_decay]


def _retention_kernel(chunk_decay, rq_ref, rk_ref, rv_ref, rg_ref, mk_ref, mv_ref,
                      intra_ref, qd_ref, kd_ref, g_ref, o_ref, state_ref):
    n = pl.program_id(1)

    @pl.when(n == 0)
    def _():
        for h in range(RET_HEADS):
            km = mk_ref[:, h * RET_QK_DIM:(h + 1) * RET_QK_DIM].astype(F32) * kd_ref[h, PAD:, :]
            km = jnp.concatenate([jnp.zeros((PAD, RET_QK_DIM), F32), km], axis=0)
            vm = mv_ref[:, h * RET_V_DIM:(h + 1) * RET_V_DIM]
            vm = jnp.concatenate([jnp.zeros((PAD, RET_V_DIM), BF16), vm], axis=0)
            state_ref[h] = _dot(km.T.astype(BF16), vm)

    for h in range(RET_HEADS):
        q = rq_ref[:, h * RET_QK_DIM:(h + 1) * RET_QK_DIM]
        k = rk_ref[:, h * RET_QK_DIM:(h + 1) * RET_QK_DIM]
        v = rv_ref[:, h * RET_V_DIM:(h + 1) * RET_V_DIM]
        state = state_ref[h]
        s = _dot_nt(q, k) * intra_ref[h]
        qd = (q.astype(F32) * qd_ref[h]).astype(BF16)
        o = _dot(s.astype(BF16), v) + _dot(qd, state.astype(BF16))
        kd = k.astype(F32) * kd_ref[h]
        state_ref[h] = chunk_decay[h] * state + _dot(kd.T.astype(BF16), v)
        ms = jnp.mean(o * o, axis=-1, keepdims=True)
        gate = rg_ref[:, h * RET_V_DIM:(h + 1) * RET_V_DIM].astype(F32)
        y = o * lax.rsqrt(ms + EPS) * g_ref[:, h * RET_V_DIM:(h + 1) * RET_V_DIM]
        o_ref[:, h * RET_V_DIM:(h + 1) * RET_V_DIM] = (y * (gate * _sigmoid(gate))).astype(BF16)


def _retention(proj, meta_proj, ret_norm_g):
    b, l, _ = proj.shape
    nc = l // CHUNK
    intra_s, qd, kd, chunk_decay = _retention_consts()
    const3 = lambda shape: pl.BlockSpec(shape, lambda bi, ni: (0, 0, 0))
    return pl.pallas_call(
        functools.partial(_retention_kernel, chunk_decay),
        out_shape=jax.ShapeDtypeStruct((b, l, RET_V_W), BF16),
        grid=(b, nc),
        in_specs=[
            pl.BlockSpec((None, CHUNK, RET_QK_W), lambda bi, ni: (bi, ni, OFF_RQ // RET_QK_W)),
            pl.BlockSpec((None, CHUNK, RET_QK_W), lambda bi, ni: (bi, ni, OFF_RK // RET_QK_W)),
            pl.BlockSpec((None, CHUNK, RET_V_W), lambda bi, ni: (bi, ni, OFF_RV // RET_V_W)),
            pl.BlockSpec((None, CHUNK, RET_V_W), lambda bi, ni: (bi, ni, OFF_RG // RET_V_W)),
            pl.BlockSpec((N_META, RET_QK_W), lambda bi, ni: (0, OFF_RK // RET_QK_W)),
            pl.BlockSpec((N_META, RET_V_W), lambda bi, ni: (0, OFF_RV // RET_V_W)),
            const3((RET_HEADS, CHUNK, CHUNK)),
            const3((RET_HEADS, CHUNK, RET_QK_DIM)),
            const3((RET_HEADS, CHUNK, RET_QK_DIM)),
            pl.BlockSpec((1, RET_V_W), lambda bi, ni: (0, 0)),
        ],
        out_specs=pl.BlockSpec((None, CHUNK, RET_V_W), lambda bi, ni: (bi, ni, 0)),
        scratch_shapes=[pltpu.VMEM((RET_HEADS, RET_QK_DIM, RET_V_DIM), F32)],
        compiler_params=pltpu.CompilerParams(
            dimension_semantics=("arbitrary", "arbitrary"),
            vmem_limit_bytes=V7X_VMEM_LIMIT_BYTES,
        ),
        name="retention",
    )(proj, proj, proj, proj, meta_proj, meta_proj,
      jnp.asarray(intra_s), jnp.asarray(qd), jnp.asarray(kd), ret_norm_g.reshape(1, RET_V_W))


PAIR_ROWS = 2 * CHUNK
PV_K = 3 * CHUNK


def _swa_consts():
    slopes = 2.0 ** (-8.0 * np.arange(1, SWA_Q_HEADS + 1, dtype=np.float64) / SWA_Q_HEADS)
    slopes = slopes.astype(np.float32)
    i = np.arange(CHUNK)[:, None]
    j = np.arange(CHUNK)[None, :]
    band_dist = np.where(j <= i, i - j, i - j + CHUNK).astype(np.float32)
    meta_dist = (i - (j % N_META) + N_META).astype(np.float32)
    band_bias = np.zeros((2, SWA_KV_HEADS, PAIR_ROWS, 2 * CHUNK), np.float32)
    meta_bias = np.full((SWA_KV_HEADS, PAIR_ROWS, CHUNK), np.inf, np.float32)
    meta_slope = np.zeros((SWA_KV_HEADS, PAIR_ROWS, CHUNK), np.float32)
    for h in range(SWA_Q_HEADS):
        g, r = divmod(h, SWA_GROUP)
        p, e = divmod(r, 2)
        rows = slice(p * CHUNK, (p + 1) * CHUNK)
        cols = slice(e * CHUNK, (e + 1) * CHUNK)
        bias = slopes[h] * band_dist
        band_bias[1, g, rows, cols] = bias
        band_bias[0, g, rows, cols] = np.where(j <= i, bias, np.inf)
        mcols = slice(e * N_META, (e + 1) * N_META)
        meta_bias[g, rows, mcols] = (slopes[h] * meta_dist)[:, mcols]
        meta_slope[g, rows, mcols] = slopes[h] * np.float32(CHUNK)
    return band_bias, meta_bias, meta_slope


def _swa_kernel(sink_ref, sq_ref, sk_ref, sv_ref, mk_ref, mv_ref,
                band_bias_ref, meta_bias_ref, meta_slope_ref, o_ref,
                bd_ref, vbuf_ref, bdm_ref, vm_ref):
    n = pl.program_id(1)
    slot = n & 1
    pslot = 1 - slot
    zeros_head = jnp.zeros((CHUNK, SWA_HEAD_DIM), BF16)

    @pl.when(n == 0)
    def _():
        bd_ref[...] = jnp.zeros_like(bd_ref)
        vbuf_ref[...] = jnp.zeros_like(vbuf_ref)
        bdm_ref[...] = jnp.zeros_like(bdm_ref)
        vm_ref[...] = jnp.zeros_like(vm_ref)
        zm = jnp.zeros((N_META, SWA_HEAD_DIM), BF16)
        for g in range(SWA_KV_HEADS):
            cols = slice(g * SWA_HEAD_DIM, (g + 1) * SWA_HEAD_DIM)
            km = mk_ref[:, cols]
            bdm_ref[g, 0:N_META, :] = jnp.concatenate([km, zm], axis=1)
            bdm_ref[g, N_META:2 * N_META, :] = jnp.concatenate([zm, km], axis=1)
            vm_ref[g, 0:N_META, :] = mv_ref[:, cols]
            vm_ref[g, N_META:2 * N_META, :] = mv_ref[:, cols]

    for g in range(SWA_KV_HEADS):
        kg = sk_ref[:, g * SWA_HEAD_DIM:(g + 1) * SWA_HEAD_DIM]
        bd_ref[slot, g, 0:CHUNK, :] = jnp.concatenate([kg, zeros_head], axis=1)
        bd_ref[slot, g, CHUNK:2 * CHUNK, :] = jnp.concatenate([zeros_head, kg], axis=1)
    vbuf_ref[slot] = sv_ref[...]

    row = lax.broadcasted_iota(jnp.int32, (CHUNK, CHUNK), 0)
    col = lax.broadcasted_iota(jnp.int32, (CHUNK, CHUNK), 1)
    own = col <= row
    own2 = jnp.concatenate([own, own], axis=1)
    own4 = jnp.concatenate([own2, own2], axis=0)
    meta_seg = [col < N_META, (col >= N_META) & (col < 2 * N_META)]
    bias_sel = jnp.minimum(n, 1)
    chunk_idx = n.astype(F32)
    neg_inf = jnp.float32(-jnp.inf)

    for g in range(SWA_KV_HEADS):
        q2 = jnp.concatenate([sq_ref[:, (2 * g + p) * LANES:(2 * g + p + 1) * LANES] for p in range(2)],
                             axis=0)
        s_own = _dot_nt(q2, bd_ref[slot, g])
        s_prev = _dot_nt(q2, bd_ref[pslot, g])
        s = jnp.where(own4, s_own, s_prev) - band_bias_ref[bias_sel, g]
        s_meta = _dot_nt(q2, bdm_ref[g]) - (meta_bias_ref[g] + meta_slope_ref[g] * chunk_idx)

        lhs = []
        inv = []
        for p in range(2):
            sm_p = s_meta[p * CHUNK:(p + 1) * CHUNK]
            for e in range(2):
                h = g * SWA_GROUP + 2 * p + e
                sink = sink_ref[h]
                sb = s[p * CHUNK:(p + 1) * CHUNK, e * CHUNK:(e + 1) * CHUNK]
                sm = jnp.where(meta_seg[e], sm_p, neg_inf)
                m = jnp.maximum(jnp.max(jnp.maximum(sb, sm), axis=-1, keepdims=True), sink)
                pb = jnp.exp(sb - m)
                pm = jnp.exp(sm - m)
                denom = jnp.sum(pb + pm, axis=-1, keepdims=True) + jnp.exp(sink - m)
                inv.append(1.0 / denom)
                lhs.append(jnp.concatenate(
                    [jnp.where(own, pb, 0.0).astype(BF16), jnp.where(own, 0.0, pb).astype(BF16),
                     pm.astype(BF16)], axis=1))
        cols = slice(g * SWA_HEAD_DIM, (g + 1) * SWA_HEAD_DIM)
        vv = jnp.concatenate([vbuf_ref[slot, :, cols], vbuf_ref[pslot, :, cols], vm_ref[g]], axis=0)
        acc = _dot(jnp.concatenate(lhs, axis=0), vv)
        for r in range(SWA_GROUP):
            h = g * SWA_GROUP + r
            o_ref[:, h * SWA_HEAD_DIM:(h + 1) * SWA_HEAD_DIM] = (
                acc[r * CHUNK:(r + 1) * CHUNK] * inv[r]).astype(BF16)


def _swa(proj, meta_proj, sinks):
    b, l, _ = proj.shape
    nc = l // CHUNK
    band_bias, meta_bias, meta_slope = _swa_consts()
    const3 = lambda shape: pl.BlockSpec(shape, lambda bi, ni, s: (0, 0, 0))
    grid_spec = pltpu.PrefetchScalarGridSpec(
        num_scalar_prefetch=1,
        grid=(b, nc),
        in_specs=[
            pl.BlockSpec((None, CHUNK, SWA_Q_W), lambda bi, ni, s: (bi, ni, OFF_SQ // SWA_Q_W)),
            pl.BlockSpec((None, CHUNK, SWA_KV_W), lambda bi, ni, s: (bi, ni, OFF_SK // SWA_KV_W)),
            pl.BlockSpec((None, CHUNK, SWA_KV_W), lambda bi, ni, s: (bi, ni, OFF_SV // SWA_KV_W)),
            pl.BlockSpec((N_META, SWA_KV_W), lambda bi, ni, s: (0, OFF_SK // SWA_KV_W)),
            pl.BlockSpec((N_META, SWA_KV_W), lambda bi, ni, s: (0, OFF_SV // SWA_KV_W)),
            pl.BlockSpec((2, SWA_KV_HEADS, PAIR_ROWS, 2 * CHUNK), lambda bi, ni, s: (0, 0, 0, 0)),
            const3((SWA_KV_HEADS, PAIR_ROWS, CHUNK)),
            const3((SWA_KV_HEADS, PAIR_ROWS, CHUNK)),
        ],
        out_specs=pl.BlockSpec((None, CHUNK, SWA_Q_W), lambda bi, ni, s: (bi, ni, 0)),
        scratch_shapes=[
            pltpu.VMEM((2, SWA_KV_HEADS, PAIR_ROWS, LANES), BF16),
            pltpu.VMEM((2, CHUNK, SWA_KV_W), BF16),
            pltpu.VMEM((SWA_KV_HEADS, CHUNK, LANES), BF16),
            pltpu.VMEM((SWA_KV_HEADS, CHUNK, SWA_HEAD_DIM), BF16),
        ],
    )
    return pl.pallas_call(
        _swa_kernel,
        out_shape=jax.ShapeDtypeStruct((b, l, SWA_Q_W), BF16),
        grid_spec=grid_spec,
        compiler_params=pltpu.CompilerParams(
            dimension_semantics=("arbitrary", "arbitrary"),
            vmem_limit_bytes=V7X_VMEM_LIMIT_BYTES,
        ),
        name="swa",
    )(sinks.astype(F32), proj, proj, proj, meta_proj, meta_proj,
      jnp.asarray(band_bias), jnp.asarray(meta_bias), jnp.asarray(meta_slope))


FF_CHUNK = 1024
MERGE_ROWS = 512


def _merge_mlp_kernel(x_ref, yr_ref, ys_ref, gr_ref, gs_ref, wr_ref, ws_ref, wo_ref,
                      g_ref, wu_ref, wd_ref, o_ref):
    a = _dot(yr_ref[...], wr_ref[...])
    b = _dot(ys_ref[...], ws_ref[...])
    merged = _sigmoid(gr_ref[...].astype(F32)) * a + _sigmoid(gs_ref[...].astype(F32)) * b
    h1 = x_ref[...] + _dot(merged.astype(BF16), wo_ref[...])
    ms = jnp.mean(h1 * h1, axis=-1, keepdims=True)
    hn = (h1 * lax.rsqrt(ms + EPS) * g_ref[...]).astype(BF16)
    acc = h1
    for c in range(0, D_FF, FF_CHUNK):
        u = jnp.maximum(_dot(hn, wu_ref[:, c:c + FF_CHUNK]), 0.0)
        acc = acc + _dot((u * u).astype(BF16), wd_ref[c:c + FF_CHUNK, :])
    o_ref[...] = acc


def _merge_mlp(x2d, y_ret, y_swa, gates2d, w_ret, w_swa, w_out, mlp_g, w_up, w_down):
    rows = x2d.shape[0]
    assert rows % MERGE_ROWS == 0
    row_spec = lambda w: pl.BlockSpec((MERGE_ROWS, w), lambda i: (i, 0))
    resident = lambda shape: pl.BlockSpec(shape, lambda i: (0, 0), pipeline_mode=pl.Buffered(1))
    return pl.pallas_call(
        _merge_mlp_kernel,
        out_shape=jax.ShapeDtypeStruct((rows, D_MODEL), F32),
        grid=(rows // MERGE_ROWS,),
        in_specs=[
            row_spec(D_MODEL),
            row_spec(RET_V_W),
            row_spec(SWA_Q_W),
            pl.BlockSpec((MERGE_ROWS, D_MODEL), lambda i: (i, 0)),
            pl.BlockSpec((MERGE_ROWS, D_MODEL), lambda i: (i, 1)),
            resident((RET_V_W, D_MODEL)),
            resident((SWA_Q_W, D_MODEL)),
            resident((D_MODEL, D_MODEL)),
            pl.BlockSpec((1, D_MODEL), lambda i: (0, 0)),
            resident((D_MODEL, D_FF)),
            resident((D_FF, D_MODEL)),
        ],
        out_specs=row_spec(D_MODEL),
        compiler_params=pltpu.CompilerParams(
            dimension_semantics=("arbitrary",),
            vmem_limit_bytes=V7X_VMEM_LIMIT_BYTES,
        ),
        name="merge_mlp",
    )(x2d, y_ret, y_swa, gates2d, gates2d, w_ret, w_swa, w_out, mlp_g, w_up, w_down)


INPROJ_ROWS = 512


def kernel(x, meta_tokens, mix_norm_g, w_in, ret_norm_g, q_norm_g, k_norm_g, sinks,
           w_ret_branch, w_swa_branch, w_out, mlp_norm_g, w_up, w_down):
    b, l, d = x.shape
    assert d == D_MODEL and l % CHUNK == 0 and mix_norm_g.shape[0] == 1
    x2d = x.reshape(b * l, d)
    mix_g = mix_norm_g[0].reshape(1, d).astype(F32)
    w_in_b = w_in[0].astype(BF16)
    qg2 = jnp.tile(q_norm_g[0].astype(F32) * (SWA_HEAD_DIM ** -0.5), 2).reshape(1, LANES)
    kg2 = jnp.tile(k_norm_g[0].astype(F32), 2).reshape(1, LANES)

    mix2d, gates2d = _inproj(x2d, mix_g, qg2, kg2, w_in_b, INPROJ_ROWS)
    meta_proj, _ = _inproj(meta_tokens.astype(F32), mix_g, qg2, kg2, w_in_b, N_META)
    proj = mix2d.reshape(b, l, MIX_W)

    y_ret = _retention(proj, meta_proj, ret_norm_g[0].astype(F32))
    y_swa = _swa(proj, meta_proj, sinks[0])

    out = _merge_mlp(
        x2d, y_ret.reshape(b * l, RET_V_W), y_swa.reshape(b * l, SWA_Q_W), gates2d,
        w_ret_branch[0].astype(BF16), w_swa_branch[0].astype(BF16), w_out[0].astype(BF16),
        mlp_norm_g[0].reshape(1, d).astype(F32), w_up[0].astype(BF16), w_down[0].astype(BF16))
    return out.reshape(b, l, d)
```

```python
import functools

import numpy as np
import jax
import jax.numpy as jnp
from jax import lax
from jax.experimental import pallas as pl
from jax.experimental.pallas import tpu as pltpu

D_MODEL = 1024
N_META = 16
CHUNK = 128
PAD = CHUNK - N_META
RET_HEADS = 4
RET_QK_DIM = 128
RET_V_DIM = 256
SWA_Q_HEADS = 16
SWA_KV_HEADS = 4
SWA_GROUP = SWA_Q_HEADS // SWA_KV_HEADS
SWA_HEAD_DIM = 64
D_FF = 4 * D_MODEL
EPS = 1e-6
RET_QK_W = RET_HEADS * RET_QK_DIM
RET_V_W = RET_HEADS * RET_V_DIM
SWA_Q_W = SWA_Q_HEADS * SWA_HEAD_DIM
SWA_KV_W = SWA_KV_HEADS * SWA_HEAD_DIM
D_IN = 2 * RET_QK_W + 2 * RET_V_W + SWA_Q_W + 2 * SWA_KV_W + 2 * D_MODEL

OFF_RQ = 0
OFF_RK = OFF_RQ + RET_QK_W
OFF_RV = OFF_RK + RET_QK_W
OFF_RG = OFF_RV + RET_V_W
OFF_SQ = OFF_RG + RET_V_W
OFF_SK = OFF_SQ + SWA_Q_W
OFF_SV = OFF_SK + SWA_KV_W
OFF_GR = OFF_SV + SWA_KV_W
OFF_GS = OFF_GR + D_MODEL

W_SQ = 0
W_RV = W_SQ + SWA_Q_W
W_RQ = W_RV + RET_V_W
W_RK = W_RQ + RET_QK_W
W_SK = W_RK + RET_QK_W
W_SV = W_SK + SWA_KV_W
W_GATES = W_SV + SWA_KV_W
MIX_SQ = 0
MIX_RV = MIX_SQ + SWA_Q_W
MIX_RQ = MIX_RV + RET_V_W
MIX_SK = MIX_RQ + RET_QK_W
MIX_SV = MIX_SK + SWA_KV_W
MIX_W = MIX_SV + SWA_KV_W
GATE_W = RET_V_W + 2 * D_MODEL

V7X_VMEM_LIMIT_BYTES = 56 * 1024 * 1024
LANES = 128
assert LANES == 2 * SWA_HEAD_DIM and CHUNK == LANES

F32 = jnp.float32
BF16 = jnp.bfloat16


def _sigmoid(x):
    return 1.0 / (1.0 + jnp.exp(-x))


def _dot(a, b):
    return jnp.dot(a, b, preferred_element_type=F32)


def _dot_nt(a, b):
    return lax.dot_general(a, b, (((1,), (1,)), ((), ())), preferred_element_type=F32)


def _retention_log_gamma():
    return np.log1p(-(2.0 ** (-5.0 - np.arange(RET_HEADS, dtype=np.float64))))


IN_N_CHUNK = 512
RET_CHUNK = 256


def _head_pair_rms(y, g2):
    rows, width = y.shape
    low = lax.broadcasted_iota(jnp.int32, (rows, LANES), 1) < SWA_HEAD_DIM
    out = []
    for c in range(0, width, LANES):
        yb = y[:, c:c + LANES]
        y2 = yb * yb
        ms_lo = jnp.sum(jnp.where(low, y2, 0.0), axis=-1, keepdims=True) * (1.0 / SWA_HEAD_DIM)
        ms_hi = jnp.sum(jnp.where(low, 0.0, y2), axis=-1, keepdims=True) * (1.0 / SWA_HEAD_DIM)
        scale = jnp.where(low, lax.rsqrt(ms_lo + EPS), lax.rsqrt(ms_hi + EPS))
        out.append(yb * scale * g2)
    return jnp.concatenate(out, axis=1) if len(out) > 1 else out[0]


def _inproj_kernel(x_ref, g_ref, qg_ref, kg_ref, dec_ref, w_ref, mix_ref, kdt_ref, gate_ref):
    x = x_ref[...]
    ms = jnp.mean(x * x, axis=-1, keepdims=True)
    hn = (x * lax.rsqrt(ms + EPS) * g_ref[...]).astype(BF16)

    def proj(lo, width):
        return _dot(hn, w_ref[:, lo:lo + width])

    for c in range(0, SWA_Q_W, IN_N_CHUNK):
        mix_ref[:, MIX_SQ + c:MIX_SQ + c + IN_N_CHUNK] = _head_pair_rms(
            proj(W_SQ + c, IN_N_CHUNK), qg_ref[...]).astype(BF16)
    for c in range(0, RET_V_W, IN_N_CHUNK):
        mix_ref[:, MIX_RV + c:MIX_RV + c + IN_N_CHUNK] = proj(W_RV + c, IN_N_CHUNK).astype(BF16)
    mix_ref[:, MIX_RQ:MIX_RQ + RET_QK_W] = (proj(W_RQ, RET_QK_W) * dec_ref[:, 0:RET_QK_W]).astype(BF16)
    kd = proj(W_RK, RET_QK_W) * dec_ref[:, RET_QK_W:2 * RET_QK_W]
    kdt_ref[...] = kd.T.astype(BF16)
    mix_ref[:, MIX_SK:MIX_SK + SWA_KV_W] = _head_pair_rms(proj(W_SK, SWA_KV_W), kg_ref[...]).astype(BF16)
    mix_ref[:, MIX_SV:MIX_SV + SWA_KV_W] = proj(W_SV, SWA_KV_W).astype(BF16)
    for c in range(0, GATE_W, IN_N_CHUNK):
        gate_ref[:, c:c + IN_N_CHUNK] = proj(W_GATES + c, IN_N_CHUNK).astype(BF16)


def _inproj(x3d, g, qg2, kg2, dec, w_bf16, block_rows):
    b, seq, _ = x3d.shape
    assert seq % block_rows == 0 and dec.shape == (block_rows, 2 * RET_QK_W)
    steps = seq // block_rows
    const = lambda shape: pl.BlockSpec(shape, lambda i: (0, 0))
    return pl.pallas_call(
        _inproj_kernel,
        out_shape=(jax.ShapeDtypeStruct((b, seq, MIX_W), BF16),
                   jax.ShapeDtypeStruct((b, RET_QK_W, seq), BF16),
                   jax.ShapeDtypeStruct((b, seq, GATE_W), BF16)),
        grid=(b * steps,),
        in_specs=[
            pl.BlockSpec((None, block_rows, D_MODEL), lambda i: (i // steps, i % steps, 0)),
            const((1, D_MODEL)),
            const((1, LANES)),
            const((1, LANES)),
            const((block_rows, 2 * RET_QK_W)),
            pl.BlockSpec((D_MODEL, D_IN), lambda i: (0, 0), pipeline_mode=pl.Buffered(1)),
        ],
        out_specs=(pl.BlockSpec((None, block_rows, MIX_W), lambda i: (i // steps, i % steps, 0)),
                   pl.BlockSpec((None, RET_QK_W, block_rows), lambda i: (i // steps, 0, i % steps)),
                   pl.BlockSpec((None, block_rows, GATE_W), lambda i: (i // steps, i % steps, 0))),
        compiler_params=pltpu.CompilerParams(
            dimension_semantics=("arbitrary",),
            vmem_limit_bytes=V7X_VMEM_LIMIT_BYTES,
        ),
        name="inproj",
    )(x3d, g, qg2, kg2, dec, w_bf16)


def _decay_tables(block_rows):
    lg = _retention_log_gamma()
    pos = (np.arange(block_rows) % RET_CHUNK).astype(np.float64) + 1.0
    qd = np.exp(lg[None, :] * pos[:, None])
    kd = np.exp(-lg[None, :] * pos[:, None]) * (RET_QK_DIM ** -0.5)
    rep = lambda t: np.repeat(t, RET_QK_DIM, axis=1)
    return np.concatenate([rep(qd), rep(kd)], axis=1).astype(np.float32)


def _meta_decay_tables():
    lg = _retention_log_gamma()
    pos = np.arange(CHUNK, dtype=np.float64)
    kd = np.exp(lg[None, :] * (CHUNK - 1.0 - pos[:, None])) * (RET_QK_DIM ** -0.5)
    rep = lambda t: np.repeat(t, RET_QK_DIM, axis=1)
    return np.concatenate([np.ones((CHUNK, RET_QK_W)), rep(kd)], axis=1).astype(np.float32)


def _retention_kernel(chunk_decay, qd_ref, kdt_ref, v_ref, mkdt_ref, mv_ref, o_ref, state_ref):
    n = pl.program_id(1)

    @pl.when(n == 0)
    def _():
        for h in range(RET_HEADS):
            state_ref[h] = _dot(mkdt_ref[h * RET_QK_DIM:(h + 1) * RET_QK_DIM, :],
                                mv_ref[:, h * RET_V_DIM:(h + 1) * RET_V_DIM])

    row = lax.broadcasted_iota(jnp.int32, (RET_CHUNK, RET_CHUNK), 0)
    col = lax.broadcasted_iota(jnp.int32, (RET_CHUNK, RET_CHUNK), 1)
    causal = col <= row
    for h in range(RET_HEADS):
        qd = qd_ref[:, h * RET_QK_DIM:(h + 1) * RET_QK_DIM]
        kdt = kdt_ref[h * RET_QK_DIM:(h + 1) * RET_QK_DIM, :]
        v = v_ref[:, h * RET_V_DIM:(h + 1) * RET_V_DIM]
        state = state_ref[h]
        s = jnp.where(causal, _dot(qd, kdt), 0.0).astype(BF16)
        o = _dot(s, v) + _dot(qd, state.astype(BF16))
        state_ref[h] = chunk_decay[h] * (state + _dot(kdt, v))
        o_ref[:, h * RET_V_DIM:(h + 1) * RET_V_DIM] = o.astype(BF16)


def _retention(mix, kdt, meta_mix, meta_kdt):
    b, l, _ = mix.shape
    assert l % RET_CHUNK == 0
    chunk_decay = [float(c) for c in np.exp(_retention_log_gamma() * RET_CHUNK)]
    return pl.pallas_call(
        functools.partial(_retention_kernel, chunk_decay),
        out_shape=jax.ShapeDtypeStruct((b, l, RET_V_W), BF16),
        grid=(b, l // RET_CHUNK),
        in_specs=[
            pl.BlockSpec((None, RET_CHUNK, RET_QK_W), lambda bi, ni: (bi, ni, MIX_RQ // RET_QK_W)),
            pl.BlockSpec((None, RET_QK_W, RET_CHUNK), lambda bi, ni: (bi, 0, ni)),
            pl.BlockSpec((None, RET_CHUNK, RET_V_W), lambda bi, ni: (bi, ni, MIX_RV // RET_V_W)),
            pl.BlockSpec((None, RET_QK_W, CHUNK), lambda bi, ni: (0, 0, 0)),
            pl.BlockSpec((None, CHUNK, RET_V_W), lambda bi, ni: (0, 0, MIX_RV // RET_V_W)),
        ],
        out_specs=pl.BlockSpec((None, RET_CHUNK, RET_V_W), lambda bi, ni: (bi, ni, 0)),
        scratch_shapes=[pltpu.VMEM((RET_HEADS, RET_QK_DIM, RET_V_DIM), F32)],
        compiler_params=pltpu.CompilerParams(
            dimension_semantics=("arbitrary", "arbitrary"),
            vmem_limit_bytes=V7X_VMEM_LIMIT_BYTES,
        ),
        name="retention",
    )(mix, kdt, mix, meta_kdt, meta_mix)


PAIR_ROWS = 2 * CHUNK


def _swa_consts():
    slopes = 2.0 ** (-8.0 * np.arange(1, SWA_Q_HEADS + 1, dtype=np.float64) / SWA_Q_HEADS)
    slopes = slopes.astype(np.float32)
    i = np.arange(CHUNK)[:, None]
    j = np.arange(CHUNK)[None, :]
    band_dist = np.where(j <= i, i - j, i - j + CHUNK).astype(np.float32)
    meta_dist = (i - (j % N_META) + N_META).astype(np.float32)
    band_bias = np.zeros((2, SWA_KV_HEADS, PAIR_ROWS, 2 * CHUNK), np.float32)
    meta_bias = np.full((SWA_KV_HEADS, PAIR_ROWS, CHUNK), np.inf, np.float32)
    meta_slope = np.zeros((SWA_KV_HEADS, PAIR_ROWS, CHUNK), np.float32)
    for h in range(SWA_Q_HEADS):
        g, r = divmod(h, SWA_GROUP)
        p, e = divmod(r, 2)
        rows = slice(p * CHUNK, (p + 1) * CHUNK)
        cols = slice(e * CHUNK, (e + 1) * CHUNK)
        bias = slopes[h] * band_dist
        band_bias[1, g, rows, cols] = bias
        band_bias[0, g, rows, cols] = np.where(j <= i, bias, np.inf)
        mcols = slice(e * N_META, (e + 1) * N_META)
        meta_bias[g, rows, mcols] = (slopes[h] * meta_dist)[:, mcols]
        meta_slope[g, rows, mcols] = slopes[h] * np.float32(CHUNK)
    return band_bias, meta_bias, meta_slope


def _swa_kernel(sink_ref, sq_ref, sk_ref, sv_ref, mk_ref, mv_ref,
                band_bias_ref, meta_bias_ref, meta_slope_ref, o_ref,
                bd_ref, vbuf_ref, bdm_ref, vm_ref):
    n = pl.program_id(1)
    slot = n & 1
    pslot = 1 - slot
    zeros_head = jnp.zeros((CHUNK, SWA_HEAD_DIM), BF16)

    @pl.when(n == 0)
    def _():
        bd_ref[...] = jnp.zeros_like(bd_ref)
        vbuf_ref[...] = jnp.zeros_like(vbuf_ref)
        bdm_ref[...] = jnp.zeros_like(bdm_ref)
        vm_ref[...] = jnp.zeros_like(vm_ref)
        zm = jnp.zeros((N_META, SWA_HEAD_DIM), BF16)
        for g in range(SWA_KV_HEADS):
            cols = slice(g * SWA_HEAD_DIM, (g + 1) * SWA_HEAD_DIM)
            km = mk_ref[:, cols]
            bdm_ref[g, 0:N_META, :] = jnp.concatenate([km, zm], axis=1)
            bdm_ref[g, N_META:2 * N_META, :] = jnp.concatenate([zm, km], axis=1)
            vm_ref[g, 0:N_META, :] = mv_ref[:, cols]
            vm_ref[g, N_META:2 * N_META, :] = mv_ref[:, cols]

    for g in range(SWA_KV_HEADS):
        kg = sk_ref[:, g * SWA_HEAD_DIM:(g + 1) * SWA_HEAD_DIM]
        bd_ref[slot, g, 0:CHUNK, :] = jnp.concatenate([kg, zeros_head], axis=1)
        bd_ref[slot, g, CHUNK:2 * CHUNK, :] = jnp.concatenate([zeros_head, kg], axis=1)
    vbuf_ref[slot] = sv_ref[...]

    row = lax.broadcasted_iota(jnp.int32, (CHUNK, CHUNK), 0)
    col = lax.broadcasted_iota(jnp.int32, (CHUNK, CHUNK), 1)
    own = col <= row
    own2 = jnp.concatenate([own, own], axis=1)
    own4 = jnp.concatenate([own2, own2], axis=0)
    meta_seg = [col < N_META, (col >= N_META) & (col < 2 * N_META)]
    bias_sel = jnp.minimum(n, 1)
    chunk_idx = n.astype(F32)
    neg_inf = jnp.float32(-jnp.inf)

    for g in range(SWA_KV_HEADS):
        q2 = jnp.concatenate([sq_ref[:, (2 * g + p) * LANES:(2 * g + p + 1) * LANES] for p in range(2)],
                             axis=0)
        s_own = _dot_nt(q2, bd_ref[slot, g])
        s_prev = _dot_nt(q2, bd_ref[pslot, g])
        s = jnp.where(own4, s_own, s_prev) - band_bias_ref[bias_sel, g]
        s_meta = _dot_nt(q2, bdm_ref[g]) - (meta_bias_ref[g] + meta_slope_ref[g] * chunk_idx)

        lhs = []
        inv = []
        for p in range(2):
            sm_p = s_meta[p * CHUNK:(p + 1) * CHUNK]
            for e in range(2):
                h = g * SWA_GROUP + 2 * p + e
                sink = sink_ref[h]
                sb = s[p * CHUNK:(p + 1) * CHUNK, e * CHUNK:(e + 1) * CHUNK]
                sm = jnp.where(meta_seg[e], sm_p, neg_inf)
                m = jnp.maximum(jnp.max(jnp.maximum(sb, sm), axis=-1, keepdims=True), sink)
                pb = jnp.exp(sb - m)
                pm = jnp.exp(sm - m)
                denom = jnp.sum(pb + pm, axis=-1, keepdims=True) + jnp.exp(sink - m)
                inv.append(1.0 / denom)
                lhs.append(jnp.concatenate(
                    [jnp.where(own, pb, 0.0).astype(BF16), jnp.where(own, 0.0, pb).astype(BF16),
                     pm.astype(BF16)], axis=1))
        cols = slice(g * SWA_HEAD_DIM, (g + 1) * SWA_HEAD_DIM)
        vv = jnp.concatenate([vbuf_ref[slot, :, cols], vbuf_ref[pslot, :, cols], vm_ref[g]], axis=0)
        acc = _dot(jnp.concatenate(lhs, axis=0), vv)
        for r in range(SWA_GROUP):
            h = g * SWA_GROUP + r
            o_ref[:, h * SWA_HEAD_DIM:(h + 1) * SWA_HEAD_DIM] = (
                acc[r * CHUNK:(r + 1) * CHUNK] * inv[r]).astype(BF16)


def _swa(mix, meta_mix, sinks):
    b, l, _ = mix.shape
    nc = l // CHUNK
    band_bias, meta_bias, meta_slope = _swa_consts()
    const3 = lambda shape: pl.BlockSpec(shape, lambda bi, ni, s: (0, 0, 0))
    meta_rows = PAD // N_META
    grid_spec = pltpu.PrefetchScalarGridSpec(
        num_scalar_prefetch=1,
        grid=(b, nc),
        in_specs=[
            pl.BlockSpec((None, CHUNK, SWA_Q_W), lambda bi, ni, s: (bi, ni, MIX_SQ // SWA_Q_W)),
            pl.BlockSpec((None, CHUNK, SWA_KV_W), lambda bi, ni, s: (bi, ni, MIX_SK // SWA_KV_W)),
            pl.BlockSpec((None, CHUNK, SWA_KV_W), lambda bi, ni, s: (bi, ni, MIX_SV // SWA_KV_W)),
            pl.BlockSpec((None, N_META, SWA_KV_W), lambda bi, ni, s: (0, meta_rows, MIX_SK // SWA_KV_W)),
            pl.BlockSpec((None, N_META, SWA_KV_W), lambda bi, ni, s: (0, meta_rows, MIX_SV // SWA_KV_W)),
            pl.BlockSpec((2, SWA_KV_HEADS, PAIR_ROWS, 2 * CHUNK), lambda bi, ni, s: (0, 0, 0, 0)),
            const3((SWA_KV_HEADS, PAIR_ROWS, CHUNK)),
            const3((SWA_KV_HEADS, PAIR_ROWS, CHUNK)),
        ],
        out_specs=pl.BlockSpec((None, CHUNK, SWA_Q_W), lambda bi, ni, s: (bi, ni, 0)),
        scratch_shapes=[
            pltpu.VMEM((2, SWA_KV_HEADS, PAIR_ROWS, LANES), BF16),
            pltpu.VMEM((2, CHUNK, SWA_KV_W), BF16),
            pltpu.VMEM((SWA_KV_HEADS, CHUNK, LANES), BF16),
            pltpu.VMEM((SWA_KV_HEADS, CHUNK, SWA_HEAD_DIM), BF16),
        ],
    )
    return pl.pallas_call(
        _swa_kernel,
        out_shape=jax.ShapeDtypeStruct((b, l, SWA_Q_W), BF16),
        grid_spec=grid_spec,
        compiler_params=pltpu.CompilerParams(
            dimension_semantics=("arbitrary", "arbitrary"),
            vmem_limit_bytes=V7X_VMEM_LIMIT_BYTES,
        ),
        name="swa",
    )(sinks.astype(F32), mix, mix, mix, meta_mix, meta_mix,
      jnp.asarray(band_bias), jnp.asarray(meta_bias), jnp.asarray(meta_slope))


FF_CHUNK = 1024
MERGE_ROWS = 512


def _merge_mlp_kernel(x_ref, or_ref, ys_ref, rg_ref, gr_ref, gs_ref, rn_ref, wr_ref, ws_ref, wo_ref,
                      g_ref, wu_ref, wd_ref, o_ref):
    y_ret = []
    for h in range(RET_HEADS):
        cols = slice(h * RET_V_DIM, (h + 1) * RET_V_DIM)
        o = or_ref[:, cols].astype(F32)
        ms = jnp.mean(o * o, axis=-1, keepdims=True)
        gate = rg_ref[:, cols].astype(F32)
        y_ret.append((o * lax.rsqrt(ms + EPS) * rn_ref[:, cols] * (gate * _sigmoid(gate))).astype(BF16))
    a = _dot(jnp.concatenate(y_ret, axis=1), wr_ref[...])
    b = _dot(ys_ref[...], ws_ref[...])
    merged = _sigmoid(gr_ref[...].astype(F32)) * a + _sigmoid(gs_ref[...].astype(F32)) * b
    h1 = x_ref[...] + _dot(merged.astype(BF16), wo_ref[...])
    ms = jnp.mean(h1 * h1, axis=-1, keepdims=True)
    hn = (h1 * lax.rsqrt(ms + EPS) * g_ref[...]).astype(BF16)
    acc = h1
    for c in range(0, D_FF, FF_CHUNK):
        u = jnp.maximum(_dot(hn, wu_ref[:, c:c + FF_CHUNK]), 0.0)
        acc = acc + _dot((u * u).astype(BF16), wd_ref[c:c + FF_CHUNK, :])
    o_ref[...] = acc


def _merge_mlp(x2d, o_ret, y_swa, gates2d, ret_norm_g, w_ret, w_swa, w_out, mlp_g, w_up, w_down):
    rows = x2d.shape[0]
    assert rows % MERGE_ROWS == 0
    row_spec = lambda w, j=0: pl.BlockSpec((MERGE_ROWS, w), lambda i: (i, j))
    resident = lambda shape: pl.BlockSpec(shape, lambda i: (0, 0), pipeline_mode=pl.Buffered(1))
    return pl.pallas_call(
        _merge_mlp_kernel,
        out_shape=jax.ShapeDtypeStruct((rows, D_MODEL), F32),
        grid=(rows // MERGE_ROWS,),
        in_specs=[
            row_spec(D_MODEL),
            row_spec(RET_V_W),
            row_spec(SWA_Q_W),
            row_spec(D_MODEL, 0),
            row_spec(D_MODEL, 1),
            row_spec(D_MODEL, 2),
            pl.BlockSpec((1, RET_V_W), lambda i: (0, 0)),
            resident((RET_V_W, D_MODEL)),
            resident((SWA_Q_W, D_MODEL)),
            resident((D_MODEL, D_MODEL)),
            pl.BlockSpec((1, D_MODEL), lambda i: (0, 0)),
            resident((D_MODEL, D_FF)),
            resident((D_FF, D_MODEL)),
        ],
        out_specs=row_spec(D_MODEL),
        compiler_params=pltpu.CompilerParams(
            dimension_semantics=("arbitrary",),
            vmem_limit_bytes=V7X_VMEM_LIMIT_BYTES,
        ),
        name="merge_mlp",
    )(x2d, o_ret, y_swa, gates2d, gates2d, gates2d, ret_norm_g, w_ret, w_swa, w_out, mlp_g, w_up, w_down)


INPROJ_ROWS = 512
assert RET_V_W == D_MODEL and INPROJ_ROWS % RET_CHUNK == 0


def _permute_w_in(w):
    split = lambda off, width: w[:, off:off + width]
    return jnp.concatenate([
        split(OFF_SQ, SWA_Q_W), split(OFF_RV, RET_V_W), split(OFF_RQ, RET_QK_W), split(OFF_RK, RET_QK_W),
        split(OFF_SK, SWA_KV_W), split(OFF_SV, SWA_KV_W), split(OFF_RG, RET_V_W),
        split(OFF_GR, D_MODEL), split(OFF_GS, D_MODEL)], axis=1).astype(BF16)


def kernel(x, meta_tokens, mix_norm_g, w_in, ret_norm_g, q_norm_g, k_norm_g, sinks,
           w_ret_branch, w_swa_branch, w_out, mlp_norm_g, w_up, w_down):
    b, l, d = x.shape
    assert d == D_MODEL and l % INPROJ_ROWS == 0 and mix_norm_g.shape[0] == 1
    mix_g = mix_norm_g[0].reshape(1, d).astype(F32)
    w_in_b = _permute_w_in(w_in[0])
    qg2 = jnp.tile(q_norm_g[0].astype(F32) * (SWA_HEAD_DIM ** -0.5), 2).reshape(1, LANES)
    kg2 = jnp.tile(k_norm_g[0].astype(F32), 2).reshape(1, LANES)

    mix, kdt, gates = _inproj(x, mix_g, qg2, kg2, jnp.asarray(_decay_tables(INPROJ_ROWS)), w_in_b,
                              INPROJ_ROWS)
    meta_chunk = jnp.pad(meta_tokens.astype(F32), ((PAD, 0), (0, 0)))[None]
    meta_mix, meta_kdt, _ = _inproj(meta_chunk, mix_g, qg2, kg2, jnp.asarray(_meta_decay_tables()),
                                    w_in_b, CHUNK)

    o_ret = _retention(mix, kdt, meta_mix, meta_kdt)
    y_swa = _swa(mix, meta_mix, sinks[0])

    out = _merge_mlp(
        x.reshape(b * l, d), o_ret.reshape(b * l, RET_V_W), y_swa.reshape(b * l, SWA_Q_W),
        gates.reshape(b * l, GATE_W), ret_norm_g[0].reshape(1, RET_V_W).astype(F32),
        w_ret_branch[0].astype(BF16), w_swa_branch[0].astype(BF16), w_out[0].astype(BF16),
        mlp_norm_g[0].reshape(1, d).astype(F32), w_up[0].astype(BF16), w_down[0].astype(BF16))
    return out.reshape(b, l, d)
```

```python
import functools

import numpy as np
import jax
import jax.numpy as jnp
from jax import lax
from jax.experimental import pallas as pl
from jax.experimental.pallas import tpu as pltpu

D_MODEL = 1024
N_META = 16
CHUNK = 128
PAD = CHUNK - N_META
RET_HEADS = 4
RET_QK_DIM = 128
RET_V_DIM = 256
SWA_Q_HEADS = 16
SWA_KV_HEADS = 4
SWA_GROUP = SWA_Q_HEADS // SWA_KV_HEADS
SWA_HEAD_DIM = 64
D_FF = 4 * D_MODEL
EPS = 1e-6
RET_QK_W = RET_HEADS * RET_QK_DIM
RET_V_W = RET_HEADS * RET_V_DIM
SWA_Q_W = SWA_Q_HEADS * SWA_HEAD_DIM
SWA_KV_W = SWA_KV_HEADS * SWA_HEAD_DIM
D_IN = 2 * RET_QK_W + 2 * RET_V_W + SWA_Q_W + 2 * SWA_KV_W + 2 * D_MODEL

OFF_RQ = 0
OFF_RK = OFF_RQ + RET_QK_W
OFF_RV = OFF_RK + RET_QK_W
OFF_RG = OFF_RV + RET_V_W
OFF_SQ = OFF_RG + RET_V_W
OFF_SK = OFF_SQ + SWA_Q_W
OFF_SV = OFF_SK + SWA_KV_W
OFF_GR = OFF_SV + SWA_KV_W
OFF_GS = OFF_GR + D_MODEL

W_SQ = 0
W_RV = W_SQ + SWA_Q_W
W_RQ = W_RV + RET_V_W
W_RK = W_RQ + RET_QK_W
W_SK = W_RK + RET_QK_W
W_SV = W_SK + SWA_KV_W
W_GATES = W_SV + SWA_KV_W
MIX_SQ = 0
MIX_RV = MIX_SQ + SWA_Q_W
MIX_RQ = MIX_RV + RET_V_W
MIX_SK = MIX_RQ + RET_QK_W
MIX_SV = MIX_SK + SWA_KV_W
MIX_W = MIX_SV + SWA_KV_W
GATE_W = RET_V_W + 2 * D_MODEL

V7X_VMEM_LIMIT_BYTES = 56 * 1024 * 1024
LANES = 128
assert LANES == 2 * SWA_HEAD_DIM and CHUNK == LANES

F32 = jnp.float32
BF16 = jnp.bfloat16


def _sigmoid(x):
    return 1.0 / (1.0 + jnp.exp(-x))


def _dot(a, b):
    return jnp.dot(a, b, preferred_element_type=F32)


def _dot_nt(a, b):
    return lax.dot_general(a, b, (((1,), (1,)), ((), ())), preferred_element_type=F32)


def _retention_log_gamma():
    return np.log1p(-(2.0 ** (-5.0 - np.arange(RET_HEADS, dtype=np.float64))))


IN_N_CHUNK = 512
RET_CHUNK = 256


def _head_pair_rms(y, g2):
    rows, width = y.shape
    low = lax.broadcasted_iota(jnp.int32, (rows, LANES), 1) < SWA_HEAD_DIM
    out = []
    for c in range(0, width, LANES):
        yb = y[:, c:c + LANES]
        y2 = yb * yb
        ms_lo = jnp.sum(jnp.where(low, y2, 0.0), axis=-1, keepdims=True) * (1.0 / SWA_HEAD_DIM)
        ms_hi = jnp.sum(jnp.where(low, 0.0, y2), axis=-1, keepdims=True) * (1.0 / SWA_HEAD_DIM)
        scale = jnp.where(low, lax.rsqrt(ms_lo + EPS), lax.rsqrt(ms_hi + EPS))
        out.append(yb * scale * g2)
    return jnp.concatenate(out, axis=1) if len(out) > 1 else out[0]


def _inproj_kernel(x_ref, g_ref, qg_ref, kg_ref, dec_ref, w_ref, mix_ref, kdt_ref, gate_ref):
    x = x_ref[...]
    ms = jnp.mean(x * x, axis=-1, keepdims=True)
    hn = (x * lax.rsqrt(ms + EPS) * g_ref[...]).astype(BF16)

    def proj(lo, width):
        return _dot(hn, w_ref[:, lo:lo + width])

    for c in range(0, SWA_Q_W, IN_N_CHUNK):
        mix_ref[:, MIX_SQ + c:MIX_SQ + c + IN_N_CHUNK] = _head_pair_rms(
            proj(W_SQ + c, IN_N_CHUNK), qg_ref[...]).astype(BF16)
    for c in range(0, RET_V_W, IN_N_CHUNK):
        mix_ref[:, MIX_RV + c:MIX_RV + c + IN_N_CHUNK] = proj(W_RV + c, IN_N_CHUNK).astype(BF16)
    mix_ref[:, MIX_RQ:MIX_RQ + RET_QK_W] = (proj(W_RQ, RET_QK_W) * dec_ref[:, 0:RET_QK_W]).astype(BF16)
    kd = proj(W_RK, RET_QK_W) * dec_ref[:, RET_QK_W:2 * RET_QK_W]
    kdt_ref[...] = kd.T.astype(BF16)
    mix_ref[:, MIX_SK:MIX_SK + SWA_KV_W] = _head_pair_rms(proj(W_SK, SWA_KV_W), kg_ref[...]).astype(BF16)
    mix_ref[:, MIX_SV:MIX_SV + SWA_KV_W] = proj(W_SV, SWA_KV_W).astype(BF16)
    for c in range(0, GATE_W, IN_N_CHUNK):
        gate_ref[:, c:c + IN_N_CHUNK] = proj(W_GATES + c, IN_N_CHUNK).astype(BF16)


def _inproj(x3d, g, qg2, kg2, dec, w_bf16, block_rows):
    b, seq, _ = x3d.shape
    assert seq % block_rows == 0 and dec.shape == (block_rows, 2 * RET_QK_W)
    steps = seq // block_rows
    const = lambda shape: pl.BlockSpec(shape, lambda i: (0, 0))
    return pl.pallas_call(
        _inproj_kernel,
        out_shape=(jax.ShapeDtypeStruct((b, seq, MIX_W), BF16),
                   jax.ShapeDtypeStruct((b, RET_QK_W, seq), BF16),
                   jax.ShapeDtypeStruct((b, seq, GATE_W), BF16)),
        grid=(b * steps,),
        in_specs=[
            pl.BlockSpec((None, block_rows, D_MODEL), lambda i: (i // steps, i % steps, 0)),
            const((1, D_MODEL)),
            const((1, LANES)),
            const((1, LANES)),
            const((block_rows, 2 * RET_QK_W)),
            pl.BlockSpec((D_MODEL, D_IN), lambda i: (0, 0), pipeline_mode=pl.Buffered(1)),
        ],
        out_specs=(pl.BlockSpec((None, block_rows, MIX_W), lambda i: (i // steps, i % steps, 0)),
                   pl.BlockSpec((None, RET_QK_W, block_rows), lambda i: (i // steps, 0, i % steps)),
                   pl.BlockSpec((None, block_rows, GATE_W), lambda i: (i // steps, i % steps, 0))),
        compiler_params=pltpu.CompilerParams(
            dimension_semantics=("arbitrary",),
            vmem_limit_bytes=V7X_VMEM_LIMIT_BYTES,
        ),
        name="inproj",
    )(x3d, g, qg2, kg2, dec, w_bf16)


def _decay_tables(block_rows):
    lg = _retention_log_gamma()
    pos = (np.arange(block_rows) % RET_CHUNK).astype(np.float64) + 1.0
    qd = np.exp(lg[None, :] * pos[:, None])
    kd = np.exp(-lg[None, :] * pos[:, None]) * (RET_QK_DIM ** -0.5)
    rep = lambda t: np.repeat(t, RET_QK_DIM, axis=1)
    return np.concatenate([rep(qd), rep(kd)], axis=1).astype(np.float32)


def _meta_decay_tables():
    lg = _retention_log_gamma()
    pos = np.arange(CHUNK, dtype=np.float64)
    kd = np.exp(lg[None, :] * (CHUNK - 1.0 - pos[:, None])) * (RET_QK_DIM ** -0.5)
    rep = lambda t: np.repeat(t, RET_QK_DIM, axis=1)
    return np.concatenate([np.ones((CHUNK, RET_QK_W)), rep(kd)], axis=1).astype(np.float32)


RET_CHUNK_DECAY = [float(c) for c in np.exp(_retention_log_gamma() * RET_CHUNK)]


def _retention_init(mkdt_ref, mv_ref, state_ref):
    for h in range(RET_HEADS):
        state_ref[h] = _dot(mkdt_ref[h * RET_QK_DIM:(h + 1) * RET_QK_DIM, :],
                            mv_ref[:, h * RET_V_DIM:(h + 1) * RET_V_DIM])


def _retention_head(h, causal, qd_ref, kdt_ref, v_ref, o_ref, state_ref):
    qd = qd_ref[:, h * RET_QK_DIM:(h + 1) * RET_QK_DIM]
    kdt = kdt_ref[h * RET_QK_DIM:(h + 1) * RET_QK_DIM, :]
    v = v_ref[:, h * RET_V_DIM:(h + 1) * RET_V_DIM]
    state = state_ref[h]
    s = jnp.where(causal, _dot(qd, kdt), 0.0).astype(BF16)
    o = _dot(s, v) + _dot(qd, state.astype(BF16))
    state_ref[h] = RET_CHUNK_DECAY[h] * (state + _dot(kdt, v))
    o_ref[:, h * RET_V_DIM:(h + 1) * RET_V_DIM] = o.astype(BF16)


PAIR_ROWS = 2 * CHUNK
SWA_STEP_ROWS = 2 * CHUNK


def _swa_consts():
    slopes = 2.0 ** (-8.0 * np.arange(1, SWA_Q_HEADS + 1, dtype=np.float64) / SWA_Q_HEADS)
    slopes = slopes.astype(np.float32)
    i = np.arange(CHUNK)[:, None]
    j = np.arange(CHUNK)[None, :]
    band_dist = np.where(j <= i, i - j, i - j + CHUNK).astype(np.float32)
    meta_dist = (i - (j % N_META) + N_META).astype(np.float32)
    band_bias = np.zeros((2, SWA_KV_HEADS, PAIR_ROWS, 2 * CHUNK), np.float32)
    meta_bias = np.full((SWA_KV_HEADS, PAIR_ROWS, CHUNK), np.inf, np.float32)
    meta_slope = np.zeros((SWA_KV_HEADS, PAIR_ROWS, CHUNK), np.float32)
    for h in range(SWA_Q_HEADS):
        g, r = divmod(h, SWA_GROUP)
        p, e = divmod(r, 2)
        rows = slice(p * CHUNK, (p + 1) * CHUNK)
        cols = slice(e * CHUNK, (e + 1) * CHUNK)
        bias = slopes[h] * band_dist
        band_bias[1, g, rows, cols] = bias
        band_bias[0, g, rows, cols] = np.where(j <= i, bias, np.inf)
        mcols = slice(e * N_META, (e + 1) * N_META)
        meta_bias[g, rows, mcols] = (slopes[h] * meta_dist)[:, mcols]
        meta_slope[g, rows, mcols] = slopes[h] * np.float32(CHUNK)
    return band_bias, meta_bias, meta_slope


def _mixers_kernel(sink_ref, sq_ref, sk_ref, sv_ref, mk_ref, mv_ref,
                   band_bias_ref, meta_bias_ref, meta_slope_ref,
                   qd_ref, kdt_ref, rv_ref, mkdt_ref, mrv_ref,
                   o_ref, oret_ref,
                   bd_ref, vbuf_ref, bdm_ref, vm_ref, state_ref):
    n = pl.program_id(1)
    parity = n & 1
    odd_slot = 1 + parity
    last_odd_slot = 2 - parity
    zeros_head = jnp.zeros((CHUNK, SWA_HEAD_DIM), BF16)

    @pl.when(n == 0)
    def _():
        bd_ref[...] = jnp.zeros_like(bd_ref)
        vbuf_ref[...] = jnp.zeros_like(vbuf_ref)
        bdm_ref[...] = jnp.zeros_like(bdm_ref)
        vm_ref[...] = jnp.zeros_like(vm_ref)
        zm = jnp.zeros((N_META, SWA_HEAD_DIM), BF16)
        for g in range(SWA_KV_HEADS):
            cols = slice(g * SWA_HEAD_DIM, (g + 1) * SWA_HEAD_DIM)
            km = mk_ref[:, cols]
            bdm_ref[g, 0:N_META, :] = jnp.concatenate([km, zm], axis=1)
            bdm_ref[g, N_META:2 * N_META, :] = jnp.concatenate([zm, km], axis=1)
            vm_ref[g, 0:N_META, :] = mv_ref[:, cols]
            vm_ref[g, N_META:2 * N_META, :] = mv_ref[:, cols]
        _retention_init(mkdt_ref, mrv_ref, state_ref)

    ret_row = lax.broadcasted_iota(jnp.int32, (RET_CHUNK, RET_CHUNK), 0)
    ret_col = lax.broadcasted_iota(jnp.int32, (RET_CHUNK, RET_CHUNK), 1)
    ret_causal = ret_col <= ret_row

    row = lax.broadcasted_iota(jnp.int32, (CHUNK, CHUNK), 0)
    col = lax.broadcasted_iota(jnp.int32, (CHUNK, CHUNK), 1)
    own = col <= row
    own2 = jnp.concatenate([own, own], axis=1)
    own4 = jnp.concatenate([own2, own2], axis=0)
    meta_seg = [col < N_META, (col >= N_META) & (col < 2 * N_META)]
    neg_inf = jnp.float32(-jnp.inf)

    first_chunk = (2 * n).astype(F32)
    chunks = [(slice(0, CHUNK), 0, last_odd_slot, jnp.minimum(n, 1), first_chunk),
              (slice(CHUNK, 2 * CHUNK), odd_slot, 0, 1, first_chunk + 1.0)]

    for rows, slot, _, _, _ in chunks:
        for g in range(SWA_KV_HEADS):
            kg = sk_ref[rows, g * SWA_HEAD_DIM:(g + 1) * SWA_HEAD_DIM]
            bd_ref[slot, g, 0:CHUNK, :] = jnp.concatenate([kg, zeros_head], axis=1)
            bd_ref[slot, g, CHUNK:2 * CHUNK, :] = jnp.concatenate([zeros_head, kg], axis=1)
        vbuf_ref[slot] = sv_ref[rows, :]

    def scores(chunk, g):
        rows, slot, prev_slot, bias_sel, chunk_idx = chunk
        q2 = jnp.concatenate(
            [sq_ref[rows, (2 * g + p) * LANES:(2 * g + p + 1) * LANES] for p in range(2)],
            axis=0)
        s_own = _dot_nt(q2, bd_ref[slot, g])
        s_prev = _dot_nt(q2, bd_ref[prev_slot, g])
        s = jnp.where(own4, s_own, s_prev) - band_bias_ref[bias_sel, g]
        s_meta = _dot_nt(q2, bdm_ref[g]) - (meta_bias_ref[g] + meta_slope_ref[g] * chunk_idx)
        return s, s_meta

    def softmax(g, s, s_meta):
        lhs = []
        inv = []
        for p in range(2):
            sm_p = s_meta[p * CHUNK:(p + 1) * CHUNK]
            for e in range(2):
                h = g * SWA_GROUP + 2 * p + e
                sink = sink_ref[h]
                sb = s[p * CHUNK:(p + 1) * CHUNK, e * CHUNK:(e + 1) * CHUNK]
                sm = jnp.where(meta_seg[e], sm_p, neg_inf)
                m = jnp.maximum(jnp.max(jnp.maximum(sb, sm), axis=-1, keepdims=True), sink)
                pb = jnp.exp(sb - m)
                pm = jnp.exp(sm - m)
                denom = jnp.sum(pb + pm, axis=-1, keepdims=True) + jnp.exp(sink - m)
                inv.append(1.0 / denom)
                lhs.append(jnp.concatenate(
                    [jnp.where(own, pb, 0.0).astype(BF16), jnp.where(own, 0.0, pb).astype(BF16),
                     pm.astype(BF16)], axis=1))
        return jnp.concatenate(lhs, axis=0), inv

    def weighted_values(chunk, g, lhs, inv):
        rows, slot, prev_slot, _, _ = chunk
        cols = slice(g * SWA_HEAD_DIM, (g + 1) * SWA_HEAD_DIM)
        vv = jnp.concatenate([vbuf_ref[slot, :, cols], vbuf_ref[prev_slot, :, cols], vm_ref[g]], axis=0)
        acc = _dot(lhs, vv)
        for r in range(SWA_GROUP):
            h = g * SWA_GROUP + r
            o_ref[rows, h * SWA_HEAD_DIM:(h + 1) * SWA_HEAD_DIM] = (
                acc[r * CHUNK:(r + 1) * CHUNK] * inv[r]).astype(BF16)

    tasks = [(chunk, g) for chunk in chunks for g in range(SWA_KV_HEADS)]
    assert len(tasks) == 2 * RET_HEADS
    pending_scores = scores(*tasks[0])
    pending_probs = None
    for t, (chunk, g) in enumerate(tasks):
        current_scores = pending_scores
        if t + 1 < len(tasks):
            pending_scores = scores(*tasks[t + 1])
        probs = softmax(g, *current_scores)
        if t % 2 == 1:
            _retention_head(t // 2, ret_causal, qd_ref, kdt_ref, rv_ref, oret_ref, state_ref)
        if pending_probs is not None:
            weighted_values(*tasks[t - 1], *pending_probs)
        pending_probs = probs
    weighted_values(*tasks[-1], *pending_probs)


def _mixers(mix, kdt, meta_mix, meta_kdt, sinks):
    b, l, _ = mix.shape
    assert l % SWA_STEP_ROWS == 0 and SWA_STEP_ROWS == RET_CHUNK
    band_bias, meta_bias, meta_slope = _swa_consts()
    const3 = lambda shape: pl.BlockSpec(shape, lambda bi, ni, s: (0, 0, 0))
    meta_rows = PAD // N_META
    grid_spec = pltpu.PrefetchScalarGridSpec(
        num_scalar_prefetch=1,
        grid=(b, l // SWA_STEP_ROWS),
        in_specs=[
            pl.BlockSpec((None, SWA_STEP_ROWS, SWA_Q_W), lambda bi, ni, s: (bi, ni, MIX_SQ // SWA_Q_W)),
            pl.BlockSpec((None, SWA_STEP_ROWS, SWA_KV_W), lambda bi, ni, s: (bi, ni, MIX_SK // SWA_KV_W)),
            pl.BlockSpec((None, SWA_STEP_ROWS, SWA_KV_W), lambda bi, ni, s: (bi, ni, MIX_SV // SWA_KV_W)),
            pl.BlockSpec((None, N_META, SWA_KV_W), lambda bi, ni, s: (0, meta_rows, MIX_SK // SWA_KV_W)),
            pl.BlockSpec((None, N_META, SWA_KV_W), lambda bi, ni, s: (0, meta_rows, MIX_SV // SWA_KV_W)),
            pl.BlockSpec((2, SWA_KV_HEADS, PAIR_ROWS, 2 * CHUNK), lambda bi, ni, s: (0, 0, 0, 0)),
            const3((SWA_KV_HEADS, PAIR_ROWS, CHUNK)),
            const3((SWA_KV_HEADS, PAIR_ROWS, CHUNK)),
            pl.BlockSpec((None, RET_CHUNK, RET_QK_W), lambda bi, ni, s: (bi, ni, MIX_RQ // RET_QK_W)),
            pl.BlockSpec((None, RET_QK_W, RET_CHUNK), lambda bi, ni, s: (bi, 0, ni)),
            pl.BlockSpec((None, RET_CHUNK, RET_V_W), lambda bi, ni, s: (bi, ni, MIX_RV // RET_V_W)),
            pl.BlockSpec((None, RET_QK_W, CHUNK), lambda bi, ni, s: (0, 0, 0)),
            pl.BlockSpec((None, CHUNK, RET_V_W), lambda bi, ni, s: (0, 0, MIX_RV // RET_V_W)),
        ],
        out_specs=(pl.BlockSpec((None, SWA_STEP_ROWS, SWA_Q_W), lambda bi, ni, s: (bi, ni, 0)),
                   pl.BlockSpec((None, RET_CHUNK, RET_V_W), lambda bi, ni, s: (bi, ni, 0))),
        scratch_shapes=[
            pltpu.VMEM((3, SWA_KV_HEADS, PAIR_ROWS, LANES), BF16),
            pltpu.VMEM((3, CHUNK, SWA_KV_W), BF16),
            pltpu.VMEM((SWA_KV_HEADS, CHUNK, LANES), BF16),
            pltpu.VMEM((SWA_KV_HEADS, CHUNK, SWA_HEAD_DIM), BF16),
            pltpu.VMEM((RET_HEADS, RET_QK_DIM, RET_V_DIM), F32),
        ],
    )
    return pl.pallas_call(
        _mixers_kernel,
        out_shape=(jax.ShapeDtypeStruct((b, l, SWA_Q_W), BF16),
                   jax.ShapeDtypeStruct((b, l, RET_V_W), BF16)),
        grid_spec=grid_spec,
        compiler_params=pltpu.CompilerParams(
            dimension_semantics=("arbitrary", "arbitrary"),
            vmem_limit_bytes=V7X_VMEM_LIMIT_BYTES,
        ),
        name="mixers",
    )(sinks.astype(F32), mix, mix, mix, meta_mix, meta_mix,
      jnp.asarray(band_bias), jnp.asarray(meta_bias), jnp.asarray(meta_slope),
      mix, kdt, mix, meta_kdt, meta_mix)


FF_CHUNK = 1024
MERGE_ROWS = 512


def _merge_mlp_kernel(x_ref, or_ref, ys_ref, rg_ref, gr_ref, gs_ref, rn_ref, wr_ref, ws_ref, wo_ref,
                      g_ref, wu_ref, wd_ref, o_ref):
    b = _dot(ys_ref[...], ws_ref[...])
    a = None
    for h in range(RET_HEADS):
        cols = slice(h * RET_V_DIM, (h + 1) * RET_V_DIM)
        o = or_ref[:, cols].astype(F32)
        ms = jnp.mean(o * o, axis=-1, keepdims=True)
        gate = rg_ref[:, cols].astype(F32)
        y = (o * lax.rsqrt(ms + EPS) * rn_ref[:, cols] * (gate * _sigmoid(gate))).astype(BF16)
        part = _dot(y, wr_ref[cols, :])
        a = part if a is None else a + part
    merged = _sigmoid(gr_ref[...].astype(F32)) * a + _sigmoid(gs_ref[...].astype(F32)) * b
    h1 = x_ref[...] + _dot(merged.astype(BF16), wo_ref[...])
    ms = jnp.mean(h1 * h1, axis=-1, keepdims=True)
    hn = (h1 * lax.rsqrt(ms + EPS) * g_ref[...]).astype(BF16)
    acc = h1
    for c in range(0, D_FF, FF_CHUNK):
        u = jnp.maximum(_dot(hn, wu_ref[:, c:c + FF_CHUNK]), 0.0)
        acc = acc + _dot((u * u).astype(BF16), wd_ref[c:c + FF_CHUNK, :])
    o_ref[...] = acc


def _merge_mlp(x2d, o_ret, y_swa, gates2d, ret_norm_g, w_ret, w_swa, w_out, mlp_g, w_up, w_down):
    rows = x2d.shape[0]
    assert rows % MERGE_ROWS == 0
    row_spec = lambda w, j=0: pl.BlockSpec((MERGE_ROWS, w), lambda i: (i, j))
    resident = lambda shape: pl.BlockSpec(shape, lambda i: (0, 0), pipeline_mode=pl.Buffered(1))
    return pl.pallas_call(
        _merge_mlp_kernel,
        out_shape=jax.ShapeDtypeStruct((rows, D_MODEL), F32),
        grid=(rows // MERGE_ROWS,),
        in_specs=[
            row_spec(D_MODEL),
            row_spec(RET_V_W),
            row_spec(SWA_Q_W),
            row_spec(D_MODEL, 0),
            row_spec(D_MODEL, 1),
            row_spec(D_MODEL, 2),
            pl.BlockSpec((1, RET_V_W), lambda i: (0, 0)),
            resident((RET_V_W, D_MODEL)),
            resident((SWA_Q_W, D_MODEL)),
            resident((D_MODEL, D_MODEL)),
            pl.BlockSpec((1, D_MODEL), lambda i: (0, 0)),
            resident((D_MODEL, D_FF)),
            resident((D_FF, D_MODEL)),
        ],
        out_specs=row_spec(D_MODEL),
        compiler_params=pltpu.CompilerParams(
            dimension_semantics=("arbitrary",),
            vmem_limit_bytes=V7X_VMEM_LIMIT_BYTES,
        ),
        name="merge_mlp",
    )(x2d, o_ret, y_swa, gates2d, gates2d, gates2d, ret_norm_g, w_ret, w_swa, w_out, mlp_g, w_up, w_down)


INPROJ_ROWS = 512
assert RET_V_W == D_MODEL and INPROJ_ROWS % RET_CHUNK == 0


def _permute_w_in(w):
    split = lambda off, width: w[:, off:off + width]
    return jnp.concatenate([
        split(OFF_SQ, SWA_Q_W), split(OFF_RV, RET_V_W), split(OFF_RQ, RET_QK_W), split(OFF_RK, RET_QK_W),
        split(OFF_SK, SWA_KV_W), split(OFF_SV, SWA_KV_W), split(OFF_RG, RET_V_W),
        split(OFF_GR, D_MODEL), split(OFF_GS, D_MODEL)], axis=1).astype(BF16)


def kernel(x, meta_tokens, mix_norm_g, w_in, ret_norm_g, q_norm_g, k_norm_g, sinks,
           w_ret_branch, w_swa_branch, w_out, mlp_norm_g, w_up, w_down):
    b, l, d = x.shape
    assert d == D_MODEL and l % INPROJ_ROWS == 0 and mix_norm_g.shape[0] == 1
    mix_g = mix_norm_g[0].reshape(1, d).astype(F32)
    w_in_b = _permute_w_in(w_in[0])
    qg2 = jnp.tile(q_norm_g[0].astype(F32) * (SWA_HEAD_DIM ** -0.5), 2).reshape(1, LANES)
    kg2 = jnp.tile(k_norm_g[0].astype(F32), 2).reshape(1, LANES)

    mix, kdt, gates = _inproj(x, mix_g, qg2, kg2, jnp.asarray(_decay_tables(INPROJ_ROWS)), w_in_b,
                              INPROJ_ROWS)
    meta_chunk = jnp.pad(meta_tokens.astype(F32), ((PAD, 0), (0, 0)))[None]
    meta_mix, meta_kdt, _ = _inproj(meta_chunk, mix_g, qg2, kg2, jnp.asarray(_meta_decay_tables()),
                                    w_in_b, CHUNK)

    y_swa, o_ret = _mixers(mix, kdt, meta_mix, meta_kdt, sinks[0])

    out = _merge_mlp(
        x.reshape(b * l, d), o_ret.reshape(b * l, RET_V_W), y_swa.reshape(b * l, SWA_Q_W),
        gates.reshape(b * l, GATE_W), ret_norm_g[0].reshape(1, RET_V_W).astype(F32),
        w_ret_branch[0].astype(BF16), w_swa_branch[0].astype(BF16), w_out[0].astype(BF16),
        mlp_norm_g[0].reshape(1, d).astype(F32), w_up[0].astype(BF16), w_down[0].astype(BF16))
    return out.reshape(b, l, d)
```

```python
import functools

import numpy as np
import jax
import jax.numpy as jnp
from jax import lax
from jax.experimental import pallas as pl
from jax.experimental.pallas import tpu as pltpu

D_MODEL = 1024
N_META = 16
CHUNK = 128
PAD = CHUNK - N_META
RET_HEADS = 4
RET_QK_DIM = 128
RET_V_DIM = 256
SWA_Q_HEADS = 16
SWA_KV_HEADS = 4
SWA_GROUP = SWA_Q_HEADS // SWA_KV_HEADS
SWA_HEAD_DIM = 64
D_FF = 4 * D_MODEL
EPS = 1e-6
RET_QK_W = RET_HEADS * RET_QK_DIM
RET_V_W = RET_HEADS * RET_V_DIM
SWA_Q_W = SWA_Q_HEADS * SWA_HEAD_DIM
SWA_KV_W = SWA_KV_HEADS * SWA_HEAD_DIM
D_IN = 2 * RET_QK_W + 2 * RET_V_W + SWA_Q_W + 2 * SWA_KV_W + 2 * D_MODEL

OFF_RQ = 0
OFF_RK = OFF_RQ + RET_QK_W
OFF_RV = OFF_RK + RET_QK_W
OFF_RG = OFF_RV + RET_V_W
OFF_SQ = OFF_RG + RET_V_W
OFF_SK = OFF_SQ + SWA_Q_W
OFF_SV = OFF_SK + SWA_KV_W
OFF_GR = OFF_SV + SWA_KV_W
OFF_GS = OFF_GR + D_MODEL

W_SQ = 0
W_RV = W_SQ + SWA_Q_W
W_RQ = W_RV + RET_V_W
W_RK = W_RQ + RET_QK_W
W_SK = W_RK + RET_QK_W
W_SV = W_SK + SWA_KV_W
W_GATES = W_SV + SWA_KV_W
MIX_SQ = 0
MIX_RV = MIX_SQ + SWA_Q_W
MIX_RQ = MIX_RV + RET_V_W
MIX_SK = MIX_RQ + RET_QK_W
MIX_SV = MIX_SK + SWA_KV_W
MIX_W = MIX_SV + SWA_KV_W
GATE_W = RET_V_W + 2 * D_MODEL

V7X_VMEM_LIMIT_BYTES = 60 * 1024 * 1024
LANES = 128
assert LANES == 2 * SWA_HEAD_DIM and CHUNK == LANES and RET_V_W == D_MODEL

F32 = jnp.float32
BF16 = jnp.bfloat16

BLOCK_ROWS = 512
RET_CHUNK = 256
assert BLOCK_ROWS % RET_CHUNK == 0 and BLOCK_ROWS % CHUNK == 0


def _sigmoid(x):
    return 1.0 / (1.0 + jnp.exp(-x))


def _dot(a, b):
    return jnp.dot(a, b, preferred_element_type=F32)


def _dot_nt(a, b):
    return lax.dot_general(a, b, (((1,), (1,)), ((), ())), preferred_element_type=F32)


def _retention_log_gamma():
    return np.log1p(-(2.0 ** (-5.0 - np.arange(RET_HEADS, dtype=np.float64))))


IN_N_CHUNK = 512


def _head_pair_rms(y, g2):
    rows, width = y.shape
    low = lax.broadcasted_iota(jnp.int32, (rows, LANES), 1) < SWA_HEAD_DIM
    out = []
    for c in range(0, width, LANES):
        yb = y[:, c:c + LANES]
        y2 = yb * yb
        ms_lo = jnp.sum(jnp.where(low, y2, 0.0), axis=-1, keepdims=True) * (1.0 / SWA_HEAD_DIM)
        ms_hi = jnp.sum(jnp.where(low, 0.0, y2), axis=-1, keepdims=True) * (1.0 / SWA_HEAD_DIM)
        scale = jnp.where(low, lax.rsqrt(ms_lo + EPS), lax.rsqrt(ms_hi + EPS))
        out.append(yb * scale * g2)
    return jnp.concatenate(out, axis=1) if len(out) > 1 else out[0]


def _inproj_kernel(x_ref, g_ref, qg_ref, kg_ref, dec_ref, w_ref, mix_ref, kdt_ref, gate_ref):
    x = x_ref[...]
    ms = jnp.mean(x * x, axis=-1, keepdims=True)
    hn = (x * lax.rsqrt(ms + EPS) * g_ref[...]).astype(BF16)

    def proj(lo, width):
        return _dot(hn, w_ref[:, lo:lo + width])

    for c in range(0, SWA_Q_W, IN_N_CHUNK):
        mix_ref[:, MIX_SQ + c:MIX_SQ + c + IN_N_CHUNK] = _head_pair_rms(
            proj(W_SQ + c, IN_N_CHUNK), qg_ref[...]).astype(BF16)
    for c in range(0, RET_V_W, IN_N_CHUNK):
        mix_ref[:, MIX_RV + c:MIX_RV + c + IN_N_CHUNK] = proj(W_RV + c, IN_N_CHUNK).astype(BF16)
    mix_ref[:, MIX_RQ:MIX_RQ + RET_QK_W] = (proj(W_RQ, RET_QK_W) * dec_ref[:, 0:RET_QK_W]).astype(BF16)
    kd = proj(W_RK, RET_QK_W) * dec_ref[:, RET_QK_W:2 * RET_QK_W]
    kdt_ref[...] = kd.T.astype(BF16)
    mix_ref[:, MIX_SK:MIX_SK + SWA_KV_W] = _head_pair_rms(proj(W_SK, SWA_KV_W), kg_ref[...]).astype(BF16)
    mix_ref[:, MIX_SV:MIX_SV + SWA_KV_W] = proj(W_SV, SWA_KV_W).astype(BF16)
    for c in range(0, GATE_W, IN_N_CHUNK):
        gate_ref[:, c:c + IN_N_CHUNK] = proj(W_GATES + c, IN_N_CHUNK).astype(BF16)


def _inproj(x3d, g, qg2, kg2, dec, w_bf16, block_rows):
    b, seq, _ = x3d.shape
    assert seq % block_rows == 0 and dec.shape == (block_rows, 2 * RET_QK_W)
    steps = seq // block_rows
    const = lambda shape: pl.BlockSpec(shape, lambda i: (0, 0))
    return pl.pallas_call(
        _inproj_kernel,
        out_shape=(jax.ShapeDtypeStruct((b, seq, MIX_W), BF16),
                   jax.ShapeDtypeStruct((b, RET_QK_W, seq), BF16),
                   jax.ShapeDtypeStruct((b, seq, GATE_W), BF16)),
        grid=(b * steps,),
        in_specs=[
            pl.BlockSpec((None, block_rows, D_MODEL), lambda i: (i // steps, i % steps, 0)),
            const((1, D_MODEL)),
            const((1, LANES)),
            const((1, LANES)),
            const((block_rows, 2 * RET_QK_W)),
            pl.BlockSpec((D_MODEL, D_IN), lambda i: (0, 0), pipeline_mode=pl.Buffered(1)),
        ],
        out_specs=(pl.BlockSpec((None, block_rows, MIX_W), lambda i: (i // steps, i % steps, 0)),
                   pl.BlockSpec((None, RET_QK_W, block_rows), lambda i: (i // steps, 0, i % steps)),
                   pl.BlockSpec((None, block_rows, GATE_W), lambda i: (i // steps, i % steps, 0))),
        compiler_params=pltpu.CompilerParams(
            dimension_semantics=("arbitrary",),
            vmem_limit_bytes=V7X_VMEM_LIMIT_BYTES,
        ),
        name="inproj",
    )(x3d, g, qg2, kg2, dec, w_bf16)


def _decay_tables(block_rows):
    lg = _retention_log_gamma()
    pos = (np.arange(block_rows) % RET_CHUNK).astype(np.float64) + 1.0
    qd = np.exp(lg[None, :] * pos[:, None])
    kd = np.exp(-lg[None, :] * pos[:, None]) * (RET_QK_DIM ** -0.5)
    rep = lambda t: np.repeat(t, RET_QK_DIM, axis=1)
    return np.concatenate([rep(qd), rep(kd)], axis=1).astype(np.float32)


def _meta_decay_tables():
    lg = _retention_log_gamma()
    pos = np.arange(CHUNK, dtype=np.float64)
    kd = np.exp(lg[None, :] * (CHUNK - 1.0 - pos[:, None])) * (RET_QK_DIM ** -0.5)
    rep = lambda t: np.repeat(t, RET_QK_DIM, axis=1)
    return np.concatenate([np.ones((CHUNK, RET_QK_W)), rep(kd)], axis=1).astype(np.float32)


RET_CHUNK_DECAY = [float(c) for c in np.exp(_retention_log_gamma() * RET_CHUNK)]
PAIR_ROWS = 2 * CHUNK
BACK_ROWS = 256
assert BACK_ROWS % RET_CHUNK == 0 and BLOCK_ROWS % BACK_ROWS == 0
CHUNKS_PER_BLOCK = BACK_ROWS // CHUNK
RET_PER_BLOCK = BACK_ROWS // RET_CHUNK
KV_SLOTS = CHUNKS_PER_BLOCK + 1
FF_CHUNK = 1024


def _swa_consts():
    slopes = 2.0 ** (-8.0 * np.arange(1, SWA_Q_HEADS + 1, dtype=np.float64) / SWA_Q_HEADS)
    slopes = slopes.astype(np.float32)
    i = np.arange(CHUNK)[:, None]
    j = np.arange(CHUNK)[None, :]
    band_dist = np.where(j <= i, i - j, i - j + CHUNK).astype(np.float32)
    meta_dist = (i - (j % N_META) + N_META).astype(np.float32)
    band_dist = np.stack([np.where(j <= i, band_dist, np.inf), band_dist]).astype(np.float32)
    meta_bias = np.full((SWA_KV_HEADS, PAIR_ROWS, CHUNK), np.inf, np.float32)
    meta_slope = np.zeros((SWA_KV_HEADS, PAIR_ROWS, CHUNK), np.float32)
    for h in range(SWA_Q_HEADS):
        g, r = divmod(h, SWA_GROUP)
        p, e = divmod(r, 2)
        rows = slice(p * CHUNK, (p + 1) * CHUNK)
        mcols = slice(e * N_META, (e + 1) * N_META)
        meta_bias[g, rows, mcols] = (slopes[h] * meta_dist)[:, mcols]
        meta_slope[g, rows, mcols] = slopes[h] * np.float32(CHUNK)
    return [float(s) for s in slopes], band_dist, meta_bias, meta_slope


ALIBI_SLOPES = _swa_consts()[0]


def _back_kernel(blocks_per_batch,
                 sink_ref,
                 sq_ref, rv_ref, qd_ref, sk_ref, sv_ref, kdt_ref, rg_ref,
                 mk_ref, mv_ref, mkdt_ref, mrv_ref,
                 band_dist_ref, meta_bias_ref, meta_slope_ref, rn_ref,
                 x_ref, gr_ref, gs_ref, wr_ref, ws_ref, wo_ref, g_ref, wu_ref, wd_ref,
                 o_ref,
                 bd_ref, vbuf_ref, bdm_ref, vm_ref, state_ref, ys_ref, yr_ref, hn_ref):
    t = pl.program_id(0)
    block = jnp.minimum(t, pl.num_programs(0) - 2)
    block_in_batch = block % blocks_per_batch
    parity = t & 1
    last_slot = CHUNKS_PER_BLOCK - 1 + parity
    carried_slot = CHUNKS_PER_BLOCK - parity
    zeros_head = jnp.zeros((CHUNK, SWA_HEAD_DIM), BF16)

    @pl.when(t == 0)
    def _():
        ys_ref[...] = jnp.zeros_like(ys_ref)
        yr_ref[...] = jnp.zeros_like(yr_ref)

    @pl.when(block_in_batch == 0)
    def _():
        bd_ref[...] = jnp.zeros_like(bd_ref)
        vbuf_ref[...] = jnp.zeros_like(vbuf_ref)
        bdm_ref[...] = jnp.zeros_like(bdm_ref)
        vm_ref[...] = jnp.zeros_like(vm_ref)
        zm = jnp.zeros((N_META, SWA_HEAD_DIM), BF16)
        for g in range(SWA_KV_HEADS):
            cols = slice(g * SWA_HEAD_DIM, (g + 1) * SWA_HEAD_DIM)
            km = mk_ref[:, cols]
            bdm_ref[g, 0:N_META, :] = jnp.concatenate([km, zm], axis=1)
            bdm_ref[g, N_META:2 * N_META, :] = jnp.concatenate([zm, km], axis=1)
            vm_ref[g, 0:N_META, :] = mv_ref[:, cols]
            vm_ref[g, N_META:2 * N_META, :] = mv_ref[:, cols]
        for h in range(RET_HEADS):
            state_ref[h] = _dot(mkdt_ref[h * RET_QK_DIM:(h + 1) * RET_QK_DIM, :],
                                mrv_ref[:, h * RET_V_DIM:(h + 1) * RET_V_DIM])

    merged = (_sigmoid(gr_ref[...].astype(F32)) * _dot(yr_ref[...], wr_ref[...])
              + _sigmoid(gs_ref[...].astype(F32)) * _dot(ys_ref[...], ws_ref[...]))
    o_ref[...] = x_ref[...] + _dot(merged.astype(BF16), wo_ref[...])
    h1 = o_ref[...]
    hn_ref[...] = (h1 * lax.rsqrt(jnp.mean(h1 * h1, axis=-1, keepdims=True) + EPS)
                   * g_ref[...]).astype(BF16)

    ret_row = lax.broadcasted_iota(jnp.int32, (RET_CHUNK, RET_CHUNK), 0)
    ret_col = lax.broadcasted_iota(jnp.int32, (RET_CHUNK, RET_CHUNK), 1)
    ret_causal = ret_col <= ret_row

    def retention_head(sub, h):
        rows = slice(sub * RET_CHUNK, (sub + 1) * RET_CHUNK)
        vcols = slice(h * RET_V_DIM, (h + 1) * RET_V_DIM)
        qd = qd_ref[rows, h * RET_QK_DIM:(h + 1) * RET_QK_DIM]
        kdt = kdt_ref[h * RET_QK_DIM:(h + 1) * RET_QK_DIM, rows]
        v = rv_ref[rows, vcols]
        state = state_ref[h]
        s = jnp.where(ret_causal, _dot(qd, kdt), 0.0).astype(BF16)
        o = _dot(s, v) + _dot(qd, state.astype(BF16))
        state_ref[h] = RET_CHUNK_DECAY[h] * (state + _dot(kdt, v))
        ms = jnp.mean(o * o, axis=-1, keepdims=True)
        gate = rg_ref[rows, vcols].astype(F32)
        yr_ref[rows, vcols] = (o * lax.rsqrt(ms + EPS) * rn_ref[:, vcols]
                               * (gate * _sigmoid(gate))).astype(BF16)

    row = lax.broadcasted_iota(jnp.int32, (CHUNK, CHUNK), 0)
    col = lax.broadcasted_iota(jnp.int32, (CHUNK, CHUNK), 1)
    own = col <= row
    own2 = jnp.concatenate([own, own], axis=1)
    own4 = jnp.concatenate([own2, own2], axis=0)
    meta_seg = [col < N_META, (col >= N_META) & (col < 2 * N_META)]
    neg_inf = jnp.float32(-jnp.inf)

    first_chunk = (CHUNKS_PER_BLOCK * block_in_batch).astype(F32)
    chunks = []
    for c in range(CHUNKS_PER_BLOCK):
        slot = c if c < CHUNKS_PER_BLOCK - 1 else last_slot
        prev_slot = c - 1 if c > 0 else carried_slot
        bias_sel = jnp.minimum(block_in_batch, 1) if c == 0 else 1
        chunks.append((slice(c * CHUNK, (c + 1) * CHUNK), slot, prev_slot, bias_sel, first_chunk + float(c)))

    for rows, slot, _, _, _ in chunks:
        for g in range(SWA_KV_HEADS):
            kg = sk_ref[rows, g * SWA_HEAD_DIM:(g + 1) * SWA_HEAD_DIM]
            bd_ref[slot, g, 0:CHUNK, :] = jnp.concatenate([kg, zeros_head], axis=1)
            bd_ref[slot, g, CHUNK:2 * CHUNK, :] = jnp.concatenate([zeros_head, kg], axis=1)
        vbuf_ref[slot] = sv_ref[rows, :]

    def scores(chunk, g):
        rows, slot, prev_slot, bias_sel, chunk_idx = chunk
        q2 = jnp.concatenate(
            [sq_ref[rows, (2 * g + p) * LANES:(2 * g + p + 1) * LANES] for p in range(2)],
            axis=0)
        s_own = _dot_nt(q2, bd_ref[slot, g])
        s_prev = _dot_nt(q2, bd_ref[prev_slot, g])
        dist = band_dist_ref[bias_sel]
        bias = jnp.concatenate(
            [jnp.concatenate([ALIBI_SLOPES[g * SWA_GROUP + 2 * p + e] * dist for e in range(2)], axis=1)
             for p in range(2)], axis=0)
        s = jnp.where(own4, s_own, s_prev) - bias
        s_meta = _dot_nt(q2, bdm_ref[g]) - (meta_bias_ref[g] + meta_slope_ref[g] * chunk_idx)
        return s, s_meta

    def softmax(g, s, s_meta):
        lhs = []
        inv = []
        for p in range(2):
            sm_p = s_meta[p * CHUNK:(p + 1) * CHUNK]
            for e in range(2):
                h = g * SWA_GROUP + 2 * p + e
                sink = sink_ref[h]
                sb = s[p * CHUNK:(p + 1) * CHUNK, e * CHUNK:(e + 1) * CHUNK]
                sm = jnp.where(meta_seg[e], sm_p, neg_inf)
                m = jnp.maximum(jnp.max(jnp.maximum(sb, sm), axis=-1, keepdims=True), sink)
                pb = jnp.exp(sb - m)
                pm = jnp.exp(sm - m)
                denom = jnp.sum(pb + pm, axis=-1, keepdims=True) + jnp.exp(sink - m)
                inv.append(1.0 / denom)
                lhs.append(jnp.concatenate(
                    [jnp.where(own, pb, 0.0).astype(BF16), jnp.where(own, 0.0, pb).astype(BF16),
                     pm.astype(BF16)], axis=1))
        return jnp.concatenate(lhs, axis=0), inv

    def weighted_values(chunk, g, lhs, inv):
        rows, slot, prev_slot, _, _ = chunk
        cols = slice(g * SWA_HEAD_DIM, (g + 1) * SWA_HEAD_DIM)
        vv = jnp.concatenate([vbuf_ref[slot, :, cols], vbuf_ref[prev_slot, :, cols], vm_ref[g]], axis=0)
        acc = _dot(lhs, vv)
        for r in range(SWA_GROUP):
            h = g * SWA_GROUP + r
            ys_ref[rows, h * SWA_HEAD_DIM:(h + 1) * SWA_HEAD_DIM] = (
                acc[r * CHUNK:(r + 1) * CHUNK] * inv[r]).astype(BF16)

    tasks = [(chunk, g) for chunk in chunks for g in range(SWA_KV_HEADS)]
    ret_tasks = [(sub, h) for sub in range(RET_PER_BLOCK) for h in range(RET_HEADS)]
    assert len(tasks) == 2 * len(ret_tasks)

    def mlp_phase(k):
        c = k * FF_CHUNK
        u = jnp.maximum(_dot(hn_ref[...], wu_ref[:, c:c + FF_CHUNK]), 0.0)
        o_ref[...] += _dot((u * u).astype(BF16), wd_ref[c:c + FF_CHUNK, :])

    n_mlp_phases = D_FF // FF_CHUNK
    phase_after_task = {(i + 1) * len(tasks) // n_mlp_phases - 1: i for i in range(n_mlp_phases)}

    pending_scores = scores(*tasks[0])
    pending_probs = None
    for i, (chunk, g) in enumerate(tasks):
        current_scores = pending_scores
        if i + 1 < len(tasks):
            pending_scores = scores(*tasks[i + 1])
        probs = softmax(g, *current_scores)
        if i % 2 == 1:
            retention_head(*ret_tasks[i // 2])
        if pending_probs is not None:
            weighted_values(*tasks[i - 1], *pending_probs)
        pending_probs = probs
        if i in phase_after_task:
            mlp_phase(phase_after_task[i])
    weighted_values(*tasks[-1], *pending_probs)


def _back(x2d, mix2d, kdt, gates2d, meta_mix, meta_kdt, sinks, ret_norm_g,
          w_ret, w_swa, w_out, mlp_g, w_up, w_down, blocks_per_batch):
    rows = x2d.shape[0]
    assert rows % BACK_ROWS == 0
    n_blocks = rows // BACK_ROWS
    _, band_dist, meta_bias, meta_slope = _swa_consts()
    meta_rows = PAD // N_META

    cur = lambda t: jnp.minimum(t, n_blocks - 1)
    prev = lambda t: jnp.maximum(t - 1, 0)
    mix_spec = lambda width, off: pl.BlockSpec((BACK_ROWS, width), lambda t, s: (cur(t), off // width))
    prev_spec = lambda width, j=0: pl.BlockSpec((BACK_ROWS, width), lambda t, s: (prev(t), j))

    def resident(shape):
        zeros = (0,) * len(shape)
        return pl.BlockSpec(shape, lambda t, s: zeros, pipeline_mode=pl.Buffered(1))

    grid_spec = pltpu.PrefetchScalarGridSpec(
        num_scalar_prefetch=1,
        grid=(n_blocks + 1,),
        in_specs=[
            mix_spec(SWA_Q_W, MIX_SQ),
            mix_spec(RET_V_W, MIX_RV),
            mix_spec(RET_QK_W, MIX_RQ),
            mix_spec(SWA_KV_W, MIX_SK),
            mix_spec(SWA_KV_W, MIX_SV),
            pl.BlockSpec((None, RET_QK_W, BACK_ROWS),
                         lambda t, s: (cur(t) // blocks_per_batch, 0, cur(t) % blocks_per_batch)),
            pl.BlockSpec((BACK_ROWS, RET_V_W), lambda t, s: (cur(t), 0)),
            pl.BlockSpec((None, N_META, SWA_KV_W), lambda t, s: (0, meta_rows, MIX_SK // SWA_KV_W)),
            pl.BlockSpec((None, N_META, SWA_KV_W), lambda t, s: (0, meta_rows, MIX_SV // SWA_KV_W)),
            pl.BlockSpec((None, RET_QK_W, CHUNK), lambda t, s: (0, 0, 0)),
            pl.BlockSpec((None, CHUNK, RET_V_W), lambda t, s: (0, 0, MIX_RV // RET_V_W)),
            resident((2, CHUNK, CHUNK)),
            resident((SWA_KV_HEADS, PAIR_ROWS, CHUNK)),
            resident((SWA_KV_HEADS, PAIR_ROWS, CHUNK)),
            resident((1, RET_V_W)),
            prev_spec(D_MODEL),
            prev_spec(D_MODEL, 1),
            prev_spec(D_MODEL, 2),
            resident((RET_V_W, D_MODEL)),
            resident((SWA_Q_W, D_MODEL)),
            resident((D_MODEL, D_MODEL)),
            resident((1, D_MODEL)),
            resident((D_MODEL, D_FF)),
            resident((D_FF, D_MODEL)),
        ],
        out_specs=pl.BlockSpec((BACK_ROWS, D_MODEL), lambda t, s: (prev(t), 0)),
        scratch_shapes=[
            pltpu.VMEM((KV_SLOTS, SWA_KV_HEADS, PAIR_ROWS, LANES), BF16),
            pltpu.VMEM((KV_SLOTS, CHUNK, SWA_KV_W), BF16),
            pltpu.VMEM((SWA_KV_HEADS, CHUNK, LANES), BF16),
            pltpu.VMEM((SWA_KV_HEADS, CHUNK, SWA_HEAD_DIM), BF16),
            pltpu.VMEM((RET_HEADS, RET_QK_DIM, RET_V_DIM), F32),
            pltpu.VMEM((BACK_ROWS, SWA_Q_W), BF16),
            pltpu.VMEM((BACK_ROWS, RET_V_W), BF16),
            pltpu.VMEM((BACK_ROWS, D_MODEL), BF16),
        ],
    )
    return pl.pallas_call(
        functools.partial(_back_kernel, blocks_per_batch),
        out_shape=jax.ShapeDtypeStruct((rows, D_MODEL), F32),
        grid_spec=grid_spec,
        compiler_params=pltpu.CompilerParams(
            dimension_semantics=("arbitrary",),
            vmem_limit_bytes=V7X_VMEM_LIMIT_BYTES,
        ),
        name="back",
    )(sinks.astype(F32),
      mix2d, mix2d, mix2d, mix2d, mix2d, kdt, gates2d,
      meta_mix, meta_mix, meta_kdt, meta_mix,
      jnp.asarray(band_dist), jnp.asarray(meta_bias), jnp.asarray(meta_slope), ret_norm_g,
      x2d, gates2d, gates2d, w_ret, w_swa, w_out, mlp_g, w_up, w_down)


def _permute_w_in(w):
    split = lambda off, width: w[:, off:off + width]
    return jnp.concatenate([
        split(OFF_SQ, SWA_Q_W), split(OFF_RV, RET_V_W), split(OFF_RQ, RET_QK_W), split(OFF_RK, RET_QK_W),
        split(OFF_SK, SWA_KV_W), split(OFF_SV, SWA_KV_W), split(OFF_RG, RET_V_W),
        split(OFF_GR, D_MODEL), split(OFF_GS, D_MODEL)], axis=1).astype(BF16)


def kernel(x, meta_tokens, mix_norm_g, w_in, ret_norm_g, q_norm_g, k_norm_g, sinks,
           w_ret_branch, w_swa_branch, w_out, mlp_norm_g, w_up, w_down):
    b, l, d = x.shape
    assert d == D_MODEL and l % BLOCK_ROWS == 0 and mix_norm_g.shape[0] == 1
    mix_g = mix_norm_g[0].reshape(1, d).astype(F32)
    w_in_b = _permute_w_in(w_in[0])
    qg2 = jnp.tile(q_norm_g[0].astype(F32) * (SWA_HEAD_DIM ** -0.5), 2).reshape(1, LANES)
    kg2 = jnp.tile(k_norm_g[0].astype(F32), 2).reshape(1, LANES)

    mix, kdt, gates = _inproj(x, mix_g, qg2, kg2, jnp.asarray(_decay_tables(BLOCK_ROWS)), w_in_b,
                              BLOCK_ROWS)
    meta_chunk = jnp.pad(meta_tokens.astype(F32), ((PAD, 0), (0, 0)))[None]
    meta_mix, meta_kdt, _ = _inproj(meta_chunk, mix_g, qg2, kg2, jnp.asarray(_meta_decay_tables()),
                                    w_in_b, CHUNK)

    out = _back(
        x.reshape(b * l, d), mix.reshape(b * l, MIX_W), kdt, gates.reshape(b * l, GATE_W),
        meta_mix, meta_kdt, sinks[0], ret_norm_g[0].reshape(1, RET_V_W).astype(F32),
        w_ret_branch[0].astype(BF16), w_swa_branch[0].astype(BF16), w_out[0].astype(BF16),
        mlp_norm_g[0].reshape(1, d).astype(F32), w_up[0].astype(BF16), w_down[0].astype(BF16),
        l // BACK_ROWS)
    return out.reshape(b, l, d)
```

```python
import functools

import numpy as np
import jax
import jax.numpy as jnp
from jax import lax
from jax.experimental import pallas as pl
from jax.experimental.pallas import tpu as pltpu

D_MODEL = 1024
N_META = 16
CHUNK = 128
PAD = CHUNK - N_META
RET_HEADS = 4
RET_QK_DIM = 128
RET_V_DIM = 256
SWA_Q_HEADS = 16
SWA_KV_HEADS = 4
SWA_GROUP = SWA_Q_HEADS // SWA_KV_HEADS
SWA_HEAD_DIM = 64
D_FF = 4 * D_MODEL
EPS = 1e-6
RET_QK_W = RET_HEADS * RET_QK_DIM
RET_V_W = RET_HEADS * RET_V_DIM
SWA_Q_W = SWA_Q_HEADS * SWA_HEAD_DIM
SWA_KV_W = SWA_KV_HEADS * SWA_HEAD_DIM
D_IN = 2 * RET_QK_W + 2 * RET_V_W + SWA_Q_W + 2 * SWA_KV_W + 2 * D_MODEL

OFF_RQ = 0
OFF_RK = OFF_RQ + RET_QK_W
OFF_RV = OFF_RK + RET_QK_W
OFF_RG = OFF_RV + RET_V_W
OFF_SQ = OFF_RG + RET_V_W
OFF_SK = OFF_SQ + SWA_Q_W
OFF_SV = OFF_SK + SWA_KV_W
OFF_GR = OFF_SV + SWA_KV_W
OFF_GS = OFF_GR + D_MODEL

W_SQ = 0
W_RV = W_SQ + SWA_Q_W
W_RQ = W_RV + RET_V_W
W_RK = W_RQ + RET_QK_W
W_SK = W_RK + RET_QK_W
W_SV = W_SK + SWA_KV_W
W_GATES = W_SV + SWA_KV_W
MIX_SQ = 0
MIX_RV = MIX_SQ + SWA_Q_W
MIX_RQ = MIX_RV + RET_V_W
MIX_SK = MIX_RQ + RET_QK_W
MIX_SV = MIX_SK + SWA_KV_W
MIX_W = MIX_SV + SWA_KV_W
GATE_RG = 2 * D_MODEL
GATE_W = GATE_RG + RET_V_W

V7X_VMEM_LIMIT_BYTES = 60 * 1024 * 1024
LANES = 128
assert LANES == 2 * SWA_HEAD_DIM and CHUNK == LANES and RET_V_W == D_MODEL

F32 = jnp.float32
BF16 = jnp.bfloat16

BLOCK_ROWS = 512
RET_CHUNK = 256
assert BLOCK_ROWS % RET_CHUNK == 0 and BLOCK_ROWS % CHUNK == 0


def _sigmoid(x):
    return 1.0 / (1.0 + jnp.exp(-x))


def _dot(a, b):
    return jnp.dot(a, b, preferred_element_type=F32)


def _dot_nt(a, b):
    return lax.dot_general(a, b, (((1,), (1,)), ((), ())), preferred_element_type=F32)


def _retention_log_gamma():
    return np.log1p(-(2.0 ** (-5.0 - np.arange(RET_HEADS, dtype=np.float64))))


IN_N_CHUNK = 512


def _head_pair_rms(y, g2):
    rows, width = y.shape
    low = lax.broadcasted_iota(jnp.int32, (rows, LANES), 1) < SWA_HEAD_DIM
    out = []
    for c in range(0, width, LANES):
        yb = y[:, c:c + LANES]
        y2 = yb * yb
        ms_lo = jnp.sum(jnp.where(low, y2, 0.0), axis=-1, keepdims=True) * (1.0 / SWA_HEAD_DIM)
        ms_hi = jnp.sum(jnp.where(low, 0.0, y2), axis=-1, keepdims=True) * (1.0 / SWA_HEAD_DIM)
        scale = jnp.where(low, lax.rsqrt(ms_lo + EPS), lax.rsqrt(ms_hi + EPS))
        out.append(yb * scale * g2)
    return jnp.concatenate(out, axis=1) if len(out) > 1 else out[0]


def _inproj_kernel(x_ref, g_ref, qg_ref, kg_ref, dec_ref, w_ref, mix_ref, kdt_ref, gate_ref):
    x = x_ref[...]
    ms = jnp.mean(x * x, axis=-1, keepdims=True)
    hn = (x * lax.rsqrt(ms + EPS) * g_ref[...]).astype(BF16)

    def proj(lo, width):
        return _dot(hn, w_ref[:, lo:lo + width])

    for c in range(0, SWA_Q_W, IN_N_CHUNK):
        mix_ref[:, MIX_SQ + c:MIX_SQ + c + IN_N_CHUNK] = _head_pair_rms(
            proj(W_SQ + c, IN_N_CHUNK), qg_ref[...]).astype(BF16)
    for c in range(0, RET_V_W, IN_N_CHUNK):
        mix_ref[:, MIX_RV + c:MIX_RV + c + IN_N_CHUNK] = proj(W_RV + c, IN_N_CHUNK).astype(BF16)
    mix_ref[:, MIX_RQ:MIX_RQ + RET_QK_W] = (proj(W_RQ, RET_QK_W) * dec_ref[:, 0:RET_QK_W]).astype(BF16)
    kd = proj(W_RK, RET_QK_W) * dec_ref[:, RET_QK_W:2 * RET_QK_W]
    kdt_ref[...] = kd.T.astype(BF16)
    mix_ref[:, MIX_SK:MIX_SK + SWA_KV_W] = _head_pair_rms(proj(W_SK, SWA_KV_W), kg_ref[...]).astype(BF16)
    mix_ref[:, MIX_SV:MIX_SV + SWA_KV_W] = proj(W_SV, SWA_KV_W).astype(BF16)
    for c in range(0, GATE_W, IN_N_CHUNK):
        gate_ref[:, c:c + IN_N_CHUNK] = proj(W_GATES + c, IN_N_CHUNK).astype(BF16)


def _inproj(x3d, g, qg2, kg2, dec, w_bf16, block_rows):
    b, seq, _ = x3d.shape
    assert seq % block_rows == 0 and dec.shape == (block_rows, 2 * RET_QK_W)
    steps = seq // block_rows
    const = lambda shape: pl.BlockSpec(shape, lambda i: (0, 0))
    return pl.pallas_call(
        _inproj_kernel,
        out_shape=(jax.ShapeDtypeStruct((b, seq, MIX_W), BF16),
                   jax.ShapeDtypeStruct((b, RET_QK_W, seq), BF16),
                   jax.ShapeDtypeStruct((b, seq, GATE_W), BF16)),
        grid=(b * steps,),
        in_specs=[
            pl.BlockSpec((None, block_rows, D_MODEL), lambda i: (i // steps, i % steps, 0)),
            const((1, D_MODEL)),
            const((1, LANES)),
            const((1, LANES)),
            const((block_rows, 2 * RET_QK_W)),
            pl.BlockSpec((D_MODEL, D_IN), lambda i: (0, 0), pipeline_mode=pl.Buffered(1)),
        ],
        out_specs=(pl.BlockSpec((None, block_rows, MIX_W), lambda i: (i // steps, i % steps, 0)),
                   pl.BlockSpec((None, RET_QK_W, block_rows), lambda i: (i // steps, 0, i % steps)),
                   pl.BlockSpec((None, block_rows, GATE_W), lambda i: (i // steps, i % steps, 0))),
        compiler_params=pltpu.CompilerParams(
            dimension_semantics=("arbitrary",),
            vmem_limit_bytes=V7X_VMEM_LIMIT_BYTES,
        ),
        name="inproj",
    )(x3d, g, qg2, kg2, dec, w_bf16)


def _decay_tables(block_rows):
    lg = _retention_log_gamma()
    pos = (np.arange(block_rows) % RET_CHUNK).astype(np.float64) + 1.0
    qd = np.exp(lg[None, :] * pos[:, None])
    kd = np.exp(-lg[None, :] * pos[:, None]) * (RET_QK_DIM ** -0.5)
    rep = lambda t: np.repeat(t, RET_QK_DIM, axis=1)
    return np.concatenate([rep(qd), rep(kd)], axis=1).astype(np.float32)


def _meta_decay_tables():
    lg = _retention_log_gamma()
    pos = np.arange(CHUNK, dtype=np.float64)
    kd = np.exp(lg[None, :] * (CHUNK - 1.0 - pos[:, None])) * (RET_QK_DIM ** -0.5)
    rep = lambda t: np.repeat(t, RET_QK_DIM, axis=1)
    return np.concatenate([np.ones((CHUNK, RET_QK_W)), rep(kd)], axis=1).astype(np.float32)


RET_CHUNK_DECAY = [float(c) for c in np.exp(_retention_log_gamma() * RET_CHUNK)]
PAIR_ROWS = 2 * CHUNK
BACK_ROWS = 256
DENSE_ROWS = 256
assert BACK_ROWS % RET_CHUNK == 0 and BLOCK_ROWS % BACK_ROWS == 0 and BACK_ROWS % DENSE_ROWS == 0
CHUNKS_PER_BLOCK = BACK_ROWS // CHUNK
RET_PER_BLOCK = BACK_ROWS // RET_CHUNK
KV_SLOTS = CHUNKS_PER_BLOCK + 1
FF_CHUNK = 1024


def _swa_consts():
    slopes = 2.0 ** (-8.0 * np.arange(1, SWA_Q_HEADS + 1, dtype=np.float64) / SWA_Q_HEADS)
    slopes = slopes.astype(np.float32)
    i = np.arange(CHUNK)[:, None]
    j = np.arange(CHUNK)[None, :]
    band_dist = np.where(j <= i, i - j, i - j + CHUNK).astype(np.float32)
    meta_dist = np.where(j < 2 * N_META, i - (j % N_META) + N_META, np.inf).astype(np.float32)
    band_dist = np.stack([np.where(j <= i, band_dist, np.inf), band_dist]).astype(np.float32)
    return [float(s) for s in slopes], band_dist, meta_dist


ALIBI_SLOPES = _swa_consts()[0]


def _back_kernel(blocks_per_batch,
                 sink_ref,
                 mix_ref, kdt_ref, rg_ref, meta_ref, mkdt_ref, dist_ref, gains_ref,
                 x_ref, gate_ref, wbo_ref, wu_ref, wd_ref,
                 o_ref,
                 bd_ref, vbuf_ref, bdm_ref, vm_ref, state_ref, ys_ref, yr_ref, hn_ref):
    sq_ref = mix_ref.at[:, MIX_SQ:MIX_SQ + SWA_Q_W]
    rv_ref = mix_ref.at[:, MIX_RV:MIX_RV + RET_V_W]
    qd_ref = mix_ref.at[:, MIX_RQ:MIX_RQ + RET_QK_W]
    sk_ref = mix_ref.at[:, MIX_SK:MIX_SK + SWA_KV_W]
    sv_ref = mix_ref.at[:, MIX_SV:MIX_SV + SWA_KV_W]
    mk_ref = meta_ref.at[0, PAD:CHUNK, MIX_SK:MIX_SK + SWA_KV_W]
    mv_ref = meta_ref.at[0, PAD:CHUNK, MIX_SV:MIX_SV + SWA_KV_W]
    mrv_ref = meta_ref.at[0, :, MIX_RV:MIX_RV + RET_V_W]
    mkdt_ref = mkdt_ref.at[0]
    band_dist_ref = dist_ref.at[0:2]
    meta_dist_ref = dist_ref.at[2]
    rn_ref = gains_ref.at[0:1, :]
    g_ref = gains_ref.at[1:2, :]
    gr_ref = gate_ref.at[:, 0:D_MODEL]
    gs_ref = gate_ref.at[:, D_MODEL:2 * D_MODEL]
    wr_ref, ws_ref, wo_ref = wbo_ref.at[0], wbo_ref.at[1], wbo_ref.at[2]
    t = pl.program_id(0)
    block = jnp.minimum(t, pl.num_programs(0) - 2)
    block_in_batch = block % blocks_per_batch
    parity = t & 1
    last_slot = CHUNKS_PER_BLOCK - 1 + parity
    carried_slot = CHUNKS_PER_BLOCK - parity
    zeros_head = jnp.zeros((CHUNK, SWA_HEAD_DIM), BF16)

    @pl.when(t == 0)
    def _():
        ys_ref[...] = jnp.zeros_like(ys_ref)
        yr_ref[...] = jnp.zeros_like(yr_ref)

    @pl.when(block_in_batch == 0)
    def _():
        bd_ref[...] = jnp.zeros_like(bd_ref)
        vbuf_ref[...] = jnp.zeros_like(vbuf_ref)
        bdm_ref[...] = jnp.zeros_like(bdm_ref)
        vm_ref[...] = jnp.zeros_like(vm_ref)
        zm = jnp.zeros((N_META, SWA_HEAD_DIM), BF16)
        for g in range(SWA_KV_HEADS):
            cols = slice(g * SWA_HEAD_DIM, (g + 1) * SWA_HEAD_DIM)
            km = mk_ref[:, cols]
            bdm_ref[g, 0:N_META, :] = jnp.concatenate([km, zm], axis=1)
            bdm_ref[g, N_META:2 * N_META, :] = jnp.concatenate([zm, km], axis=1)
            vmeta = mv_ref[:, cols]
            vm_ref[g, 0:N_META, :] = jnp.concatenate([vmeta, zm], axis=1)
            vm_ref[g, N_META:2 * N_META, :] = jnp.concatenate([zm, vmeta], axis=1)
        for h in range(RET_HEADS):
            state_ref[h] = _dot(mkdt_ref[h * RET_QK_DIM:(h + 1) * RET_QK_DIM, :],
                                mrv_ref[:, h * RET_V_DIM:(h + 1) * RET_V_DIM])

    slabs = [slice(r, r + DENSE_ROWS) for r in range(0, BACK_ROWS, DENSE_ROWS)]
    for rs in slabs:
        merged = (_sigmoid(gr_ref[rs, :].astype(F32)) * _dot(yr_ref[rs, :], wr_ref[...])
                  + _sigmoid(gs_ref[rs, :].astype(F32)) * _dot(ys_ref[rs, :], ws_ref[...]))
        o_ref[rs, :] = x_ref[rs, :] + _dot(merged.astype(BF16), wo_ref[...])
        h1 = o_ref[rs, :]
        hn_ref[rs, :] = (h1 * lax.rsqrt(jnp.mean(h1 * h1, axis=-1, keepdims=True) + EPS)
                         * g_ref[...]).astype(BF16)

    ret_row = lax.broadcasted_iota(jnp.int32, (RET_CHUNK, RET_CHUNK), 0)
    ret_col = lax.broadcasted_iota(jnp.int32, (RET_CHUNK, RET_CHUNK), 1)
    ret_causal = ret_col <= ret_row

    def retention_head(sub, h):
        rows = slice(sub * RET_CHUNK, (sub + 1) * RET_CHUNK)
        vcols = slice(h * RET_V_DIM, (h + 1) * RET_V_DIM)
        qd = qd_ref[rows, h * RET_QK_DIM:(h + 1) * RET_QK_DIM]
        kdt = kdt_ref[h * RET_QK_DIM:(h + 1) * RET_QK_DIM, rows]
        v = rv_ref[rows, vcols]
        state = state_ref[h]
        s = jnp.where(ret_causal, _dot(qd, kdt), 0.0).astype(BF16)
        o = _dot(s, v) + _dot(qd, state.astype(BF16))
        state_ref[h] = RET_CHUNK_DECAY[h] * (state + _dot(kdt, v))
        ms = jnp.mean(o * o, axis=-1, keepdims=True)
        gate = rg_ref[rows, vcols].astype(F32)
        yr_ref[rows, vcols] = (o * lax.rsqrt(ms + EPS) * rn_ref[:, vcols]
                               * (gate * _sigmoid(gate))).astype(BF16)

    row = lax.broadcasted_iota(jnp.int32, (CHUNK, CHUNK), 0)
    col = lax.broadcasted_iota(jnp.int32, (CHUNK, CHUNK), 1)
    own = col <= row
    own2 = jnp.concatenate([own, own], axis=1)
    own4 = jnp.concatenate([own2, own2], axis=0)
    meta_seg = [col < N_META, (col >= N_META) & (col < 2 * N_META)]
    low_half = col < SWA_HEAD_DIM
    neg_inf = jnp.float32(-jnp.inf)

    first_chunk = (CHUNKS_PER_BLOCK * block_in_batch).astype(F32)
    chunks = []
    for c in range(CHUNKS_PER_BLOCK):
        slot = c if c < CHUNKS_PER_BLOCK - 1 else last_slot
        prev_slot = c - 1 if c > 0 else carried_slot
        bias_sel = jnp.minimum(block_in_batch, 1) if c == 0 else 1
        chunks.append((slice(c * CHUNK, (c + 1) * CHUNK), slot, prev_slot, bias_sel, first_chunk + float(c)))

    for rows, slot, _, _, _ in chunks:
        for g in range(SWA_KV_HEADS):
            kg = sk_ref[rows, g * SWA_HEAD_DIM:(g + 1) * SWA_HEAD_DIM]
            bd_ref[slot, g, 0:CHUNK, :] = jnp.concatenate([kg, zeros_head], axis=1)
            bd_ref[slot, g, CHUNK:2 * CHUNK, :] = jnp.concatenate([zeros_head, kg], axis=1)
            vg = sv_ref[rows, g * SWA_HEAD_DIM:(g + 1) * SWA_HEAD_DIM]
            vbuf_ref[slot, g] = jnp.concatenate([vg, vg], axis=1)

    def scores(chunk, g):
        rows, slot, prev_slot, bias_sel, chunk_idx = chunk
        q2 = jnp.concatenate(
            [sq_ref[rows, (2 * g + p) * LANES:(2 * g + p + 1) * LANES] for p in range(2)],
            axis=0)
        s_own = _dot_nt(q2, bd_ref[slot, g])
        s_prev = _dot_nt(q2, bd_ref[prev_slot, g])
        dist = band_dist_ref[bias_sel]
        bias = jnp.concatenate(
            [jnp.concatenate([ALIBI_SLOPES[g * SWA_GROUP + 2 * p + e] * dist for e in range(2)], axis=1)
             for p in range(2)], axis=0)
        s = jnp.where(own4, s_own, s_prev) - bias
        dist_meta = meta_dist_ref[...] + float(CHUNK) * chunk_idx
        bias_meta = jnp.concatenate(
            [jnp.where(meta_seg[0], ALIBI_SLOPES[g * SWA_GROUP + 2 * p],
                       ALIBI_SLOPES[g * SWA_GROUP + 2 * p + 1]) * dist_meta for p in range(2)], axis=0)
        s_meta = _dot_nt(q2, bdm_ref[g]) - bias_meta
        return s, s_meta

    def softmax(g, s, s_meta):
        lhs = []
        lhs_meta = []
        inv = []
        for p in range(2):
            sm_p = s_meta[p * CHUNK:(p + 1) * CHUNK]
            pm_pair = None
            for e in range(2):
                h = g * SWA_GROUP + 2 * p + e
                sink = sink_ref[h]
                sb = s[p * CHUNK:(p + 1) * CHUNK, e * CHUNK:(e + 1) * CHUNK]
                sm = jnp.where(meta_seg[e], sm_p, neg_inf)
                m = jnp.maximum(jnp.max(jnp.maximum(sb, sm), axis=-1, keepdims=True), sink)
                pb = jnp.exp(sb - m)
                pm = jnp.exp(sm - m)
                denom = jnp.sum(pb + pm, axis=-1, keepdims=True) + jnp.exp(sink - m)
                inv.append(1.0 / denom)
                lhs.append(jnp.concatenate(
                    [jnp.where(own, pb, 0.0).astype(BF16), jnp.where(own, 0.0, pb).astype(BF16)], axis=1))
                pm_pair = pm if pm_pair is None else pm_pair + pm
            lhs_meta.append(pm_pair.astype(BF16))
        return jnp.concatenate(lhs, axis=0), jnp.concatenate(lhs_meta, axis=0), inv

    def weighted_values(chunk, g, lhs, lhs_meta, inv):
        rows, slot, prev_slot, _, _ = chunk
        vv = jnp.concatenate([vbuf_ref[slot, g], vbuf_ref[prev_slot, g]], axis=0)
        acc = _dot(lhs, vv)
        acc_meta = _dot(lhs_meta, vm_ref[g])
        for p in range(2):
            first, second = 2 * p, 2 * p + 1
            pair = jnp.where(low_half, acc[first * CHUNK:(first + 1) * CHUNK],
                             acc[second * CHUNK:(second + 1) * CHUNK])
            scale = jnp.where(low_half, inv[first], inv[second])
            tile = (pair + acc_meta[p * CHUNK:(p + 1) * CHUNK]) * scale
            ys_ref[rows, (2 * g + p) * LANES:(2 * g + p + 1) * LANES] = tile.astype(BF16)

    tasks = [(chunk, g) for chunk in chunks for g in range(SWA_KV_HEADS)]
    ret_tasks = [(sub, h) for sub in range(RET_PER_BLOCK) for h in range(RET_HEADS)]
    assert len(tasks) == 2 * len(ret_tasks)

    def mlp_phase(k):
        c = k * FF_CHUNK
        for rs in slabs:
            u = jnp.maximum(_dot(hn_ref[rs, :], wu_ref[:, c:c + FF_CHUNK]), 0.0)
            o_ref[rs, :] += _dot((u * u).astype(BF16), wd_ref[c:c + FF_CHUNK, :])

    n_mlp_phases = D_FF // FF_CHUNK
    phase_at_task = {i * len(tasks) // n_mlp_phases: i for i in range(n_mlp_phases)}

    pending_scores = scores(*tasks[0])
    pending_probs = None
    for i, (chunk, g) in enumerate(tasks):
        current_scores = pending_scores
        if i + 1 < len(tasks):
            pending_scores = scores(*tasks[i + 1])
        if i in phase_at_task:
            mlp_phase(phase_at_task[i])
        probs = softmax(g, *current_scores)
        if i % 2 == 1:
            retention_head(*ret_tasks[i // 2])
        if pending_probs is not None:
            weighted_values(*tasks[i - 1], *pending_probs)
        pending_probs = probs
    weighted_values(*tasks[-1], *pending_probs)


def _back(x2d, mix2d, kdt, gates2d, meta_mix, meta_kdt, sinks, norm_gains, w_branch_out, w_up, w_down,
          blocks_per_batch):
    rows = x2d.shape[0]
    assert rows % BACK_ROWS == 0
    n_blocks = rows // BACK_ROWS
    _, band_dist, meta_dist = _swa_consts()

    cur = lambda t: jnp.minimum(t, n_blocks - 1)
    prev = lambda t: jnp.maximum(t - 1, 0)

    def resident(shape):
        zeros = (0,) * len(shape)
        return pl.BlockSpec(shape, lambda t, s: zeros, pipeline_mode=pl.Buffered(1))

    grid_spec = pltpu.PrefetchScalarGridSpec(
        num_scalar_prefetch=1,
        grid=(n_blocks + 1,),
        in_specs=[
            pl.BlockSpec((BACK_ROWS, MIX_W), lambda t, s: (cur(t), 0)),
            pl.BlockSpec((None, RET_QK_W, BACK_ROWS),
                         lambda t, s: (cur(t) // blocks_per_batch, 0, cur(t) % blocks_per_batch)),
            pl.BlockSpec((BACK_ROWS, RET_V_W), lambda t, s: (cur(t), GATE_RG // RET_V_W)),
            resident((1, CHUNK, MIX_W)),
            resident((1, RET_QK_W, CHUNK)),
            resident((3, CHUNK, CHUNK)),
            resident((2, D_MODEL)),
            pl.BlockSpec((BACK_ROWS, D_MODEL), lambda t, s: (prev(t), 0)),
            pl.BlockSpec((BACK_ROWS, 2 * D_MODEL), lambda t, s: (prev(t), 0)),
            resident((3, D_MODEL, D_MODEL)),
            resident((D_MODEL, D_FF)),
            resident((D_FF, D_MODEL)),
        ],
        out_specs=pl.BlockSpec((BACK_ROWS, D_MODEL), lambda t, s: (prev(t), 0)),
        scratch_shapes=[
            pltpu.VMEM((KV_SLOTS, SWA_KV_HEADS, PAIR_ROWS, LANES), BF16),
            pltpu.VMEM((KV_SLOTS, SWA_KV_HEADS, CHUNK, LANES), BF16),
            pltpu.VMEM((SWA_KV_HEADS, CHUNK, LANES), BF16),
            pltpu.VMEM((SWA_KV_HEADS, CHUNK, LANES), BF16),
            pltpu.VMEM((RET_HEADS, RET_QK_DIM, RET_V_DIM), F32),
            pltpu.VMEM((BACK_ROWS, SWA_Q_W), BF16),
            pltpu.VMEM((BACK_ROWS, RET_V_W), BF16),
            pltpu.VMEM((BACK_ROWS, D_MODEL), BF16),
        ],
    )
    return pl.pallas_call(
        functools.partial(_back_kernel, blocks_per_batch),
        out_shape=jax.ShapeDtypeStruct((rows, D_MODEL), F32),
        grid_spec=grid_spec,
        compiler_params=pltpu.CompilerParams(
            dimension_semantics=("arbitrary",),
            vmem_limit_bytes=V7X_VMEM_LIMIT_BYTES,
        ),
        name="back",
    )(sinks.astype(F32), mix2d, kdt, gates2d, meta_mix, meta_kdt,
      jnp.asarray(np.concatenate([band_dist, meta_dist[None]])), norm_gains,
      x2d, gates2d, w_branch_out, w_up, w_down)


def _permute_w_in(w):
    split = lambda off, width: w[:, off:off + width]
    return jnp.concatenate([
        split(OFF_SQ, SWA_Q_W), split(OFF_RV, RET_V_W), split(OFF_RQ, RET_QK_W), split(OFF_RK, RET_QK_W),
        split(OFF_SK, SWA_KV_W), split(OFF_SV, SWA_KV_W),
        split(OFF_GR, D_MODEL), split(OFF_GS, D_MODEL), split(OFF_RG, RET_V_W)], axis=1).astype(BF16)


def kernel(x, meta_tokens, mix_norm_g, w_in, ret_norm_g, q_norm_g, k_norm_g, sinks,
           w_ret_branch, w_swa_branch, w_out, mlp_norm_g, w_up, w_down):
    b, l, d = x.shape
    assert d == D_MODEL and l % BLOCK_ROWS == 0 and mix_norm_g.shape[0] == 1
    mix_g = mix_norm_g[0].reshape(1, d).astype(F32)
    w_in_b = _permute_w_in(w_in[0])
    qg2 = jnp.tile(q_norm_g[0].astype(F32) * (SWA_HEAD_DIM ** -0.5), 2).reshape(1, LANES)
    kg2 = jnp.tile(k_norm_g[0].astype(F32), 2).reshape(1, LANES)

    mix, kdt, gates = _inproj(x, mix_g, qg2, kg2, jnp.asarray(_decay_tables(BLOCK_ROWS)), w_in_b,
                              BLOCK_ROWS)
    meta_chunk = jnp.pad(meta_tokens.astype(F32), ((PAD, 0), (0, 0)))[None]
    meta_mix, meta_kdt, _ = _inproj(meta_chunk, mix_g, qg2, kg2, jnp.asarray(_meta_decay_tables()),
                                    w_in_b, CHUNK)

    norm_gains = jnp.stack([ret_norm_g[0].reshape(RET_V_W), mlp_norm_g[0]]).astype(F32)
    w_branch_out = jnp.stack([w_ret_branch[0], w_swa_branch[0], w_out[0]]).astype(BF16)
    out = _back(
        x.reshape(b * l, d), mix.reshape(b * l, MIX_W), kdt, gates.reshape(b * l, GATE_W),
        meta_mix, meta_kdt, sinks[0], norm_gains, w_branch_out,
        w_up[0].astype(BF16), w_down[0].astype(BF16), l // BACK_ROWS)
    return out.reshape(b, l, d)
```

```python
import functools

import numpy as np
import jax
import jax.numpy as jnp
from jax import lax
from jax.experimental import pallas as pl
from jax.experimental.pallas import tpu as pltpu

D_MODEL = 1024
N_META = 16
CHUNK = 128
PAD = CHUNK - N_META
RET_HEADS = 4
RET_QK_DIM = 128
RET_V_DIM = 256
SWA_Q_HEADS = 16
SWA_KV_HEADS = 4
SWA_GROUP = SWA_Q_HEADS // SWA_KV_HEADS
SWA_HEAD_DIM = 64
D_FF = 4 * D_MODEL
EPS = 1e-6
RET_QK_W = RET_HEADS * RET_QK_DIM
RET_V_W = RET_HEADS * RET_V_DIM
SWA_Q_W = SWA_Q_HEADS * SWA_HEAD_DIM
SWA_KV_W = SWA_KV_HEADS * SWA_HEAD_DIM
D_IN = 2 * RET_QK_W + 2 * RET_V_W + SWA_Q_W + 2 * SWA_KV_W + 2 * D_MODEL

OFF_RQ = 0
OFF_RK = OFF_RQ + RET_QK_W
OFF_RV = OFF_RK + RET_QK_W
OFF_RG = OFF_RV + RET_V_W
OFF_SQ = OFF_RG + RET_V_W
OFF_SK = OFF_SQ + SWA_Q_W
OFF_SV = OFF_SK + SWA_KV_W
OFF_GR = OFF_SV + SWA_KV_W
OFF_GS = OFF_GR + D_MODEL

W_SQ = 0
W_RV = W_SQ + SWA_Q_W
W_RQ = W_RV + RET_V_W
W_RK = W_RQ + RET_QK_W
W_SK = W_RK + RET_QK_W
W_SV = W_SK + SWA_KV_W
W_GATES = W_SV + SWA_KV_W
MIX_SQ = 0
MIX_RV = MIX_SQ + SWA_Q_W
MIX_RQ = MIX_RV + RET_V_W
MIX_SK = MIX_RQ + RET_QK_W
MIX_SV = MIX_SK + SWA_KV_W
MIX_W = MIX_SV + SWA_KV_W
GATE_RG = 2 * D_MODEL
GATE_W = GATE_RG + RET_V_W

V7X_VMEM_LIMIT_BYTES = 64 * 1024 * 1024
LANES = 128
assert LANES == 2 * SWA_HEAD_DIM and CHUNK == LANES and RET_V_W == D_MODEL

F32 = jnp.float32
BF16 = jnp.bfloat16

BLOCK_ROWS = 512
RET_CHUNK = 256
assert BLOCK_ROWS % RET_CHUNK == 0 and BLOCK_ROWS % CHUNK == 0


def _sigmoid(x):
    return 1.0 / (1.0 + jnp.exp(-x))


def _dot(a, b):
    return jnp.dot(a, b, preferred_element_type=F32)


def _dot_nt(a, b):
    return lax.dot_general(a, b, (((1,), (1,)), ((), ())), preferred_element_type=F32)


def _retention_log_gamma():
    return np.log1p(-(2.0 ** (-5.0 - np.arange(RET_HEADS, dtype=np.float64))))


IN_N_CHUNK = 512


def _head_pair_rms(y, g2):
    rows, width = y.shape
    low = lax.broadcasted_iota(jnp.int32, (rows, LANES), 1) < SWA_HEAD_DIM
    out = []
    for c in range(0, width, LANES):
        yb = y[:, c:c + LANES]
        y2 = yb * yb
        ms_lo = jnp.sum(jnp.where(low, y2, 0.0), axis=-1, keepdims=True) * (1.0 / SWA_HEAD_DIM)
        ms_hi = jnp.sum(jnp.where(low, 0.0, y2), axis=-1, keepdims=True) * (1.0 / SWA_HEAD_DIM)
        scale = jnp.where(low, lax.rsqrt(ms_lo + EPS), lax.rsqrt(ms_hi + EPS))
        out.append(yb * scale * g2)
    return jnp.concatenate(out, axis=1) if len(out) > 1 else out[0]


def _inproj_kernel(x_ref, g_ref, qg_ref, kg_ref, dec_ref, w_ref, mix_ref, kdt_ref, gate_ref):
    x = x_ref[...]
    ms = jnp.mean(x * x, axis=-1, keepdims=True)
    hn = (x * lax.rsqrt(ms + EPS) * g_ref[...]).astype(BF16)

    def proj(lo, width):
        return _dot(hn, w_ref[:, lo:lo + width])

    for c in range(0, SWA_Q_W, IN_N_CHUNK):
        mix_ref[:, MIX_SQ + c:MIX_SQ + c + IN_N_CHUNK] = _head_pair_rms(
            proj(W_SQ + c, IN_N_CHUNK), qg_ref[...]).astype(BF16)
    for c in range(0, RET_V_W, IN_N_CHUNK):
        mix_ref[:, MIX_RV + c:MIX_RV + c + IN_N_CHUNK] = proj(W_RV + c, IN_N_CHUNK).astype(BF16)
    mix_ref[:, MIX_RQ:MIX_RQ + RET_QK_W] = (proj(W_RQ, RET_QK_W) * dec_ref[:, 0:RET_QK_W]).astype(BF16)
    kd = proj(W_RK, RET_QK_W) * dec_ref[:, RET_QK_W:2 * RET_QK_W]
    kdt_ref[...] = kd.T.astype(BF16)
    mix_ref[:, MIX_SK:MIX_SK + SWA_KV_W] = _head_pair_rms(proj(W_SK, SWA_KV_W), kg_ref[...]).astype(BF16)
    mix_ref[:, MIX_SV:MIX_SV + SWA_KV_W] = proj(W_SV, SWA_KV_W).astype(BF16)
    for c in range(0, GATE_W, IN_N_CHUNK):
        gate_ref[:, c:c + IN_N_CHUNK] = proj(W_GATES + c, IN_N_CHUNK).astype(BF16)


def _inproj(x3d, g, qg2, kg2, dec, w_bf16, block_rows):
    b, seq, _ = x3d.shape
    assert seq % block_rows == 0 and dec.shape == (block_rows, 2 * RET_QK_W)
    steps = seq // block_rows
    const = lambda shape: pl.BlockSpec(shape, lambda i: (0, 0))
    return pl.pallas_call(
        _inproj_kernel,
        out_shape=(jax.ShapeDtypeStruct((b, seq, MIX_W), BF16),
                   jax.ShapeDtypeStruct((b, RET_QK_W, seq), BF16),
                   jax.ShapeDtypeStruct((b, seq, GATE_W), BF16)),
        grid=(b * steps,),
        in_specs=[
            pl.BlockSpec((None, block_rows, D_MODEL), lambda i: (i // steps, i % steps, 0)),
            const((1, D_MODEL)),
            const((1, LANES)),
            const((1, LANES)),
            const((block_rows, 2 * RET_QK_W)),
            pl.BlockSpec((D_MODEL, D_IN), lambda i: (0, 0), pipeline_mode=pl.Buffered(1)),
        ],
        out_specs=(pl.BlockSpec((None, block_rows, MIX_W), lambda i: (i // steps, i % steps, 0)),
                   pl.BlockSpec((None, RET_QK_W, block_rows), lambda i: (i // steps, 0, i % steps)),
                   pl.BlockSpec((None, block_rows, GATE_W), lambda i: (i // steps, i % steps, 0))),
        compiler_params=pltpu.CompilerParams(
            dimension_semantics=("arbitrary",),
            vmem_limit_bytes=V7X_VMEM_LIMIT_BYTES,
        ),
        name="inproj",
    )(x3d, g, qg2, kg2, dec, w_bf16)


def _decay_tables(block_rows):
    lg = _retention_log_gamma()
    pos = (np.arange(block_rows) % RET_CHUNK).astype(np.float64) + 1.0
    qd = np.exp(lg[None, :] * pos[:, None])
    kd = np.exp(-lg[None, :] * pos[:, None]) * (RET_QK_DIM ** -0.5)
    rep = lambda t: np.repeat(t, RET_QK_DIM, axis=1)
    return np.concatenate([rep(qd), rep(kd)], axis=1).astype(np.float32)


def _meta_decay_tables():
    lg = _retention_log_gamma()
    pos = np.arange(CHUNK, dtype=np.float64)
    kd = np.exp(lg[None, :] * (CHUNK - 1.0 - pos[:, None])) * (RET_QK_DIM ** -0.5)
    rep = lambda t: np.repeat(t, RET_QK_DIM, axis=1)
    return np.concatenate([np.ones((CHUNK, RET_QK_W)), rep(kd)], axis=1).astype(np.float32)


RET_CHUNK_DECAY = [float(c) for c in np.exp(_retention_log_gamma() * RET_CHUNK)]
PAIR_ROWS = 2 * CHUNK
BACK_ROWS = 512
DENSE_ROWS = 256
assert BACK_ROWS % RET_CHUNK == 0 and BLOCK_ROWS % BACK_ROWS == 0 and BACK_ROWS % DENSE_ROWS == 0
CHUNKS_PER_BLOCK = BACK_ROWS // CHUNK
RET_PER_BLOCK = BACK_ROWS // RET_CHUNK
KV_SLOTS = CHUNKS_PER_BLOCK + 1
FF_CHUNK = 1024


def _swa_consts():
    slopes = 2.0 ** (-8.0 * np.arange(1, SWA_Q_HEADS + 1, dtype=np.float64) / SWA_Q_HEADS)
    slopes = slopes.astype(np.float32)
    i = np.arange(CHUNK)[:, None]
    j = np.arange(CHUNK)[None, :]
    band_dist = np.where(j <= i, i - j, i - j + CHUNK).astype(np.float32)
    meta_dist = np.where(j < 2 * N_META, i - (j % N_META) + N_META, np.inf).astype(np.float32)
    band_dist = np.stack([np.where(j <= i, band_dist, np.inf), band_dist]).astype(np.float32)
    return [float(s) for s in slopes], band_dist, meta_dist


ALIBI_SLOPES = _swa_consts()[0]


def _back_kernel(blocks_per_batch,
                 sink_ref,
                 mix_ref, kdt_ref, rg_ref, meta_ref, mkdt_ref, dist_ref, gains_ref,
                 x_ref, gate_ref, wbo_ref, wu_ref, wd_ref,
                 o_ref,
                 bd_ref, vbuf_ref, bdm_ref, vm_ref, state_ref, ys_ref, yr_ref, hn_ref):
    sq_ref = mix_ref.at[:, MIX_SQ:MIX_SQ + SWA_Q_W]
    rv_ref = mix_ref.at[:, MIX_RV:MIX_RV + RET_V_W]
    qd_ref = mix_ref.at[:, MIX_RQ:MIX_RQ + RET_QK_W]
    sk_ref = mix_ref.at[:, MIX_SK:MIX_SK + SWA_KV_W]
    sv_ref = mix_ref.at[:, MIX_SV:MIX_SV + SWA_KV_W]
    mk_ref = meta_ref.at[0, PAD:CHUNK, MIX_SK:MIX_SK + SWA_KV_W]
    mv_ref = meta_ref.at[0, PAD:CHUNK, MIX_SV:MIX_SV + SWA_KV_W]
    mrv_ref = meta_ref.at[0, :, MIX_RV:MIX_RV + RET_V_W]
    mkdt_ref = mkdt_ref.at[0]
    band_dist_ref = dist_ref.at[0:2]
    meta_dist_ref = dist_ref.at[2]
    rn_ref = gains_ref.at[0:1, :]
    g_ref = gains_ref.at[1:2, :]
    gr_ref = gate_ref.at[:, 0:D_MODEL]
    gs_ref = gate_ref.at[:, D_MODEL:2 * D_MODEL]
    wr_ref, ws_ref, wo_ref = wbo_ref.at[0], wbo_ref.at[1], wbo_ref.at[2]
    t = pl.program_id(0)
    block = jnp.minimum(t, pl.num_programs(0) - 2)
    block_in_batch = block % blocks_per_batch
    parity = t & 1
    last_slot = CHUNKS_PER_BLOCK - 1 + parity
    carried_slot = CHUNKS_PER_BLOCK - parity
    zeros_head = jnp.zeros((CHUNK, SWA_HEAD_DIM), BF16)

    @pl.when(t == 0)
    def _():
        ys_ref[...] = jnp.zeros_like(ys_ref)
        yr_ref[...] = jnp.zeros_like(yr_ref)

    @pl.when(block_in_batch == 0)
    def _():
        bd_ref[...] = jnp.zeros_like(bd_ref)
        vbuf_ref[...] = jnp.zeros_like(vbuf_ref)
        bdm_ref[...] = jnp.zeros_like(bdm_ref)
        vm_ref[...] = jnp.zeros_like(vm_ref)
        zm = jnp.zeros((N_META, SWA_HEAD_DIM), BF16)
        for g in range(SWA_KV_HEADS):
            cols = slice(g * SWA_HEAD_DIM, (g + 1) * SWA_HEAD_DIM)
            km = mk_ref[:, cols]
            bdm_ref[g, 0:N_META, :] = jnp.concatenate([km, zm], axis=1)
            bdm_ref[g, N_META:2 * N_META, :] = jnp.concatenate([zm, km], axis=1)
            vmeta = mv_ref[:, cols]
            vm_ref[g, 0:N_META, :] = jnp.concatenate([vmeta, zm], axis=1)
            vm_ref[g, N_META:2 * N_META, :] = jnp.concatenate([zm, vmeta], axis=1)
        for h in range(RET_HEADS):
            state_ref[h] = _dot(mkdt_ref[h * RET_QK_DIM:(h + 1) * RET_QK_DIM, :],
                                mrv_ref[:, h * RET_V_DIM:(h + 1) * RET_V_DIM])

    slabs = [slice(r, r + DENSE_ROWS) for r in range(0, BACK_ROWS, DENSE_ROWS)]
    for rs in slabs:
        merged = (_sigmoid(gr_ref[rs, :].astype(F32)) * _dot(yr_ref[rs, :], wr_ref[...])
                  + _sigmoid(gs_ref[rs, :].astype(F32)) * _dot(ys_ref[rs, :], ws_ref[...]))
        o_ref[rs, :] = x_ref[rs, :] + _dot(merged.astype(BF16), wo_ref[...])
        h1 = o_ref[rs, :]
        hn_ref[rs, :] = (h1 * lax.rsqrt(jnp.mean(h1 * h1, axis=-1, keepdims=True) + EPS)
                         * g_ref[...]).astype(BF16)

    ret_row = lax.broadcasted_iota(jnp.int32, (RET_CHUNK, RET_CHUNK), 0)
    ret_col = lax.broadcasted_iota(jnp.int32, (RET_CHUNK, RET_CHUNK), 1)
    ret_causal = ret_col <= ret_row

    def retention_head(sub, h):
        rows = slice(sub * RET_CHUNK, (sub + 1) * RET_CHUNK)
        vcols = slice(h * RET_V_DIM, (h + 1) * RET_V_DIM)
        qd = qd_ref[rows, h * RET_QK_DIM:(h + 1) * RET_QK_DIM]
        kdt = kdt_ref[h * RET_QK_DIM:(h + 1) * RET_QK_DIM, rows]
        v = rv_ref[rows, vcols]
        state = state_ref[h]
        s = jnp.where(ret_causal, _dot(qd, kdt), 0.0).astype(BF16)
        o = _dot(s, v) + _dot(qd, state.astype(BF16))
        state_ref[h] = RET_CHUNK_DECAY[h] * (state + _dot(kdt, v))
        ms = jnp.mean(o * o, axis=-1, keepdims=True)
        gate = rg_ref[rows, vcols].astype(F32)
        yr_ref[rows, vcols] = (o * lax.rsqrt(ms + EPS) * rn_ref[:, vcols]
                               * (gate * _sigmoid(gate))).astype(BF16)

    row = lax.broadcasted_iota(jnp.int32, (CHUNK, CHUNK), 0)
    col = lax.broadcasted_iota(jnp.int32, (CHUNK, CHUNK), 1)
    own = col <= row
    own2 = jnp.concatenate([own, own], axis=1)
    own4 = jnp.concatenate([own2, own2], axis=0)
    meta_seg = [col < N_META, (col >= N_META) & (col < 2 * N_META)]
    low_half = col < SWA_HEAD_DIM
    neg_inf = jnp.float32(-jnp.inf)

    first_chunk = (CHUNKS_PER_BLOCK * block_in_batch).astype(F32)
    chunks = []
    for c in range(CHUNKS_PER_BLOCK):
        slot = c if c < CHUNKS_PER_BLOCK - 1 else last_slot
        prev_slot = c - 1 if c > 0 else carried_slot
        bias_sel = jnp.minimum(block_in_batch, 1) if c == 0 else 1
        chunks.append((slice(c * CHUNK, (c + 1) * CHUNK), slot, prev_slot, bias_sel, first_chunk + float(c)))

    for rows, slot, _, _, _ in chunks:
        for g in range(SWA_KV_HEADS):
            kg = sk_ref[rows, g * SWA_HEAD_DIM:(g + 1) * SWA_HEAD_DIM]
            bd_ref[slot, g, 0:CHUNK, :] = jnp.concatenate([kg, zeros_head], axis=1)
            bd_ref[slot, g, CHUNK:2 * CHUNK, :] = jnp.concatenate([zeros_head, kg], axis=1)
            vg = sv_ref[rows, g * SWA_HEAD_DIM:(g + 1) * SWA_HEAD_DIM]
            vbuf_ref[slot, g] = jnp.concatenate([vg, vg], axis=1)

    def scores(chunk, g):
        rows, slot, prev_slot, bias_sel, chunk_idx = chunk
        q2 = jnp.concatenate(
            [sq_ref[rows, (2 * g + p) * LANES:(2 * g + p + 1) * LANES] for p in range(2)],
            axis=0)
        s_own = _dot_nt(q2, bd_ref[slot, g])
        s_prev = _dot_nt(q2, bd_ref[prev_slot, g])
        dist = band_dist_ref[bias_sel]
        bias = jnp.concatenate(
            [jnp.concatenate([ALIBI_SLOPES[g * SWA_GROUP + 2 * p + e] * dist for e in range(2)], axis=1)
             for p in range(2)], axis=0)
        s = jnp.where(own4, s_own, s_prev) - bias
        dist_meta = meta_dist_ref[...] + float(CHUNK) * chunk_idx
        bias_meta = jnp.concatenate(
            [jnp.where(meta_seg[0], ALIBI_SLOPES[g * SWA_GROUP + 2 * p],
                       ALIBI_SLOPES[g * SWA_GROUP + 2 * p + 1]) * dist_meta for p in range(2)], axis=0)
        s_meta = _dot_nt(q2, bdm_ref[g]) - bias_meta
        return s, s_meta

    def softmax(g, s, s_meta):
        lhs = []
        lhs_meta = []
        inv = []
        for p in range(2):
            sm_p = s_meta[p * CHUNK:(p + 1) * CHUNK]
            pm_pair = None
            for e in range(2):
                h = g * SWA_GROUP + 2 * p + e
                sink = sink_ref[h]
                sb = s[p * CHUNK:(p + 1) * CHUNK, e * CHUNK:(e + 1) * CHUNK]
                sm = jnp.where(meta_seg[e], sm_p, neg_inf)
                m = jnp.maximum(jnp.max(jnp.maximum(sb, sm), axis=-1, keepdims=True), sink)
                pb = jnp.exp(sb - m)
                pm = jnp.exp(sm - m)
                denom = jnp.sum(pb + pm, axis=-1, keepdims=True) + jnp.exp(sink - m)
                inv.append(1.0 / denom)
                lhs.append(jnp.concatenate(
                    [jnp.where(own, pb, 0.0).astype(BF16), jnp.where(own, 0.0, pb).astype(BF16)], axis=1))
                pm_pair = pm if pm_pair is None else pm_pair + pm
            lhs_meta.append(pm_pair.astype(BF16))
        return jnp.concatenate(lhs, axis=0), jnp.concatenate(lhs_meta, axis=0), inv

    def weighted_values(chunk, g, lhs, lhs_meta, inv):
        rows, slot, prev_slot, _, _ = chunk
        vv = jnp.concatenate([vbuf_ref[slot, g], vbuf_ref[prev_slot, g]], axis=0)
        acc = _dot(lhs, vv)
        acc_meta = _dot(lhs_meta, vm_ref[g])
        for p in range(2):
            first, second = 2 * p, 2 * p + 1
            pair = jnp.where(low_half, acc[first * CHUNK:(first + 1) * CHUNK],
                             acc[second * CHUNK:(second + 1) * CHUNK])
            scale = jnp.where(low_half, inv[first], inv[second])
            tile = (pair + acc_meta[p * CHUNK:(p + 1) * CHUNK]) * scale
            ys_ref[rows, (2 * g + p) * LANES:(2 * g + p + 1) * LANES] = tile.astype(BF16)

    tasks = [(chunk, g) for chunk in chunks for g in range(SWA_KV_HEADS)]
    ret_tasks = [(sub, h) for sub in range(RET_PER_BLOCK) for h in range(RET_HEADS)]
    assert len(tasks) == 2 * len(ret_tasks)

    def mlp_phase(k):
        c = k * FF_CHUNK
        for rs in slabs:
            u = jnp.maximum(_dot(hn_ref[rs, :], wu_ref[:, c:c + FF_CHUNK]), 0.0)
            o_ref[rs, :] += _dot((u * u).astype(BF16), wd_ref[c:c + FF_CHUNK, :])

    n_mlp_phases = D_FF // FF_CHUNK
    phase_at_task = {i * len(tasks) // n_mlp_phases: i for i in range(n_mlp_phases)}

    pending_scores = scores(*tasks[0])
    pending_probs = None
    for i, (chunk, g) in enumerate(tasks):
        current_scores = pending_scores
        if i + 1 < len(tasks):
            pending_scores = scores(*tasks[i + 1])
        if i in phase_at_task:
            mlp_phase(phase_at_task[i])
        probs = softmax(g, *current_scores)
        if i % 2 == 1:
            retention_head(*ret_tasks[i // 2])
        if pending_probs is not None:
            weighted_values(*tasks[i - 1], *pending_probs)
        pending_probs = probs
    weighted_values(*tasks[-1], *pending_probs)


def _back(x2d, mix2d, kdt, gates2d, meta_mix, meta_kdt, sinks, norm_gains, w_branch_out, w_up, w_down,
          blocks_per_batch):
    rows = x2d.shape[0]
    assert rows % BACK_ROWS == 0
    n_blocks = rows // BACK_ROWS
    _, band_dist, meta_dist = _swa_consts()

    cur = lambda t: jnp.minimum(t, n_blocks - 1)
    prev = lambda t: jnp.maximum(t - 1, 0)

    def resident(shape):
        zeros = (0,) * len(shape)
        return pl.BlockSpec(shape, lambda t, s: zeros, pipeline_mode=pl.Buffered(1))

    grid_spec = pltpu.PrefetchScalarGridSpec(
        num_scalar_prefetch=1,
        grid=(n_blocks + 1,),
        in_specs=[
            pl.BlockSpec((BACK_ROWS, MIX_W), lambda t, s: (cur(t), 0)),
            pl.BlockSpec((None, RET_QK_W, BACK_ROWS),
                         lambda t, s: (cur(t) // blocks_per_batch, 0, cur(t) % blocks_per_batch)),
            pl.BlockSpec((BACK_ROWS, RET_V_W), lambda t, s: (cur(t), GATE_RG // RET_V_W)),
            resident((1, CHUNK, MIX_W)),
            resident((1, RET_QK_W, CHUNK)),
            resident((3, CHUNK, CHUNK)),
            resident((2, D_MODEL)),
            pl.BlockSpec((BACK_ROWS, D_MODEL), lambda t, s: (prev(t), 0)),
            pl.BlockSpec((BACK_ROWS, 2 * D_MODEL), lambda t, s: (prev(t), 0)),
            resident((3, D_MODEL, D_MODEL)),
            resident((D_MODEL, D_FF)),
            resident((D_FF, D_MODEL)),
        ],
        out_specs=pl.BlockSpec((BACK_ROWS, D_MODEL), lambda t, s: (prev(t), 0)),
        scratch_shapes=[
            pltpu.VMEM((KV_SLOTS, SWA_KV_HEADS, PAIR_ROWS, LANES), BF16),
            pltpu.VMEM((KV_SLOTS, SWA_KV_HEADS, CHUNK, LANES), BF16),
            pltpu.VMEM((SWA_KV_HEADS, CHUNK, LANES), BF16),
            pltpu.VMEM((SWA_KV_HEADS, CHUNK, LANES), BF16),
            pltpu.VMEM((RET_HEADS, RET_QK_DIM, RET_V_DIM), F32),
            pltpu.VMEM((BACK_ROWS, SWA_Q_W), BF16),
            pltpu.VMEM((BACK_ROWS, RET_V_W), BF16),
            pltpu.VMEM((BACK_ROWS, D_MODEL), BF16),
        ],
    )
    return pl.pallas_call(
        functools.partial(_back_kernel, blocks_per_batch),
        out_shape=jax.ShapeDtypeStruct((rows, D_MODEL), F32),
        grid_spec=grid_spec,
        compiler_params=pltpu.CompilerParams(
            dimension_semantics=("arbitrary",),
            vmem_limit_bytes=V7X_VMEM_LIMIT_BYTES,
        ),
        name="back",
    )(sinks.astype(F32), mix2d, kdt, gates2d, meta_mix, meta_kdt,
      jnp.asarray(np.concatenate([band_dist, meta_dist[None]])), norm_gains,
      x2d, gates2d, w_branch_out, w_up, w_down)


def _permute_w_in(w):
    split = lambda off, width: w[:, off:off + width]
    return jnp.concatenate([
        split(OFF_SQ, SWA_Q_W), split(OFF_RV, RET_V_W), split(OFF_RQ, RET_QK_W), split(OFF_RK, RET_QK_W),
        split(OFF_SK, SWA_KV_W), split(OFF_SV, SWA_KV_W),
        split(OFF_GR, D_MODEL), split(OFF_GS, D_MODEL), split(OFF_RG, RET_V_W)], axis=1).astype(BF16)


def kernel(x, meta_tokens, mix_norm_g, w_in, ret_norm_g, q_norm_g, k_norm_g, sinks,
           w_ret_branch, w_swa_branch, w_out, mlp_norm_g, w_up, w_down):
    b, l, d = x.shape
    assert d == D_MODEL and l % BLOCK_ROWS == 0 and mix_norm_g.shape[0] == 1
    mix_g = mix_norm_g[0].reshape(1, d).astype(F32)
    w_in_b = _permute_w_in(w_in[0])
    qg2 = jnp.tile(q_norm_g[0].astype(F32) * (SWA_HEAD_DIM ** -0.5), 2).reshape(1, LANES)
    kg2 = jnp.tile(k_norm_g[0].astype(F32), 2).reshape(1, LANES)

    mix, kdt, gates = _inproj(x, mix_g, qg2, kg2, jnp.asarray(_decay_tables(BLOCK_ROWS)), w_in_b,
                              BLOCK_ROWS)
    meta_chunk = jnp.pad(meta_tokens.astype(F32), ((PAD, 0), (0, 0)))[None]
    meta_mix, meta_kdt, _ = _inproj(meta_chunk, mix_g, qg2, kg2, jnp.asarray(_meta_decay_tables()),
                                    w_in_b, CHUNK)

    norm_gains = jnp.stack([ret_norm_g[0].reshape(RET_V_W), mlp_norm_g[0]]).astype(F32)
    w_branch_out = jnp.stack([w_ret_branch[0], w_swa_branch[0], w_out[0]]).astype(BF16)
    out = _back(
        x.reshape(b * l, d), mix.reshape(b * l, MIX_W), kdt, gates.reshape(b * l, GATE_W),
        meta_mix, meta_kdt, sinks[0], norm_gains, w_branch_out,
        w_up[0].astype(BF16), w_down[0].astype(BF16), l // BACK_ROWS)
    return out.reshape(b, l, d)
```

```python
import functools

import numpy as np
import jax
import jax.numpy as jnp
from jax import lax
from jax.experimental import pallas as pl
from jax.experimental.pallas import tpu as pltpu

D_MODEL = 1024
N_META = 16
CHUNK = 128
PAD = CHUNK - N_META
RET_HEADS = 4
RET_QK_DIM = 128
RET_V_DIM = 256
SWA_Q_HEADS = 16
SWA_KV_HEADS = 4
SWA_GROUP = SWA_Q_HEADS // SWA_KV_HEADS
SWA_HEAD_DIM = 64
D_FF = 4 * D_MODEL
EPS = 1e-6
RET_QK_W = RET_HEADS * RET_QK_DIM
RET_V_W = RET_HEADS * RET_V_DIM
SWA_Q_W = SWA_Q_HEADS * SWA_HEAD_DIM
SWA_KV_W = SWA_KV_HEADS * SWA_HEAD_DIM
D_IN = 2 * RET_QK_W + 2 * RET_V_W + SWA_Q_W + 2 * SWA_KV_W + 2 * D_MODEL

OFF_RQ = 0
OFF_RK = OFF_RQ + RET_QK_W
OFF_RV = OFF_RK + RET_QK_W
OFF_RG = OFF_RV + RET_V_W
OFF_SQ = OFF_RG + RET_V_W
OFF_SK = OFF_SQ + SWA_Q_W
OFF_SV = OFF_SK + SWA_KV_W
OFF_GR = OFF_SV + SWA_KV_W
OFF_GS = OFF_GR + D_MODEL

MIX_SQ = 0
MIX_RV = MIX_SQ + SWA_Q_W
MIX_RQ = MIX_RV + RET_V_W
MIX_SK = MIX_RQ + RET_QK_W
MIX_SV = MIX_SK + SWA_KV_W
MIX_W = MIX_SV + SWA_KV_W
GATE_RG = 2 * D_MODEL
GATE_W = GATE_RG + RET_V_W

V7X_VMEM_LIMIT_BYTES = 60 * 1024 * 1024
LANES = 128
assert LANES == 2 * SWA_HEAD_DIM and CHUNK == LANES and RET_V_W == D_MODEL

F32 = jnp.float32
BF16 = jnp.bfloat16

BLOCK_ROWS = 512
RET_CHUNK = 256
assert BLOCK_ROWS % RET_CHUNK == 0 and BLOCK_ROWS % CHUNK == 0


def _sigmoid(x):
    return 1.0 / (1.0 + jnp.exp(-x))


def _dot(a, b):
    return jnp.dot(a, b, preferred_element_type=F32)


def _dot_nt(a, b):
    return lax.dot_general(a, b, (((1,), (1,)), ((), ())), preferred_element_type=F32)


def _retention_log_gamma():
    return np.log1p(-(2.0 ** (-5.0 - np.arange(RET_HEADS, dtype=np.float64))))


IN_N_CHUNK = 512


def _head_pair_rms(y, g2):
    rows, width = y.shape
    low = lax.broadcasted_iota(jnp.int32, (rows, LANES), 1) < SWA_HEAD_DIM
    out = []
    for c in range(0, width, LANES):
        yb = y[:, c:c + LANES]
        y2 = yb * yb
        ms_lo = jnp.sum(jnp.where(low, y2, 0.0), axis=-1, keepdims=True) * (1.0 / SWA_HEAD_DIM)
        ms_hi = jnp.sum(jnp.where(low, 0.0, y2), axis=-1, keepdims=True) * (1.0 / SWA_HEAD_DIM)
        scale = jnp.where(low, lax.rsqrt(ms_lo + EPS), lax.rsqrt(ms_hi + EPS))
        out.append(yb * scale * g2)
    return jnp.concatenate(out, axis=1) if len(out) > 1 else out[0]


def _inproj_kernel(x_ref, g_ref, qg_ref, kg_ref, dec_ref, w_ref, *rest):
    mix_ref, kdt_ref, gate_ref = rest[-3:] if len(rest) == 3 else rest[5:8]
    x = x_ref[...]
    ms = jnp.mean(x * x, axis=-1, keepdims=True)
    hn = (x * lax.rsqrt(ms + EPS) * g_ref[...]).astype(BF16)

    def proj(lo, width):
        return _dot(hn, w_ref[:, lo:lo + width])

    for c in range(0, SWA_Q_W, IN_N_CHUNK):
        mix_ref[:, MIX_SQ + c:MIX_SQ + c + IN_N_CHUNK] = _head_pair_rms(
            proj(OFF_SQ + c, IN_N_CHUNK), qg_ref[...]).astype(BF16)
    for c in range(0, RET_V_W, IN_N_CHUNK):
        mix_ref[:, MIX_RV + c:MIX_RV + c + IN_N_CHUNK] = proj(OFF_RV + c, IN_N_CHUNK).astype(BF16)
    mix_ref[:, MIX_RQ:MIX_RQ + RET_QK_W] = (proj(OFF_RQ, RET_QK_W) * dec_ref[:, 0:RET_QK_W]).astype(BF16)
    kd = proj(OFF_RK, RET_QK_W) * dec_ref[:, RET_QK_W:2 * RET_QK_W]
    kdt_ref[...] = kd.T.astype(BF16)
    mix_ref[:, MIX_SK:MIX_SK + SWA_KV_W] = _head_pair_rms(proj(OFF_SK, SWA_KV_W), kg_ref[...]).astype(BF16)
    mix_ref[:, MIX_SV:MIX_SV + SWA_KV_W] = proj(OFF_SV, SWA_KV_W).astype(BF16)
    for c in range(0, 2 * D_MODEL, IN_N_CHUNK):
        gate_ref[:, c:c + IN_N_CHUNK] = proj(OFF_GR + c, IN_N_CHUNK).astype(BF16)
    for c in range(0, RET_V_W, IN_N_CHUNK):
        gate_ref[:, GATE_RG + c:GATE_RG + c + IN_N_CHUNK] = proj(OFF_RG + c, IN_N_CHUNK).astype(BF16)

    if len(rest) > 3:
        wr_ref, ws_ref, wo_ref, wu_ref, wd_ref = rest[:5]
        wbo_out_ref, wu_out_ref, wd_out_ref = rest[8:]
        for k, src in enumerate((wr_ref, ws_ref, wo_ref)):
            wbo_out_ref[k] = src[...].astype(BF16)
        wu_out_ref[...] = wu_ref[...].astype(BF16)
        wd_out_ref[...] = wd_ref[...].astype(BF16)


def _inproj(x3d, g, qg2, kg2, dec, w_bf16, block_rows, cast_weights=None):
    b, seq, _ = x3d.shape
    assert seq % block_rows == 0 and dec.shape == (block_rows, 2 * RET_QK_W)
    steps = seq // block_rows
    n_steps = b * steps
    const = lambda shape: pl.BlockSpec(shape, lambda i: (0, 0))
    out_shape = [jax.ShapeDtypeStruct((b, seq, MIX_W), BF16),
                 jax.ShapeDtypeStruct((b, RET_QK_W, seq), BF16),
                 jax.ShapeDtypeStruct((b, seq, GATE_W), BF16)]
    out_specs = [pl.BlockSpec((None, block_rows, MIX_W), lambda i: (i // steps, i % steps, 0)),
                 pl.BlockSpec((None, RET_QK_W, block_rows), lambda i: (i // steps, 0, i % steps)),
                 pl.BlockSpec((None, block_rows, GATE_W), lambda i: (i // steps, i % steps, 0))]
    extra_in, extra_specs = [], []
    if cast_weights is not None:
        band = lambda w: w.shape[1] // n_steps
        for w in cast_weights:
            assert w.shape[1] % n_steps == 0 and band(w) % 16 == 0
            extra_in.append(w)
            extra_specs.append(pl.BlockSpec((None, band(w), w.shape[2]), lambda i: (0, i, 0)))
        w_r, _, _, w_u, w_d = cast_weights
        out_shape += [jax.ShapeDtypeStruct((3,) + w_r.shape[1:], BF16),
                      jax.ShapeDtypeStruct(w_u.shape[1:], BF16),
                      jax.ShapeDtypeStruct(w_d.shape[1:], BF16)]
        out_specs += [pl.BlockSpec((3, band(w_r), w_r.shape[2]), lambda i: (0, i, 0)),
                      pl.BlockSpec((band(w_u), w_u.shape[2]), lambda i: (i, 0)),
                      pl.BlockSpec((band(w_d), w_d.shape[2]), lambda i: (i, 0))]
    return pl.pallas_call(
        _inproj_kernel,
        out_shape=tuple(out_shape),
        grid=(n_steps,),
        in_specs=[
            pl.BlockSpec((None, block_rows, D_MODEL), lambda i: (i // steps, i % steps, 0)),
            const((1, D_MODEL)),
            const((1, LANES)),
            const((1, LANES)),
            const((block_rows, 2 * RET_QK_W)),
            pl.BlockSpec((D_MODEL, D_IN), lambda i: (0, 0), pipeline_mode=pl.Buffered(1)),
        ] + extra_specs,
        out_specs=tuple(out_specs),
        compiler_params=pltpu.CompilerParams(
            dimension_semantics=("arbitrary",),
            vmem_limit_bytes=V7X_VMEM_LIMIT_BYTES,
        ),
        name="inproj",
    )(x3d, g, qg2, kg2, dec, w_bf16, *extra_in)


def _decay_tables(block_rows):
    lg = _retention_log_gamma()
    pos = (np.arange(block_rows) % RET_CHUNK).astype(np.float64) + 1.0
    qd = np.exp(lg[None, :] * pos[:, None])
    kd = np.exp(-lg[None, :] * pos[:, None]) * (RET_QK_DIM ** -0.5)
    rep = lambda t: np.repeat(t, RET_QK_DIM, axis=1)
    return np.concatenate([rep(qd), rep(kd)], axis=1).astype(np.float32)


def _meta_decay_tables():
    lg = _retention_log_gamma()
    pos = np.arange(CHUNK, dtype=np.float64)
    kd = np.exp(lg[None, :] * (CHUNK - 1.0 - pos[:, None])) * (RET_QK_DIM ** -0.5)
    rep = lambda t: np.repeat(t, RET_QK_DIM, axis=1)
    return np.concatenate([np.ones((CHUNK, RET_QK_W)), rep(kd)], axis=1).astype(np.float32)


RET_CHUNK_DECAY = [float(c) for c in np.exp(_retention_log_gamma() * RET_CHUNK)]
PAIR_ROWS = 2 * CHUNK
BACK_ROWS = 256
DENSE_ROWS = 256
assert BACK_ROWS % RET_CHUNK == 0 and BLOCK_ROWS % BACK_ROWS == 0 and BACK_ROWS % DENSE_ROWS == 0
CHUNKS_PER_BLOCK = BACK_ROWS // CHUNK
RET_PER_BLOCK = BACK_ROWS // RET_CHUNK
KV_SLOTS = CHUNKS_PER_BLOCK + 1
FF_CHUNK = 1024


def _swa_consts():
    slopes = 2.0 ** (-8.0 * np.arange(1, SWA_Q_HEADS + 1, dtype=np.float64) / SWA_Q_HEADS)
    slopes = slopes.astype(np.float32)
    i = np.arange(CHUNK)[:, None]
    j = np.arange(CHUNK)[None, :]
    band_dist = np.where(j <= i, i - j, i - j + CHUNK).astype(np.float32)
    meta_dist = np.where(j < 2 * N_META, i - (j % N_META) + N_META, np.inf).astype(np.float32)
    band_dist = np.stack([np.where(j <= i, band_dist, np.inf), band_dist]).astype(np.float32)
    return [float(s) for s in slopes], band_dist, meta_dist


ALIBI_SLOPES = _swa_consts()[0]


def _back_kernel(blocks_per_batch,
                 sink_ref,
                 mix_ref, kdt_ref, rg_ref, meta_ref, mkdt_ref, dist_ref, gains_ref,
                 x_ref, gate_ref, wbo_ref, wu_ref, wd_ref,
                 o_ref,
                 bd_ref, vbuf_ref, bdm_ref, vm_ref, state_ref, ys_ref, yr_ref, hn_ref):
    sq_ref = mix_ref.at[:, MIX_SQ:MIX_SQ + SWA_Q_W]
    rv_ref = mix_ref.at[:, MIX_RV:MIX_RV + RET_V_W]
    qd_ref = mix_ref.at[:, MIX_RQ:MIX_RQ + RET_QK_W]
    sk_ref = mix_ref.at[:, MIX_SK:MIX_SK + SWA_KV_W]
    sv_ref = mix_ref.at[:, MIX_SV:MIX_SV + SWA_KV_W]
    mk_ref = meta_ref.at[0, PAD:CHUNK, MIX_SK:MIX_SK + SWA_KV_W]
    mv_ref = meta_ref.at[0, PAD:CHUNK, MIX_SV:MIX_SV + SWA_KV_W]
    mrv_ref = meta_ref.at[0, :, MIX_RV:MIX_RV + RET_V_W]
    mkdt_ref = mkdt_ref.at[0]
    band_dist_ref = dist_ref.at[0:2]
    meta_dist_ref = dist_ref.at[2]
    rn_ref = gains_ref.at[0:1, :]
    g_ref = gains_ref.at[1:2, :]
    gr_ref = gate_ref.at[:, 0:D_MODEL]
    gs_ref = gate_ref.at[:, D_MODEL:2 * D_MODEL]
    wr_ref, ws_ref, wo_ref = wbo_ref.at[0], wbo_ref.at[1], wbo_ref.at[2]
    t = pl.program_id(0)
    block = jnp.minimum(t, pl.num_programs(0) - 2)
    block_in_batch = block % blocks_per_batch
    parity = t & 1
    last_slot = CHUNKS_PER_BLOCK - 1 + parity
    carried_slot = CHUNKS_PER_BLOCK - parity
    zeros_head = jnp.zeros((CHUNK, SWA_HEAD_DIM), BF16)

    @pl.when(t == 0)
    def _():
        ys_ref[...] = jnp.zeros_like(ys_ref)
        yr_ref[...] = jnp.zeros_like(yr_ref)

    @pl.when(block_in_batch == 0)
    def _():
        bd_ref[...] = jnp.zeros_like(bd_ref)
        vbuf_ref[...] = jnp.zeros_like(vbuf_ref)
        bdm_ref[...] = jnp.zeros_like(bdm_ref)
        vm_ref[...] = jnp.zeros_like(vm_ref)
        zm = jnp.zeros((N_META, SWA_HEAD_DIM), BF16)
        for g in range(SWA_KV_HEADS):
            cols = slice(g * SWA_HEAD_DIM, (g + 1) * SWA_HEAD_DIM)
            km = mk_ref[:, cols]
            bdm_ref[g, 0:N_META, :] = jnp.concatenate([km, zm], axis=1)
            bdm_ref[g, N_META:2 * N_META, :] = jnp.concatenate([zm, km], axis=1)
            vmeta = mv_ref[:, cols]
            vm_ref[g, 0:N_META, :] = jnp.concatenate([vmeta, zm], axis=1)
            vm_ref[g, N_META:2 * N_META, :] = jnp.concatenate([zm, vmeta], axis=1)
        for h in range(RET_HEADS):
            state_ref[h] = _dot(mkdt_ref[h * RET_QK_DIM:(h + 1) * RET_QK_DIM, :],
                                mrv_ref[:, h * RET_V_DIM:(h + 1) * RET_V_DIM])

    slabs = [slice(r, r + DENSE_ROWS) for r in range(0, BACK_ROWS, DENSE_ROWS)]
    for rs in slabs:
        merged = (_sigmoid(gr_ref[rs, :].astype(F32)) * _dot(yr_ref[rs, :], wr_ref[...])
                  + _sigmoid(gs_ref[rs, :].astype(F32)) * _dot(ys_ref[rs, :], ws_ref[...]))
        o_ref[rs, :] = x_ref[rs, :] + _dot(merged.astype(BF16), wo_ref[...])
        h1 = o_ref[rs, :]
        hn_ref[rs, :] = (h1 * lax.rsqrt(jnp.mean(h1 * h1, axis=-1, keepdims=True) + EPS)
                         * g_ref[...]).astype(BF16)

    ret_row = lax.broadcasted_iota(jnp.int32, (RET_CHUNK, RET_CHUNK), 0)
    ret_col = lax.broadcasted_iota(jnp.int32, (RET_CHUNK, RET_CHUNK), 1)
    ret_causal = ret_col <= ret_row

    def retention_head(sub, h):
        rows = slice(sub * RET_CHUNK, (sub + 1) * RET_CHUNK)
        vcols = slice(h * RET_V_DIM, (h + 1) * RET_V_DIM)
        qd = qd_ref[rows, h * RET_QK_DIM:(h + 1) * RET_QK_DIM]
        kdt = kdt_ref[h * RET_QK_DIM:(h + 1) * RET_QK_DIM, rows]
        v = rv_ref[rows, vcols]
        state = state_ref[h]
        s = jnp.where(ret_causal, _dot(qd, kdt), 0.0).astype(BF16)
        o = _dot(s, v) + _dot(qd, state.astype(BF16))
        state_ref[h] = RET_CHUNK_DECAY[h] * (state + _dot(kdt, v))
        ms = jnp.mean(o * o, axis=-1, keepdims=True)
        gate = rg_ref[rows, vcols].astype(F32)
        yr_ref[rows, vcols] = (o * lax.rsqrt(ms + EPS) * rn_ref[:, vcols]
                               * (gate * _sigmoid(gate))).astype(BF16)

    row = lax.broadcasted_iota(jnp.int32, (CHUNK, CHUNK), 0)
    col = lax.broadcasted_iota(jnp.int32, (CHUNK, CHUNK), 1)
    own = col <= row
    own2 = jnp.concatenate([own, own], axis=1)
    own4 = jnp.concatenate([own2, own2], axis=0)
    meta_seg = [col < N_META, (col >= N_META) & (col < 2 * N_META)]
    low_half = col < SWA_HEAD_DIM
    neg_inf = jnp.float32(-jnp.inf)

    first_chunk = (CHUNKS_PER_BLOCK * block_in_batch).astype(F32)
    chunks = []
    for c in range(CHUNKS_PER_BLOCK):
        slot = c if c < CHUNKS_PER_BLOCK - 1 else last_slot
        prev_slot = c - 1 if c > 0 else carried_slot
        bias_sel = jnp.minimum(block_in_batch, 1) if c == 0 else 1
        chunks.append((slice(c * CHUNK, (c + 1) * CHUNK), slot, prev_slot, bias_sel, first_chunk + float(c)))

    for rows, slot, _, _, _ in chunks:
        for g in range(SWA_KV_HEADS):
            kg = sk_ref[rows, g * SWA_HEAD_DIM:(g + 1) * SWA_HEAD_DIM]
            bd_ref[slot, g, 0:CHUNK, :] = jnp.concatenate([kg, zeros_head], axis=1)
            bd_ref[slot, g, CHUNK:2 * CHUNK, :] = jnp.concatenate([zeros_head, kg], axis=1)
            vg = sv_ref[rows, g * SWA_HEAD_DIM:(g + 1) * SWA_HEAD_DIM]
            vbuf_ref[slot, g] = jnp.concatenate([vg, vg], axis=1)

    def scores(chunk, g):
        rows, slot, prev_slot, bias_sel, chunk_idx = chunk
        q2 = jnp.concatenate(
            [sq_ref[rows, (2 * g + p) * LANES:(2 * g + p + 1) * LANES] for p in range(2)],
            axis=0)
        s_own = _dot_nt(q2, bd_ref[slot, g])
        s_prev = _dot_nt(q2, bd_ref[prev_slot, g])
        dist = band_dist_ref[bias_sel]
        bias = jnp.concatenate(
            [jnp.concatenate([ALIBI_SLOPES[g * SWA_GROUP + 2 * p + e] * dist for e in range(2)], axis=1)
             for p in range(2)], axis=0)
        s = jnp.where(own4, s_own, s_prev) - bias
        dist_meta = meta_dist_ref[...] + float(CHUNK) * chunk_idx
        bias_meta = jnp.concatenate(
            [jnp.where(meta_seg[0], ALIBI_SLOPES[g * SWA_GROUP + 2 * p],
                       ALIBI_SLOPES[g * SWA_GROUP + 2 * p + 1]) * dist_meta for p in range(2)], axis=0)
        s_meta = _dot_nt(q2, bdm_ref[g]) - bias_meta
        return s, s_meta

    def softmax(g, s, s_meta):
        lhs = []
        lhs_meta = []
        inv = []
        for p in range(2):
            sm_p = s_meta[p * CHUNK:(p + 1) * CHUNK]
            pm_pair = None
            for e in range(2):
                h = g * SWA_GROUP + 2 * p + e
                sink = sink_ref[h]
                sb = s[p * CHUNK:(p + 1) * CHUNK, e * CHUNK:(e + 1) * CHUNK]
                sm = jnp.where(meta_seg[e], sm_p, neg_inf)
                m = jnp.maximum(jnp.max(jnp.maximum(sb, sm), axis=-1, keepdims=True), sink)
                pb = jnp.exp(sb - m)
                pm = jnp.exp(sm - m)
                denom = jnp.sum(pb + pm, axis=-1, keepdims=True) + jnp.exp(sink - m)
                inv.append(1.0 / denom)
                lhs.append(jnp.concatenate(
                    [jnp.where(own, pb, 0.0).astype(BF16), jnp.where(own, 0.0, pb).astype(BF16)], axis=1))
                pm_pair = pm if pm_pair is None else pm_pair + pm
            lhs_meta.append(pm_pair.astype(BF16))
        return jnp.concatenate(lhs, axis=0), jnp.concatenate(lhs_meta, axis=0), inv

    def weighted_values(chunk, g, lhs, lhs_meta, inv):
        rows, slot, prev_slot, _, _ = chunk
        vv = jnp.concatenate([vbuf_ref[slot, g], vbuf_ref[prev_slot, g]], axis=0)
        acc = _dot(lhs, vv)
        acc_meta = _dot(lhs_meta, vm_ref[g])
        for p in range(2):
            first, second = 2 * p, 2 * p + 1
            pair = jnp.where(low_half, acc[first * CHUNK:(first + 1) * CHUNK],
                             acc[second * CHUNK:(second + 1) * CHUNK])
            scale = jnp.where(low_half, inv[first], inv[second])
            tile = (pair + acc_meta[p * CHUNK:(p + 1) * CHUNK]) * scale
            ys_ref[rows, (2 * g + p) * LANES:(2 * g + p + 1) * LANES] = tile.astype(BF16)

    tasks = [(chunk, g) for chunk in chunks for g in range(SWA_KV_HEADS)]
    ret_tasks = [(sub, h) for sub in range(RET_PER_BLOCK) for h in range(RET_HEADS)]
    assert len(tasks) == 2 * len(ret_tasks)

    def mlp_phase(k):
        c = k * FF_CHUNK
        for rs in slabs:
            u = jnp.maximum(_dot(hn_ref[rs, :], wu_ref[:, c:c + FF_CHUNK]), 0.0)
            o_ref[rs, :] += _dot((u * u).astype(BF16), wd_ref[c:c + FF_CHUNK, :])

    n_mlp_phases = D_FF // FF_CHUNK
    phase_at_task = {i * len(tasks) // n_mlp_phases: i for i in range(n_mlp_phases)}

    pending_scores = scores(*tasks[0])
    pending_probs = None
    for i, (chunk, g) in enumerate(tasks):
        current_scores = pending_scores
        if i + 1 < len(tasks):
            pending_scores = scores(*tasks[i + 1])
        if i in phase_at_task:
            mlp_phase(phase_at_task[i])
        probs = softmax(g, *current_scores)
        if i % 2 == 1:
            retention_head(*ret_tasks[i // 2])
        if pending_probs is not None:
            weighted_values(*tasks[i - 1], *pending_probs)
        pending_probs = probs
    weighted_values(*tasks[-1], *pending_probs)


def _back(x2d, mix2d, kdt, gates2d, meta_mix, meta_kdt, sinks, norm_gains, w_branch_out, w_up, w_down,
          blocks_per_batch):
    rows = x2d.shape[0]
    assert rows % BACK_ROWS == 0
    n_blocks = rows // BACK_ROWS
    _, band_dist, meta_dist = _swa_consts()

    cur = lambda t: jnp.minimum(t, n_blocks - 1)
    prev = lambda t: jnp.maximum(t - 1, 0)

    def resident(shape):
        zeros = (0,) * len(shape)
        return pl.BlockSpec(shape, lambda t, s: zeros, pipeline_mode=pl.Buffered(1))

    grid_spec = pltpu.PrefetchScalarGridSpec(
        num_scalar_prefetch=1,
        grid=(n_blocks + 1,),
        in_specs=[
            pl.BlockSpec((BACK_ROWS, MIX_W), lambda t, s: (cur(t), 0)),
            pl.BlockSpec((None, RET_QK_W, BACK_ROWS),
                         lambda t, s: (cur(t) // blocks_per_batch, 0, cur(t) % blocks_per_batch)),
            pl.BlockSpec((BACK_ROWS, RET_V_W), lambda t, s: (cur(t), GATE_RG // RET_V_W)),
            resident((1, CHUNK, MIX_W)),
            resident((1, RET_QK_W, CHUNK)),
            resident((3, CHUNK, CHUNK)),
            resident((2, D_MODEL)),
            pl.BlockSpec((BACK_ROWS, D_MODEL), lambda t, s: (prev(t), 0)),
            pl.BlockSpec((BACK_ROWS, 2 * D_MODEL), lambda t, s: (prev(t), 0)),
            resident((3, D_MODEL, D_MODEL)),
            resident((D_MODEL, D_FF)),
            resident((D_FF, D_MODEL)),
        ],
        out_specs=pl.BlockSpec((BACK_ROWS, D_MODEL), lambda t, s: (prev(t), 0)),
        scratch_shapes=[
            pltpu.VMEM((KV_SLOTS, SWA_KV_HEADS, PAIR_ROWS, LANES), BF16),
            pltpu.VMEM((KV_SLOTS, SWA_KV_HEADS, CHUNK, LANES), BF16),
            pltpu.VMEM((SWA_KV_HEADS, CHUNK, LANES), BF16),
            pltpu.VMEM((SWA_KV_HEADS, CHUNK, LANES), BF16),
            pltpu.VMEM((RET_HEADS, RET_QK_DIM, RET_V_DIM), F32),
            pltpu.VMEM((BACK_ROWS, SWA_Q_W), BF16),
            pltpu.VMEM((BACK_ROWS, RET_V_W), BF16),
            pltpu.VMEM((BACK_ROWS, D_MODEL), BF16),
        ],
    )
    return pl.pallas_call(
        functools.partial(_back_kernel, blocks_per_batch),
        out_shape=jax.ShapeDtypeStruct((rows, D_MODEL), F32),
        grid_spec=grid_spec,
        compiler_params=pltpu.CompilerParams(
            dimension_semantics=("arbitrary",),
            vmem_limit_bytes=V7X_VMEM_LIMIT_BYTES,
        ),
        name="back",
    )(sinks.astype(F32), mix2d, kdt, gates2d, meta_mix, meta_kdt,
      jnp.asarray(np.concatenate([band_dist, meta_dist[None]])), norm_gains,
      x2d, gates2d, w_branch_out, w_up, w_down)


def kernel(x, meta_tokens, mix_norm_g, w_in, ret_norm_g, q_norm_g, k_norm_g, sinks,
           w_ret_branch, w_swa_branch, w_out, mlp_norm_g, w_up, w_down):
    b, l, d = x.shape
    assert d == D_MODEL and l % BLOCK_ROWS == 0 and mix_norm_g.shape[0] == 1
    mix_g = mix_norm_g[0].reshape(1, d).astype(F32)
    w_in_b = w_in[0].astype(BF16)
    qg2 = jnp.tile(q_norm_g[0].astype(F32) * (SWA_HEAD_DIM ** -0.5), 2).reshape(1, LANES)
    kg2 = jnp.tile(k_norm_g[0].astype(F32), 2).reshape(1, LANES)

    mix, kdt, gates, w_branch_out, w_up_b, w_down_b = _inproj(
        x, mix_g, qg2, kg2, jnp.asarray(_decay_tables(BLOCK_ROWS)), w_in_b, BLOCK_ROWS,
        cast_weights=(w_ret_branch, w_swa_branch, w_out, w_up, w_down))
    meta_chunk = jnp.pad(meta_tokens.astype(F32), ((PAD, 0), (0, 0)))[None]
    meta_mix, meta_kdt, _ = _inproj(meta_chunk, mix_g, qg2, kg2, jnp.asarray(_meta_decay_tables()),
                                    w_in_b, CHUNK)

    norm_gains = jnp.stack([ret_norm_g[0].reshape(RET_V_W), mlp_norm_g[0]]).astype(F32)
    out = _back(
        x.reshape(b * l, d), mix.reshape(b * l, MIX_W), kdt, gates.reshape(b * l, GATE_W),
        meta_mix, meta_kdt, sinks[0], norm_gains, w_branch_out, w_up_b, w_down_b, l // BACK_ROWS)
    return out.reshape(b, l, d)
```

```python
import functools

import numpy as np
import jax
import jax.numpy as jnp
from jax import lax
from jax.experimental import pallas as pl
from jax.experimental.pallas import tpu as pltpu

D_MODEL = 1024
N_META = 16
CHUNK = 128
PAD = CHUNK - N_META
RET_HEADS = 4
RET_QK_DIM = 128
RET_V_DIM = 256
SWA_Q_HEADS = 16
SWA_KV_HEADS = 4
SWA_GROUP = SWA_Q_HEADS // SWA_KV_HEADS
SWA_HEAD_DIM = 64
D_FF = 4 * D_MODEL
EPS = 1e-6
RET_QK_W = RET_HEADS * RET_QK_DIM
RET_V_W = RET_HEADS * RET_V_DIM
SWA_Q_W = SWA_Q_HEADS * SWA_HEAD_DIM
SWA_KV_W = SWA_KV_HEADS * SWA_HEAD_DIM
D_IN = 2 * RET_QK_W + 2 * RET_V_W + SWA_Q_W + 2 * SWA_KV_W + 2 * D_MODEL

OFF_RQ = 0
OFF_RK = OFF_RQ + RET_QK_W
OFF_RV = OFF_RK + RET_QK_W
OFF_RG = OFF_RV + RET_V_W
OFF_SQ = OFF_RG + RET_V_W
OFF_SK = OFF_SQ + SWA_Q_W
OFF_SV = OFF_SK + SWA_KV_W
OFF_GR = OFF_SV + SWA_KV_W
OFF_GS = OFF_GR + D_MODEL

MIX_SQ = 0
MIX_RV = MIX_SQ + SWA_Q_W
MIX_RQ = MIX_RV + RET_V_W
MIX_SK = MIX_RQ + RET_QK_W
MIX_SV = MIX_SK + SWA_KV_W
MIX_W = MIX_SV + SWA_KV_W
GATE_RG = 2 * D_MODEL
GATE_W = GATE_RG + RET_V_W

V7X_VMEM_LIMIT_BYTES = 60 * 1024 * 1024
LANES = 128
assert LANES == 2 * SWA_HEAD_DIM and CHUNK == LANES and RET_V_W == D_MODEL

F32 = jnp.float32
BF16 = jnp.bfloat16

BLOCK_ROWS = 512
RET_CHUNK = 256
assert BLOCK_ROWS % RET_CHUNK == 0 and BLOCK_ROWS % CHUNK == 0


def _sigmoid(x):
    return 1.0 / (1.0 + jnp.exp(-x))


def _dot(a, b):
    return jnp.dot(a, b, preferred_element_type=F32)


def _dot_nt(a, b):
    return lax.dot_general(a, b, (((1,), (1,)), ((), ())), preferred_element_type=F32)


def _retention_log_gamma():
    return np.log1p(-(2.0 ** (-5.0 - np.arange(RET_HEADS, dtype=np.float64))))


IN_N_CHUNK = 512


def _head_pair_rms(y, g2):
    rows, width = y.shape
    low = lax.broadcasted_iota(jnp.int32, (rows, LANES), 1) < SWA_HEAD_DIM
    out = []
    for c in range(0, width, LANES):
        yb = y[:, c:c + LANES]
        y2 = yb * yb
        ms_lo = jnp.sum(jnp.where(low, y2, 0.0), axis=-1, keepdims=True) * (1.0 / SWA_HEAD_DIM)
        ms_hi = jnp.sum(jnp.where(low, 0.0, y2), axis=-1, keepdims=True) * (1.0 / SWA_HEAD_DIM)
        scale = jnp.where(low, lax.rsqrt(ms_lo + EPS), lax.rsqrt(ms_hi + EPS))
        out.append(yb * scale * g2)
    return jnp.concatenate(out, axis=1) if len(out) > 1 else out[0]


def _inproj_kernel(x_ref, g_ref, qg_ref, kg_ref, dec_ref, w_ref, mix_ref, kdt_ref, gate_ref):
    x = x_ref[...]
    ms = jnp.mean(x * x, axis=-1, keepdims=True)
    hn = (x * lax.rsqrt(ms + EPS) * g_ref[...]).astype(BF16)

    def proj(lo, width):
        return _dot(hn, w_ref[:, lo:lo + width])

    for c in range(0, SWA_Q_W, IN_N_CHUNK):
        mix_ref[:, MIX_SQ + c:MIX_SQ + c + IN_N_CHUNK] = _head_pair_rms(
            proj(OFF_SQ + c, IN_N_CHUNK), qg_ref[...]).astype(BF16)
    for c in range(0, RET_V_W, IN_N_CHUNK):
        mix_ref[:, MIX_RV + c:MIX_RV + c + IN_N_CHUNK] = proj(OFF_RV + c, IN_N_CHUNK).astype(BF16)
    mix_ref[:, MIX_RQ:MIX_RQ + RET_QK_W] = (proj(OFF_RQ, RET_QK_W) * dec_ref[:, 0:RET_QK_W]).astype(BF16)
    kd = proj(OFF_RK, RET_QK_W) * dec_ref[:, RET_QK_W:2 * RET_QK_W]
    kdt_ref[...] = kd.T.astype(BF16)
    mix_ref[:, MIX_SK:MIX_SK + SWA_KV_W] = _head_pair_rms(proj(OFF_SK, SWA_KV_W), kg_ref[...]).astype(BF16)
    mix_ref[:, MIX_SV:MIX_SV + SWA_KV_W] = proj(OFF_SV, SWA_KV_W).astype(BF16)
    for c in range(0, 2 * D_MODEL, IN_N_CHUNK):
        gate_ref[:, c:c + IN_N_CHUNK] = proj(OFF_GR + c, IN_N_CHUNK).astype(BF16)
    for c in range(0, RET_V_W, IN_N_CHUNK):
        gate_ref[:, GATE_RG + c:GATE_RG + c + IN_N_CHUNK] = proj(OFF_RG + c, IN_N_CHUNK).astype(BF16)


def _inproj_cast_kernel(x_ref, g_ref, qg_ref, kg_ref, dec_ref, w_ref,
                        wr_ref, ws_ref, wo_ref, wu_ref, wd_ref,
                        mix_ref, kdt_ref, gate_ref, wbo_out_ref, wu_out_ref, wd_out_ref):
    _inproj_kernel(x_ref, g_ref, qg_ref, kg_ref, dec_ref, w_ref, mix_ref, kdt_ref, gate_ref)
    for k, src in enumerate((wr_ref, ws_ref, wo_ref)):
        wbo_out_ref[k] = src[...].astype(BF16)
    wu_out_ref[...] = wu_ref[...].astype(BF16)
    wd_out_ref[...] = wd_ref[...].astype(BF16)


BF16_ROW_TILE = 16


def _inproj(x3d, g, qg2, kg2, dec, w_bf16, block_rows, cast_weights=None):
    b, seq, _ = x3d.shape
    assert seq % block_rows == 0 and dec.shape == (block_rows, 2 * RET_QK_W)
    steps = seq // block_rows
    n_steps = b * steps
    const = lambda shape: pl.BlockSpec(shape, lambda i: (0, 0))
    out_shape = [jax.ShapeDtypeStruct((b, seq, MIX_W), BF16),
                 jax.ShapeDtypeStruct((b, RET_QK_W, seq), BF16),
                 jax.ShapeDtypeStruct((b, seq, GATE_W), BF16)]
    out_specs = [pl.BlockSpec((None, block_rows, MIX_W), lambda i: (i // steps, i % steps, 0)),
                 pl.BlockSpec((None, RET_QK_W, block_rows), lambda i: (i // steps, 0, i % steps)),
                 pl.BlockSpec((None, block_rows, GATE_W), lambda i: (i // steps, i % steps, 0))]
    extra_in, extra_specs = [], []
    if cast_weights is not None:
        band = lambda w: w.shape[1] // n_steps
        for w in cast_weights:
            assert w.shape[1] % n_steps == 0 and band(w) % BF16_ROW_TILE == 0
            extra_in.append(w)
            extra_specs.append(pl.BlockSpec((None, band(w), w.shape[2]), lambda i: (0, i, 0)))
        w_r, _, _, w_u, w_d = cast_weights
        out_shape += [jax.ShapeDtypeStruct((3,) + w_r.shape[1:], BF16),
                      jax.ShapeDtypeStruct(w_u.shape[1:], BF16),
                      jax.ShapeDtypeStruct(w_d.shape[1:], BF16)]
        out_specs += [pl.BlockSpec((3, band(w_r), w_r.shape[2]), lambda i: (0, i, 0)),
                      pl.BlockSpec((band(w_u), w_u.shape[2]), lambda i: (i, 0)),
                      pl.BlockSpec((band(w_d), w_d.shape[2]), lambda i: (i, 0))]
    return pl.pallas_call(
        _inproj_kernel if cast_weights is None else _inproj_cast_kernel,
        out_shape=tuple(out_shape),
        grid=(n_steps,),
        in_specs=[
            pl.BlockSpec((None, block_rows, D_MODEL), lambda i: (i // steps, i % steps, 0)),
            const((1, D_MODEL)),
            const((1, LANES)),
            const((1, LANES)),
            const((block_rows, 2 * RET_QK_W)),
            pl.BlockSpec((D_MODEL, D_IN), lambda i: (0, 0), pipeline_mode=pl.Buffered(1)),
        ] + extra_specs,
        out_specs=tuple(out_specs),
        compiler_params=pltpu.CompilerParams(
            dimension_semantics=("arbitrary",),
            vmem_limit_bytes=V7X_VMEM_LIMIT_BYTES,
        ),
        name="inproj",
    )(x3d, g, qg2, kg2, dec, w_bf16, *extra_in)


def _decay_tables(block_rows):
    lg = _retention_log_gamma()
    pos = (np.arange(block_rows) % RET_CHUNK).astype(np.float64) + 1.0
    qd = np.exp(lg[None, :] * pos[:, None])
    kd = np.exp(-lg[None, :] * pos[:, None]) * (RET_QK_DIM ** -0.5)
    rep = lambda t: np.repeat(t, RET_QK_DIM, axis=1)
    return np.concatenate([rep(qd), rep(kd)], axis=1).astype(np.float32)


def _meta_decay_tables():
    lg = _retention_log_gamma()
    pos = np.arange(CHUNK, dtype=np.float64)
    kd = np.exp(lg[None, :] * (CHUNK - 1.0 - pos[:, None])) * (RET_QK_DIM ** -0.5)
    rep = lambda t: np.repeat(t, RET_QK_DIM, axis=1)
    return np.concatenate([np.ones((CHUNK, RET_QK_W)), rep(kd)], axis=1).astype(np.float32)


RET_CHUNK_DECAY = [float(c) for c in np.exp(_retention_log_gamma() * RET_CHUNK)]
PAIR_ROWS = 2 * CHUNK
BACK_ROWS = 256
DENSE_ROWS = 256
assert BACK_ROWS % RET_CHUNK == 0 and BLOCK_ROWS % BACK_ROWS == 0 and BACK_ROWS % DENSE_ROWS == 0
CHUNKS_PER_BLOCK = BACK_ROWS // CHUNK
RET_PER_BLOCK = BACK_ROWS // RET_CHUNK
KV_SLOTS = CHUNKS_PER_BLOCK + 1
FF_CHUNK = 1024


def _swa_consts():
    slopes = 2.0 ** (-8.0 * np.arange(1, SWA_Q_HEADS + 1, dtype=np.float64) / SWA_Q_HEADS)
    slopes = slopes.astype(np.float32)
    i = np.arange(CHUNK)[:, None]
    j = np.arange(CHUNK)[None, :]
    band_dist = np.where(j <= i, i - j, i - j + CHUNK).astype(np.float32)
    meta_dist = np.where(j < 2 * N_META, i - (j % N_META) + N_META, np.inf).astype(np.float32)
    band_dist = np.stack([np.where(j <= i, band_dist, np.inf), band_dist]).astype(np.float32)
    return [float(s) for s in slopes], band_dist, meta_dist


ALIBI_SLOPES = _swa_consts()[0]


def _back_kernel(blocks_per_batch,
                 sink_ref,
                 mix_ref, kdt_ref, rg_ref, meta_ref, mkdt_ref, dist_ref, gains_ref,
                 x_ref, gate_ref, wbo_ref, wu_ref, wd_ref,
                 o_ref,
                 bd_ref, vbuf_ref, bdm_ref, vm_ref, state_ref, ys_ref, yr_ref, hn_ref):
    sq_ref = mix_ref.at[:, MIX_SQ:MIX_SQ + SWA_Q_W]
    rv_ref = mix_ref.at[:, MIX_RV:MIX_RV + RET_V_W]
    qd_ref = mix_ref.at[:, MIX_RQ:MIX_RQ + RET_QK_W]
    sk_ref = mix_ref.at[:, MIX_SK:MIX_SK + SWA_KV_W]
    sv_ref = mix_ref.at[:, MIX_SV:MIX_SV + SWA_KV_W]
    mk_ref = meta_ref.at[0, PAD:CHUNK, MIX_SK:MIX_SK + SWA_KV_W]
    mv_ref = meta_ref.at[0, PAD:CHUNK, MIX_SV:MIX_SV + SWA_KV_W]
    mrv_ref = meta_ref.at[0, :, MIX_RV:MIX_RV + RET_V_W]
    mkdt_ref = mkdt_ref.at[0]
    band_dist_ref = dist_ref.at[0:2]
    meta_dist_ref = dist_ref.at[2]
    rn_ref = gains_ref.at[0:1, :]
    g_ref = gains_ref.at[1:2, :]
    gr_ref = gate_ref.at[:, 0:D_MODEL]
    gs_ref = gate_ref.at[:, D_MODEL:2 * D_MODEL]
    wr_ref, ws_ref, wo_ref = wbo_ref.at[0], wbo_ref.at[1], wbo_ref.at[2]
    t = pl.program_id(0)
    block = jnp.minimum(t, pl.num_programs(0) - 2)
    block_in_batch = block % blocks_per_batch
    parity = t & 1
    last_slot = CHUNKS_PER_BLOCK - 1 + parity
    carried_slot = CHUNKS_PER_BLOCK - parity
    zeros_head = jnp.zeros((CHUNK, SWA_HEAD_DIM), BF16)

    @pl.when(t == 0)
    def _():
        ys_ref[...] = jnp.zeros_like(ys_ref)
        yr_ref[...] = jnp.zeros_like(yr_ref)

    @pl.when(block_in_batch == 0)
    def _():
        bd_ref[...] = jnp.zeros_like(bd_ref)
        vbuf_ref[...] = jnp.zeros_like(vbuf_ref)
        bdm_ref[...] = jnp.zeros_like(bdm_ref)
        vm_ref[...] = jnp.zeros_like(vm_ref)
        zm = jnp.zeros((N_META, SWA_HEAD_DIM), BF16)
        for g in range(SWA_KV_HEADS):
            cols = slice(g * SWA_HEAD_DIM, (g + 1) * SWA_HEAD_DIM)
            km = mk_ref[:, cols]
            bdm_ref[g, 0:N_META, :] = jnp.concatenate([km, zm], axis=1)
            bdm_ref[g, N_META:2 * N_META, :] = jnp.concatenate([zm, km], axis=1)
            vmeta = mv_ref[:, cols]
            vm_ref[g, 0:N_META, :] = jnp.concatenate([vmeta, zm], axis=1)
            vm_ref[g, N_META:2 * N_META, :] = jnp.concatenate([zm, vmeta], axis=1)
        for h in range(RET_HEADS):
            state_ref[h] = _dot(mkdt_ref[h * RET_QK_DIM:(h + 1) * RET_QK_DIM, :],
                                mrv_ref[:, h * RET_V_DIM:(h + 1) * RET_V_DIM])

    slabs = [slice(r, r + DENSE_ROWS) for r in range(0, BACK_ROWS, DENSE_ROWS)]
    for rs in slabs:
        merged = (_sigmoid(gr_ref[rs, :].astype(F32)) * _dot(yr_ref[rs, :], wr_ref[...])
                  + _sigmoid(gs_ref[rs, :].astype(F32)) * _dot(ys_ref[rs, :], ws_ref[...]))
        o_ref[rs, :] = x_ref[rs, :] + _dot(merged.astype(BF16), wo_ref[...])
        h1 = o_ref[rs, :]
        hn_ref[rs, :] = (h1 * lax.rsqrt(jnp.mean(h1 * h1, axis=-1, keepdims=True) + EPS)
                         * g_ref[...]).astype(BF16)

    ret_row = lax.broadcasted_iota(jnp.int32, (RET_CHUNK, RET_CHUNK), 0)
    ret_col = lax.broadcasted_iota(jnp.int32, (RET_CHUNK, RET_CHUNK), 1)
    ret_causal = ret_col <= ret_row

    def retention_head(sub, h):
        rows = slice(sub * RET_CHUNK, (sub + 1) * RET_CHUNK)
        vcols = slice(h * RET_V_DIM, (h + 1) * RET_V_DIM)
        qd = qd_ref[rows, h * RET_QK_DIM:(h + 1) * RET_QK_DIM]
        kdt = kdt_ref[h * RET_QK_DIM:(h + 1) * RET_QK_DIM, rows]
        v = rv_ref[rows, vcols]
        state = state_ref[h]
        s = jnp.where(ret_causal, _dot(qd, kdt), 0.0).astype(BF16)
        o = _dot(s, v) + _dot(qd, state.astype(BF16))
        state_ref[h] = RET_CHUNK_DECAY[h] * (state + _dot(kdt, v))
        ms = jnp.mean(o * o, axis=-1, keepdims=True)
        gate = rg_ref[rows, vcols].astype(F32)
        yr_ref[rows, vcols] = (o * lax.rsqrt(ms + EPS) * rn_ref[:, vcols]
                               * (gate * _sigmoid(gate))).astype(BF16)

    row = lax.broadcasted_iota(jnp.int32, (CHUNK, CHUNK), 0)
    col = lax.broadcasted_iota(jnp.int32, (CHUNK, CHUNK), 1)
    own = col <= row
    own2 = jnp.concatenate([own, own], axis=1)
    own4 = jnp.concatenate([own2, own2], axis=0)
    meta_seg = [col < N_META, (col >= N_META) & (col < 2 * N_META)]
    low_half = col < SWA_HEAD_DIM
    neg_inf = jnp.float32(-jnp.inf)

    first_chunk = (CHUNKS_PER_BLOCK * block_in_batch).astype(F32)
    chunks = []
    for c in range(CHUNKS_PER_BLOCK):
        slot = c if c < CHUNKS_PER_BLOCK - 1 else last_slot
        prev_slot = c - 1 if c > 0 else carried_slot
        bias_sel = jnp.minimum(block_in_batch, 1) if c == 0 else 1
        chunks.append((slice(c * CHUNK, (c + 1) * CHUNK), slot, prev_slot, bias_sel, first_chunk + float(c)))

    for rows, slot, _, _, _ in chunks:
        for g in range(SWA_KV_HEADS):
            kg = sk_ref[rows, g * SWA_HEAD_DIM:(g + 1) * SWA_HEAD_DIM]
            bd_ref[slot, g, 0:CHUNK, :] = jnp.concatenate([kg, zeros_head], axis=1)
            bd_ref[slot, g, CHUNK:2 * CHUNK, :] = jnp.concatenate([zeros_head, kg], axis=1)
            vg = sv_ref[rows, g * SWA_HEAD_DIM:(g + 1) * SWA_HEAD_DIM]
            vbuf_ref[slot, g] = jnp.concatenate([vg, vg], axis=1)

    def scores(chunk, g):
        rows, slot, prev_slot, bias_sel, chunk_idx = chunk
        q2 = jnp.concatenate(
            [sq_ref[rows, (2 * g + p) * LANES:(2 * g + p + 1) * LANES] for p in range(2)],
            axis=0)
        s_own = _dot_nt(q2, bd_ref[slot, g])
        s_prev = _dot_nt(q2, bd_ref[prev_slot, g])
        dist = band_dist_ref[bias_sel]
        bias = jnp.concatenate(
            [jnp.concatenate([ALIBI_SLOPES[g * SWA_GROUP + 2 * p + e] * dist for e in range(2)], axis=1)
             for p in range(2)], axis=0)
        s = jnp.where(own4, s_own, s_prev) - bias
        dist_meta = meta_dist_ref[...] + float(CHUNK) * chunk_idx
        bias_meta = jnp.concatenate(
            [jnp.where(meta_seg[0], ALIBI_SLOPES[g * SWA_GROUP + 2 * p],
                       ALIBI_SLOPES[g * SWA_GROUP + 2 * p + 1]) * dist_meta for p in range(2)], axis=0)
        s_meta = _dot_nt(q2, bdm_ref[g]) - bias_meta
        return s, s_meta

    def softmax(g, s, s_meta):
        lhs = []
        lhs_meta = []
        inv = []
        for p in range(2):
            sm_p = s_meta[p * CHUNK:(p + 1) * CHUNK]
            pm_pair = None
            for e in range(2):
                h = g * SWA_GROUP + 2 * p + e
                sink = sink_ref[h]
                sb = s[p * CHUNK:(p + 1) * CHUNK, e * CHUNK:(e + 1) * CHUNK]
                sm = jnp.where(meta_seg[e], sm_p, neg_inf)
                m = jnp.maximum(jnp.max(jnp.maximum(sb, sm), axis=-1, keepdims=True), sink)
                pb = jnp.exp(sb - m)
                pm = jnp.exp(sm - m)
                denom = jnp.sum(pb + pm, axis=-1, keepdims=True) + jnp.exp(sink - m)
                inv.append(1.0 / denom)
                lhs.append(jnp.concatenate(
                    [jnp.where(own, pb, 0.0).astype(BF16), jnp.where(own, 0.0, pb).astype(BF16)], axis=1))
                pm_pair = pm if pm_pair is None else pm_pair + pm
            lhs_meta.append(pm_pair.astype(BF16))
        return jnp.concatenate(lhs, axis=0), jnp.concatenate(lhs_meta, axis=0), inv

    def weighted_values(chunk, g, lhs, lhs_meta, inv):
        rows, slot, prev_slot, _, _ = chunk
        vv = jnp.concatenate([vbuf_ref[slot, g], vbuf_ref[prev_slot, g]], axis=0)
        acc = _dot(lhs, vv)
        acc_meta = _dot(lhs_meta, vm_ref[g])
        for p in range(2):
            first, second = 2 * p, 2 * p + 1
            pair = jnp.where(low_half, acc[first * CHUNK:(first + 1) * CHUNK],
                             acc[second * CHUNK:(second + 1) * CHUNK])
            scale = jnp.where(low_half, inv[first], inv[second])
            tile = (pair + acc_meta[p * CHUNK:(p + 1) * CHUNK]) * scale
            ys_ref[rows, (2 * g + p) * LANES:(2 * g + p + 1) * LANES] = tile.astype(BF16)

    tasks = [(chunk, g) for chunk in chunks for g in range(SWA_KV_HEADS)]
    ret_tasks = [(sub, h) for sub in range(RET_PER_BLOCK) for h in range(RET_HEADS)]
    assert len(tasks) == 2 * len(ret_tasks)

    def mlp_phase(k):
        c = k * FF_CHUNK
        for rs in slabs:
            u = jnp.maximum(_dot(hn_ref[rs, :], wu_ref[:, c:c + FF_CHUNK]), 0.0)
            o_ref[rs, :] += _dot((u * u).astype(BF16), wd_ref[c:c + FF_CHUNK, :])

    n_mlp_phases = D_FF // FF_CHUNK
    phase_at_task = {i * len(tasks) // n_mlp_phases: i for i in range(n_mlp_phases)}

    pending_scores = scores(*tasks[0])
    pending_probs = None
    for i, (chunk, g) in enumerate(tasks):
        current_scores = pending_scores
        if i + 1 < len(tasks):
            pending_scores = scores(*tasks[i + 1])
        if i in phase_at_task:
            mlp_phase(phase_at_task[i])
        probs = softmax(g, *current_scores)
        if i % 2 == 0:
            retention_head(*ret_tasks[i // 2])
        if pending_probs is not None:
            weighted_values(*tasks[i - 1], *pending_probs)
        pending_probs = probs
    weighted_values(*tasks[-1], *pending_probs)


def _back(x2d, mix2d, kdt, gates2d, meta_mix, meta_kdt, sinks, norm_gains, w_branch_out, w_up, w_down,
          blocks_per_batch):
    rows = x2d.shape[0]
    assert rows % BACK_ROWS == 0
    n_blocks = rows // BACK_ROWS
    _, band_dist, meta_dist = _swa_consts()

    cur = lambda t: jnp.minimum(t, n_blocks - 1)
    prev = lambda t: jnp.maximum(t - 1, 0)

    def resident(shape):
        zeros = (0,) * len(shape)
        return pl.BlockSpec(shape, lambda t, s: zeros, pipeline_mode=pl.Buffered(1))

    grid_spec = pltpu.PrefetchScalarGridSpec(
        num_scalar_prefetch=1,
        grid=(n_blocks + 1,),
        in_specs=[
            pl.BlockSpec((BACK_ROWS, MIX_W), lambda t, s: (cur(t), 0)),
            pl.BlockSpec((None, RET_QK_W, BACK_ROWS),
                         lambda t, s: (cur(t) // blocks_per_batch, 0, cur(t) % blocks_per_batch)),
            pl.BlockSpec((BACK_ROWS, RET_V_W), lambda t, s: (cur(t), GATE_RG // RET_V_W)),
            resident((1, CHUNK, MIX_W)),
            resident((1, RET_QK_W, CHUNK)),
            resident((3, CHUNK, CHUNK)),
            resident((2, D_MODEL)),
            pl.BlockSpec((BACK_ROWS, D_MODEL), lambda t, s: (prev(t), 0)),
            pl.BlockSpec((BACK_ROWS, 2 * D_MODEL), lambda t, s: (prev(t), 0)),
            resident((3, D_MODEL, D_MODEL)),
            resident((D_MODEL, D_FF)),
            resident((D_FF, D_MODEL)),
        ],
        out_specs=pl.BlockSpec((BACK_ROWS, D_MODEL), lambda t, s: (prev(t), 0)),
        scratch_shapes=[
            pltpu.VMEM((KV_SLOTS, SWA_KV_HEADS, PAIR_ROWS, LANES), BF16),
            pltpu.VMEM((KV_SLOTS, SWA_KV_HEADS, CHUNK, LANES), BF16),
            pltpu.VMEM((SWA_KV_HEADS, CHUNK, LANES), BF16),
            pltpu.VMEM((SWA_KV_HEADS, CHUNK, LANES), BF16),
            pltpu.VMEM((RET_HEADS, RET_QK_DIM, RET_V_DIM), F32),
            pltpu.VMEM((BACK_ROWS, SWA_Q_W), BF16),
            pltpu.VMEM((BACK_ROWS, RET_V_W), BF16),
            pltpu.VMEM((BACK_ROWS, D_MODEL), BF16),
        ],
    )
    return pl.pallas_call(
        functools.partial(_back_kernel, blocks_per_batch),
        out_shape=jax.ShapeDtypeStruct((rows, D_MODEL), F32),
        grid_spec=grid_spec,
        compiler_params=pltpu.CompilerParams(
            dimension_semantics=("arbitrary",),
            vmem_limit_bytes=V7X_VMEM_LIMIT_BYTES,
        ),
        name="back",
    )(sinks.astype(F32), mix2d, kdt, gates2d, meta_mix, meta_kdt,
      jnp.asarray(np.concatenate([band_dist, meta_dist[None]])), norm_gains,
      x2d, gates2d, w_branch_out, w_up, w_down)


def kernel(x, meta_tokens, mix_norm_g, w_in, ret_norm_g, q_norm_g, k_norm_g, sinks,
           w_ret_branch, w_swa_branch, w_out, mlp_norm_g, w_up, w_down):
    b, l, d = x.shape
    assert d == D_MODEL and l % BLOCK_ROWS == 0 and mix_norm_g.shape[0] == 1
    mix_g = mix_norm_g[0].reshape(1, d).astype(F32)
    w_in_b = w_in[0].astype(BF16)
    qg2 = jnp.tile(q_norm_g[0].astype(F32) * (SWA_HEAD_DIM ** -0.5), 2).reshape(1, LANES)
    kg2 = jnp.tile(k_norm_g[0].astype(F32), 2).reshape(1, LANES)

    mix, kdt, gates, w_branch_out, w_up_b, w_down_b = _inproj(
        x, mix_g, qg2, kg2, jnp.asarray(_decay_tables(BLOCK_ROWS)), w_in_b, BLOCK_ROWS,
        cast_weights=(w_ret_branch, w_swa_branch, w_out, w_up, w_down))
    meta_chunk = jnp.pad(meta_tokens.astype(F32), ((PAD, 0), (0, 0)))[None]
    meta_mix, meta_kdt, _ = _inproj(meta_chunk, mix_g, qg2, kg2, jnp.asarray(_meta_decay_tables()),
                                    w_in_b, CHUNK)

    norm_gains = jnp.stack([ret_norm_g[0].reshape(RET_V_W), mlp_norm_g[0]]).astype(F32)
    out = _back(
        x.reshape(b * l, d), mix.reshape(b * l, MIX_W), kdt, gates.reshape(b * l, GATE_W),
        meta_mix, meta_kdt, sinks[0], norm_gains, w_branch_out, w_up_b, w_down_b, l // BACK_ROWS)
    return out.reshape(b, l, d)
```

```python
import functools

import numpy as np
import jax
import jax.numpy as jnp
from jax import lax
from jax.experimental import pallas as pl
from jax.experimental.pallas import tpu as pltpu

D_MODEL = 1024
N_META = 16
CHUNK = 128
PAD = CHUNK - N_META
RET_HEADS = 4
RET_QK_DIM = 128
RET_V_DIM = 256
SWA_Q_HEADS = 16
SWA_KV_HEADS = 4
SWA_GROUP = SWA_Q_HEADS // SWA_KV_HEADS
SWA_HEAD_DIM = 64
D_FF = 4 * D_MODEL
EPS = 1e-6
RET_QK_W = RET_HEADS * RET_QK_DIM
RET_V_W = RET_HEADS * RET_V_DIM
SWA_Q_W = SWA_Q_HEADS * SWA_HEAD_DIM
SWA_KV_W = SWA_KV_HEADS * SWA_HEAD_DIM
D_IN = 2 * RET_QK_W + 2 * RET_V_W + SWA_Q_W + 2 * SWA_KV_W + 2 * D_MODEL

OFF_RQ = 0
OFF_RK = OFF_RQ + RET_QK_W
OFF_RV = OFF_RK + RET_QK_W
OFF_RG = OFF_RV + RET_V_W
OFF_SQ = OFF_RG + RET_V_W
OFF_SK = OFF_SQ + SWA_Q_W
OFF_SV = OFF_SK + SWA_KV_W
OFF_GR = OFF_SV + SWA_KV_W
OFF_GS = OFF_GR + D_MODEL

MIX_SQ = 0
MIX_RV = MIX_SQ + SWA_Q_W
MIX_RQ = MIX_RV + RET_V_W
MIX_SK = MIX_RQ + RET_QK_W
MIX_SV = MIX_SK + SWA_KV_W
MIX_W = MIX_SV + SWA_KV_W
GATE_RG = 2 * D_MODEL
GATE_W = GATE_RG + RET_V_W

V7X_VMEM_LIMIT_BYTES = 64 * 1024 * 1024
LANES = 128
assert LANES == 2 * SWA_HEAD_DIM and CHUNK == LANES and RET_V_W == D_MODEL

F32 = jnp.float32
BF16 = jnp.bfloat16

BLOCK_ROWS = 512
RET_CHUNK = 256
assert BLOCK_ROWS % RET_CHUNK == 0 and BLOCK_ROWS % CHUNK == 0


def _sigmoid(x):
    return 1.0 / (1.0 + jnp.exp(-x))


def _dot(a, b):
    return jnp.dot(a, b, preferred_element_type=F32)


def _dot_nt(a, b):
    return lax.dot_general(a, b, (((1,), (1,)), ((), ())), preferred_element_type=F32)


def _retention_log_gamma():
    return np.log1p(-(2.0 ** (-5.0 - np.arange(RET_HEADS, dtype=np.float64))))


IN_N_CHUNK = 512


def _head_pair_rms(y, g2):
    rows, width = y.shape
    low = lax.broadcasted_iota(jnp.int32, (rows, LANES), 1) < SWA_HEAD_DIM
    out = []
    for c in range(0, width, LANES):
        yb = y[:, c:c + LANES]
        y2 = yb * yb
        ms_lo = jnp.sum(jnp.where(low, y2, 0.0), axis=-1, keepdims=True) * (1.0 / SWA_HEAD_DIM)
        ms_hi = jnp.sum(jnp.where(low, 0.0, y2), axis=-1, keepdims=True) * (1.0 / SWA_HEAD_DIM)
        scale = jnp.where(low, lax.rsqrt(ms_lo + EPS), lax.rsqrt(ms_hi + EPS))
        out.append(yb * scale * g2)
    return jnp.concatenate(out, axis=1) if len(out) > 1 else out[0]


def _inproj_kernel(x_ref, g_ref, qg_ref, kg_ref, dec_ref, w_ref, mix_ref, kdt_ref, gate_ref):
    x = x_ref[...]
    ms = jnp.mean(x * x, axis=-1, keepdims=True)
    hn = (x * lax.rsqrt(ms + EPS) * g_ref[...]).astype(BF16)

    def proj(lo, width):
        return _dot(hn, w_ref[:, lo:lo + width])

    for c in range(0, SWA_Q_W, IN_N_CHUNK):
        mix_ref[:, MIX_SQ + c:MIX_SQ + c + IN_N_CHUNK] = _head_pair_rms(
            proj(OFF_SQ + c, IN_N_CHUNK), qg_ref[...]).astype(BF16)
    for c in range(0, RET_V_W, IN_N_CHUNK):
        mix_ref[:, MIX_RV + c:MIX_RV + c + IN_N_CHUNK] = proj(OFF_RV + c, IN_N_CHUNK).astype(BF16)
    mix_ref[:, MIX_RQ:MIX_RQ + RET_QK_W] = (proj(OFF_RQ, RET_QK_W) * dec_ref[:, 0:RET_QK_W]).astype(BF16)
    kd = proj(OFF_RK, RET_QK_W) * dec_ref[:, RET_QK_W:2 * RET_QK_W]
    kdt_ref[...] = kd.T.astype(BF16)
    mix_ref[:, MIX_SK:MIX_SK + SWA_KV_W] = _head_pair_rms(proj(OFF_SK, SWA_KV_W), kg_ref[...]).astype(BF16)
    mix_ref[:, MIX_SV:MIX_SV + SWA_KV_W] = proj(OFF_SV, SWA_KV_W).astype(BF16)
    for c in range(0, 2 * D_MODEL, IN_N_CHUNK):
        gate_ref[:, c:c + IN_N_CHUNK] = proj(OFF_GR + c, IN_N_CHUNK).astype(BF16)
    for c in range(0, RET_V_W, IN_N_CHUNK):
        gate_ref[:, GATE_RG + c:GATE_RG + c + IN_N_CHUNK] = proj(OFF_RG + c, IN_N_CHUNK).astype(BF16)


def _inproj_cast_kernel(x_ref, g_ref, qg_ref, kg_ref, dec_ref, w_ref,
                        wr_ref, ws_ref, wo_ref, wu_ref, wd_ref,
                        mix_ref, kdt_ref, gate_ref, wbo_out_ref, wu_out_ref, wd_out_ref):
    _inproj_kernel(x_ref, g_ref, qg_ref, kg_ref, dec_ref, w_ref, mix_ref, kdt_ref, gate_ref)
    for k, src in enumerate((wr_ref, ws_ref, wo_ref)):
        wbo_out_ref[k] = src[...].astype(BF16)
    wu_out_ref[...] = wu_ref[...].astype(BF16)
    wd_out_ref[...] = wd_ref[...].astype(BF16)


BF16_ROW_TILE = 16


def _inproj(x3d, g, qg2, kg2, dec, w_bf16, block_rows, cast_weights=None):
    b, seq, _ = x3d.shape
    assert seq % block_rows == 0 and dec.shape == (block_rows, 2 * RET_QK_W)
    steps = seq // block_rows
    n_steps = b * steps
    const = lambda shape: pl.BlockSpec(shape, lambda i: (0, 0))
    out_shape = [jax.ShapeDtypeStruct((b, seq, MIX_W), BF16),
                 jax.ShapeDtypeStruct((b, RET_QK_W, seq), BF16),
                 jax.ShapeDtypeStruct((b, seq, GATE_W), BF16)]
    out_specs = [pl.BlockSpec((None, block_rows, MIX_W), lambda i: (i // steps, i % steps, 0)),
                 pl.BlockSpec((None, RET_QK_W, block_rows), lambda i: (i // steps, 0, i % steps)),
                 pl.BlockSpec((None, block_rows, GATE_W), lambda i: (i // steps, i % steps, 0))]
    extra_in, extra_specs = [], []
    if cast_weights is not None:
        band = lambda w: w.shape[1] // n_steps
        for w in cast_weights:
            assert w.shape[1] % n_steps == 0 and band(w) % BF16_ROW_TILE == 0
            extra_in.append(w)
            extra_specs.append(pl.BlockSpec((None, band(w), w.shape[2]), lambda i: (0, i, 0)))
        w_r, _, _, w_u, w_d = cast_weights
        out_shape += [jax.ShapeDtypeStruct((3,) + w_r.shape[1:], BF16),
                      jax.ShapeDtypeStruct(w_u.shape[1:], BF16),
                      jax.ShapeDtypeStruct(w_d.shape[1:], BF16)]
        out_specs += [pl.BlockSpec((3, band(w_r), w_r.shape[2]), lambda i: (0, i, 0)),
                      pl.BlockSpec((band(w_u), w_u.shape[2]), lambda i: (i, 0)),
                      pl.BlockSpec((band(w_d), w_d.shape[2]), lambda i: (i, 0))]
    return pl.pallas_call(
        _inproj_kernel if cast_weights is None else _inproj_cast_kernel,
        out_shape=tuple(out_shape),
        grid=(n_steps,),
        in_specs=[
            pl.BlockSpec((None, block_rows, D_MODEL), lambda i: (i // steps, i % steps, 0)),
            const((1, D_MODEL)),
            const((1, LANES)),
            const((1, LANES)),
            const((block_rows, 2 * RET_QK_W)),
            pl.BlockSpec((D_MODEL, D_IN), lambda i: (0, 0), pipeline_mode=pl.Buffered(1)),
        ] + extra_specs,
        out_specs=tuple(out_specs),
        compiler_params=pltpu.CompilerParams(
            dimension_semantics=("arbitrary",),
            vmem_limit_bytes=V7X_VMEM_LIMIT_BYTES,
        ),
        name="inproj",
    )(x3d, g, qg2, kg2, dec, w_bf16, *extra_in)


def _decay_tables(block_rows):
    lg = _retention_log_gamma()
    pos = (np.arange(block_rows) % RET_CHUNK).astype(np.float64) + 1.0
    qd = np.exp(lg[None, :] * pos[:, None])
    kd = np.exp(-lg[None, :] * pos[:, None]) * (RET_QK_DIM ** -0.5)
    rep = lambda t: np.repeat(t, RET_QK_DIM, axis=1)
    return np.concatenate([rep(qd), rep(kd)], axis=1).astype(np.float32)


def _meta_decay_tables():
    lg = _retention_log_gamma()
    pos = np.arange(CHUNK, dtype=np.float64)
    kd = np.exp(lg[None, :] * (CHUNK - 1.0 - pos[:, None])) * (RET_QK_DIM ** -0.5)
    rep = lambda t: np.repeat(t, RET_QK_DIM, axis=1)
    return np.concatenate([np.ones((CHUNK, RET_QK_W)), rep(kd)], axis=1).astype(np.float32)


RET_CHUNK_DECAY = [float(c) for c in np.exp(_retention_log_gamma() * RET_CHUNK)]
PAIR_ROWS = 2 * CHUNK
BACK_ROWS = 512
DENSE_ROWS = 512
assert BACK_ROWS % RET_CHUNK == 0 and BLOCK_ROWS % BACK_ROWS == 0 and BACK_ROWS % DENSE_ROWS == 0
CHUNKS_PER_BLOCK = BACK_ROWS // CHUNK
RET_PER_BLOCK = BACK_ROWS // RET_CHUNK
KV_SLOTS = CHUNKS_PER_BLOCK + 1
FF_CHUNK = 1024


def _swa_consts():
    slopes = 2.0 ** (-8.0 * np.arange(1, SWA_Q_HEADS + 1, dtype=np.float64) / SWA_Q_HEADS)
    slopes = slopes.astype(np.float32)
    i = np.arange(CHUNK)[:, None]
    j = np.arange(CHUNK)[None, :]
    band_dist = np.where(j <= i, i - j, i - j + CHUNK).astype(np.float32)
    meta_dist = np.where(j < 2 * N_META, i - (j % N_META) + N_META, np.inf).astype(np.float32)
    band_dist = np.stack([np.where(j <= i, band_dist, np.inf), band_dist]).astype(np.float32)
    return [float(s) for s in slopes], band_dist, meta_dist


ALIBI_SLOPES = _swa_consts()[0]


def _back_kernel(blocks_per_batch,
                 sink_ref,
                 mix_ref, kdt_ref, rg_ref, meta_ref, mkdt_ref, dist_ref, gains_ref,
                 x_ref, gate_ref, wbo_ref, wu_ref, wd_ref,
                 o_ref,
                 bd_ref, vbuf_ref, bdm_ref, vm_ref, state_ref, ys_ref, yr_ref, hn_ref):
    sq_ref = mix_ref.at[:, MIX_SQ:MIX_SQ + SWA_Q_W]
    rv_ref = mix_ref.at[:, MIX_RV:MIX_RV + RET_V_W]
    qd_ref = mix_ref.at[:, MIX_RQ:MIX_RQ + RET_QK_W]
    sk_ref = mix_ref.at[:, MIX_SK:MIX_SK + SWA_KV_W]
    sv_ref = mix_ref.at[:, MIX_SV:MIX_SV + SWA_KV_W]
    mk_ref = meta_ref.at[0, PAD:CHUNK, MIX_SK:MIX_SK + SWA_KV_W]
    mv_ref = meta_ref.at[0, PAD:CHUNK, MIX_SV:MIX_SV + SWA_KV_W]
    mrv_ref = meta_ref.at[0, :, MIX_RV:MIX_RV + RET_V_W]
    mkdt_ref = mkdt_ref.at[0]
    band_dist_ref = dist_ref.at[0:2]
    meta_dist_ref = dist_ref.at[2]
    rn_ref = gains_ref.at[0:1, :]
    g_ref = gains_ref.at[1:2, :]
    gr_ref = gate_ref.at[:, 0:D_MODEL]
    gs_ref = gate_ref.at[:, D_MODEL:2 * D_MODEL]
    wr_ref, ws_ref, wo_ref = wbo_ref.at[0], wbo_ref.at[1], wbo_ref.at[2]
    t = pl.program_id(0)
    block = jnp.minimum(t, pl.num_programs(0) - 2)
    block_in_batch = block % blocks_per_batch
    parity = t & 1
    last_slot = CHUNKS_PER_BLOCK - 1 + parity
    carried_slot = CHUNKS_PER_BLOCK - parity
    zeros_head = jnp.zeros((CHUNK, SWA_HEAD_DIM), BF16)

    @pl.when(t == 0)
    def _():
        ys_ref[...] = jnp.zeros_like(ys_ref)
        yr_ref[...] = jnp.zeros_like(yr_ref)

    @pl.when(block_in_batch == 0)
    def _():
        bd_ref[...] = jnp.zeros_like(bd_ref)
        vbuf_ref[...] = jnp.zeros_like(vbuf_ref)
        bdm_ref[...] = jnp.zeros_like(bdm_ref)
        vm_ref[...] = jnp.zeros_like(vm_ref)
        zm = jnp.zeros((N_META, SWA_HEAD_DIM), BF16)
        for g in range(SWA_KV_HEADS):
            cols = slice(g * SWA_HEAD_DIM, (g + 1) * SWA_HEAD_DIM)
            km = mk_ref[:, cols]
            bdm_ref[g, 0:N_META, :] = jnp.concatenate([km, zm], axis=1)
            bdm_ref[g, N_META:2 * N_META, :] = jnp.concatenate([zm, km], axis=1)
            vmeta = mv_ref[:, cols]
            vm_ref[g, 0:N_META, :] = jnp.concatenate([vmeta, zm], axis=1)
            vm_ref[g, N_META:2 * N_META, :] = jnp.concatenate([zm, vmeta], axis=1)
        for h in range(RET_HEADS):
            state_ref[h] = _dot(mkdt_ref[h * RET_QK_DIM:(h + 1) * RET_QK_DIM, :],
                                mrv_ref[:, h * RET_V_DIM:(h + 1) * RET_V_DIM])

    slabs = [slice(r, r + DENSE_ROWS) for r in range(0, BACK_ROWS, DENSE_ROWS)]
    for rs in slabs:
        merged = (_sigmoid(gr_ref[rs, :].astype(F32)) * _dot(yr_ref[rs, :], wr_ref[...])
                  + _sigmoid(gs_ref[rs, :].astype(F32)) * _dot(ys_ref[rs, :], ws_ref[...]))
        o_ref[rs, :] = x_ref[rs, :] + _dot(merged.astype(BF16), wo_ref[...])
        h1 = o_ref[rs, :]
        hn_ref[rs, :] = (h1 * lax.rsqrt(jnp.mean(h1 * h1, axis=-1, keepdims=True) + EPS)
                         * g_ref[...]).astype(BF16)

    ret_row = lax.broadcasted_iota(jnp.int32, (RET_CHUNK, RET_CHUNK), 0)
    ret_col = lax.broadcasted_iota(jnp.int32, (RET_CHUNK, RET_CHUNK), 1)
    ret_causal = ret_col <= ret_row

    def retention_head(sub, h):
        rows = slice(sub * RET_CHUNK, (sub + 1) * RET_CHUNK)
        vcols = slice(h * RET_V_DIM, (h + 1) * RET_V_DIM)
        qd = qd_ref[rows, h * RET_QK_DIM:(h + 1) * RET_QK_DIM]
        kdt = kdt_ref[h * RET_QK_DIM:(h + 1) * RET_QK_DIM, rows]
        v = rv_ref[rows, vcols]
        state = state_ref[h]
        s = jnp.where(ret_causal, _dot(qd, kdt), 0.0).astype(BF16)
        o = _dot(s, v) + _dot(qd, state.astype(BF16))
        state_ref[h] = RET_CHUNK_DECAY[h] * (state + _dot(kdt, v))
        ms = jnp.mean(o * o, axis=-1, keepdims=True)
        gate = rg_ref[rows, vcols].astype(F32)
        yr_ref[rows, vcols] = (o * lax.rsqrt(ms + EPS) * rn_ref[:, vcols]
                               * (gate * _sigmoid(gate))).astype(BF16)

    row = lax.broadcasted_iota(jnp.int32, (CHUNK, CHUNK), 0)
    col = lax.broadcasted_iota(jnp.int32, (CHUNK, CHUNK), 1)
    own = col <= row
    own2 = jnp.concatenate([own, own], axis=1)
    own4 = jnp.concatenate([own2, own2], axis=0)
    meta_seg = [col < N_META, (col >= N_META) & (col < 2 * N_META)]
    low_half = col < SWA_HEAD_DIM
    neg_inf = jnp.float32(-jnp.inf)

    first_chunk = (CHUNKS_PER_BLOCK * block_in_batch).astype(F32)
    chunks = []
    for c in range(CHUNKS_PER_BLOCK):
        slot = c if c < CHUNKS_PER_BLOCK - 1 else last_slot
        prev_slot = c - 1 if c > 0 else carried_slot
        bias_sel = jnp.minimum(block_in_batch, 1) if c == 0 else 1
        chunks.append((slice(c * CHUNK, (c + 1) * CHUNK), slot, prev_slot, bias_sel, first_chunk + float(c)))

    for rows, slot, _, _, _ in chunks:
        for g in range(SWA_KV_HEADS):
            kg = sk_ref[rows, g * SWA_HEAD_DIM:(g + 1) * SWA_HEAD_DIM]
            bd_ref[slot, g, 0:CHUNK, :] = jnp.concatenate([kg, zeros_head], axis=1)
            bd_ref[slot, g, CHUNK:2 * CHUNK, :] = jnp.concatenate([zeros_head, kg], axis=1)
            vg = sv_ref[rows, g * SWA_HEAD_DIM:(g + 1) * SWA_HEAD_DIM]
            vbuf_ref[slot, g] = jnp.concatenate([vg, vg], axis=1)

    def scores(chunk, g):
        rows, slot, prev_slot, bias_sel, chunk_idx = chunk
        q2 = jnp.concatenate(
            [sq_ref[rows, (2 * g + p) * LANES:(2 * g + p + 1) * LANES] for p in range(2)],
            axis=0)
        s_own = _dot_nt(q2, bd_ref[slot, g])
        s_prev = _dot_nt(q2, bd_ref[prev_slot, g])
        dist = band_dist_ref[bias_sel]
        bias = jnp.concatenate(
            [jnp.concatenate([ALIBI_SLOPES[g * SWA_GROUP + 2 * p + e] * dist for e in range(2)], axis=1)
             for p in range(2)], axis=0)
        s = jnp.where(own4, s_own, s_prev) - bias
        dist_meta = meta_dist_ref[...] + float(CHUNK) * chunk_idx
        bias_meta = jnp.concatenate(
            [jnp.where(meta_seg[0], ALIBI_SLOPES[g * SWA_GROUP + 2 * p],
                       ALIBI_SLOPES[g * SWA_GROUP + 2 * p + 1]) * dist_meta for p in range(2)], axis=0)
        s_meta = _dot_nt(q2, bdm_ref[g]) - bias_meta
        return s, s_meta

    def softmax(g, s, s_meta):
        lhs = []
        lhs_meta = []
        inv = []
        for p in range(2):
            sm_p = s_meta[p * CHUNK:(p + 1) * CHUNK]
            pm_pair = None
            for e in range(2):
                h = g * SWA_GROUP + 2 * p + e
                sink = sink_ref[h]
                sb = s[p * CHUNK:(p + 1) * CHUNK, e * CHUNK:(e + 1) * CHUNK]
                sm = jnp.where(meta_seg[e], sm_p, neg_inf)
                m = jnp.maximum(jnp.max(jnp.maximum(sb, sm), axis=-1, keepdims=True), sink)
                pb = jnp.exp(sb - m)
                pm = jnp.exp(sm - m)
                denom = jnp.sum(pb + pm, axis=-1, keepdims=True) + jnp.exp(sink - m)
                inv.append(1.0 / denom)
                lhs.append(jnp.concatenate(
                    [jnp.where(own, pb, 0.0).astype(BF16), jnp.where(own, 0.0, pb).astype(BF16)], axis=1))
                pm_pair = pm if pm_pair is None else pm_pair + pm
            lhs_meta.append(pm_pair.astype(BF16))
        return jnp.concatenate(lhs, axis=0), jnp.concatenate(lhs_meta, axis=0), inv

    def weighted_values(chunk, g, lhs, lhs_meta, inv):
        rows, slot, prev_slot, _, _ = chunk
        vv = jnp.concatenate([vbuf_ref[slot, g], vbuf_ref[prev_slot, g]], axis=0)
        acc = _dot(lhs, vv)
        acc_meta = _dot(lhs_meta, vm_ref[g])
        for p in range(2):
            first, second = 2 * p, 2 * p + 1
            pair = jnp.where(low_half, acc[first * CHUNK:(first + 1) * CHUNK],
                             acc[second * CHUNK:(second + 1) * CHUNK])
            scale = jnp.where(low_half, inv[first], inv[second])
            tile = (pair + acc_meta[p * CHUNK:(p + 1) * CHUNK]) * scale
            ys_ref[rows, (2 * g + p) * LANES:(2 * g + p + 1) * LANES] = tile.astype(BF16)

    tasks = [(chunk, g) for chunk in chunks for g in range(SWA_KV_HEADS)]
    ret_tasks = [(sub, h) for sub in range(RET_PER_BLOCK) for h in range(RET_HEADS)]
    assert len(tasks) == 2 * len(ret_tasks)

    def mlp_phase(k):
        c = k * FF_CHUNK
        for rs in slabs:
            u = jnp.maximum(_dot(hn_ref[rs, :], wu_ref[:, c:c + FF_CHUNK]), 0.0)
            o_ref[rs, :] += _dot((u * u).astype(BF16), wd_ref[c:c + FF_CHUNK, :])

    n_mlp_phases = D_FF // FF_CHUNK
    phase_at_task = {i * len(tasks) // n_mlp_phases: i for i in range(n_mlp_phases)}

    pending_scores = scores(*tasks[0])
    pending_probs = None
    for i, (chunk, g) in enumerate(tasks):
        current_scores = pending_scores
        if i + 1 < len(tasks):
            pending_scores = scores(*tasks[i + 1])
        if i in phase_at_task:
            mlp_phase(phase_at_task[i])
        probs = softmax(g, *current_scores)
        if i % 2 == 0:
            retention_head(*ret_tasks[i // 2])
        if pending_probs is not None:
            weighted_values(*tasks[i - 1], *pending_probs)
        pending_probs = probs
    weighted_values(*tasks[-1], *pending_probs)


def _back(x2d, mix2d, kdt, gates2d, meta_mix, meta_kdt, sinks, norm_gains, w_branch_out, w_up, w_down,
          blocks_per_batch):
    rows = x2d.shape[0]
    assert rows % BACK_ROWS == 0
    n_blocks = rows // BACK_ROWS
    _, band_dist, meta_dist = _swa_consts()

    cur = lambda t: jnp.minimum(t, n_blocks - 1)
    prev = lambda t: jnp.maximum(t - 1, 0)

    def resident(shape):
        zeros = (0,) * len(shape)
        return pl.BlockSpec(shape, lambda t, s: zeros, pipeline_mode=pl.Buffered(1))

    grid_spec = pltpu.PrefetchScalarGridSpec(
        num_scalar_prefetch=1,
        grid=(n_blocks + 1,),
        in_specs=[
            pl.BlockSpec((BACK_ROWS, MIX_W), lambda t, s: (cur(t), 0)),
            pl.BlockSpec((None, RET_QK_W, BACK_ROWS),
                         lambda t, s: (cur(t) // blocks_per_batch, 0, cur(t) % blocks_per_batch)),
            pl.BlockSpec((BACK_ROWS, RET_V_W), lambda t, s: (cur(t), GATE_RG // RET_V_W)),
            resident((1, CHUNK, MIX_W)),
            resident((1, RET_QK_W, CHUNK)),
            resident((3, CHUNK, CHUNK)),
            resident((2, D_MODEL)),
            pl.BlockSpec((BACK_ROWS, D_MODEL), lambda t, s: (prev(t), 0)),
            pl.BlockSpec((BACK_ROWS, 2 * D_MODEL), lambda t, s: (prev(t), 0)),
            resident((3, D_MODEL, D_MODEL)),
            resident((D_MODEL, D_FF)),
            resident((D_FF, D_MODEL)),
        ],
        out_specs=pl.BlockSpec((BACK_ROWS, D_MODEL), lambda t, s: (prev(t), 0)),
        scratch_shapes=[
            pltpu.VMEM((KV_SLOTS, SWA_KV_HEADS, PAIR_ROWS, LANES), BF16),
            pltpu.VMEM((KV_SLOTS, SWA_KV_HEADS, CHUNK, LANES), BF16),
            pltpu.VMEM((SWA_KV_HEADS, CHUNK, LANES), BF16),
            pltpu.VMEM((SWA_KV_HEADS, CHUNK, LANES), BF16),
            pltpu.VMEM((RET_HEADS, RET_QK_DIM, RET_V_DIM), F32),
            pltpu.VMEM((BACK_ROWS, SWA_Q_W), BF16),
            pltpu.VMEM((BACK_ROWS, RET_V_W), BF16),
            pltpu.VMEM((BACK_ROWS, D_MODEL), BF16),
        ],
    )
    return pl.pallas_call(
        functools.partial(_back_kernel, blocks_per_batch),
        out_shape=jax.ShapeDtypeStruct((rows, D_MODEL), F32),
        grid_spec=grid_spec,
        compiler_params=pltpu.CompilerParams(
            dimension_semantics=("arbitrary",),
            vmem_limit_bytes=V7X_VMEM_LIMIT_BYTES,
        ),
        name="back",
    )(sinks.astype(F32), mix2d, kdt, gates2d, meta_mix, meta_kdt,
      jnp.asarray(np.concatenate([band_dist, meta_dist[None]])), norm_gains,
      x2d, gates2d, w_branch_out, w_up, w_down)


def kernel(x, meta_tokens, mix_norm_g, w_in, ret_norm_g, q_norm_g, k_norm_g, sinks,
           w_ret_branch, w_swa_branch, w_out, mlp_norm_g, w_up, w_down):
    b, l, d = x.shape
    assert d == D_MODEL and l % BLOCK_ROWS == 0 and mix_norm_g.shape[0] == 1
    mix_g = mix_norm_g[0].reshape(1, d).astype(F32)
    w_in_b = w_in[0].astype(BF16)
    qg2 = jnp.tile(q_norm_g[0].astype(F32) * (SWA_HEAD_DIM ** -0.5), 2).reshape(1, LANES)
    kg2 = jnp.tile(k_norm_g[0].astype(F32), 2).reshape(1, LANES)

    mix, kdt, gates, w_branch_out, w_up_b, w_down_b = _inproj(
        x, mix_g, qg2, kg2, jnp.asarray(_decay_tables(BLOCK_ROWS)), w_in_b, BLOCK_ROWS,
        cast_weights=(w_ret_branch, w_swa_branch, w_out, w_up, w_down))
    meta_chunk = jnp.pad(meta_tokens.astype(F32), ((PAD, 0), (0, 0)))[None]
    meta_mix, meta_kdt, _ = _inproj(meta_chunk, mix_g, qg2, kg2, jnp.asarray(_meta_decay_tables()),
                                    w_in_b, CHUNK)

    norm_gains = jnp.stack([ret_norm_g[0].reshape(RET_V_W), mlp_norm_g[0]]).astype(F32)
    out = _back(
        x.reshape(b * l, d), mix.reshape(b * l, MIX_W), kdt, gates.reshape(b * l, GATE_W),
        meta_mix, meta_kdt, sinks[0], norm_gains, w_branch_out, w_up_b, w_down_b, l // BACK_ROWS)
    return out.reshape(b, l, d)
```

```python
import functools

import numpy as np
import jax
import jax.numpy as jnp
from jax import lax
from jax.experimental import pallas as pl
from jax.experimental.pallas import tpu as pltpu

D_MODEL = 1024
N_META = 16
CHUNK = 128
PAD = CHUNK - N_META
RET_HEADS = 4
RET_QK_DIM = 128
RET_V_DIM = 256
SWA_Q_HEADS = 16
SWA_KV_HEADS = 4
SWA_GROUP = SWA_Q_HEADS // SWA_KV_HEADS
SWA_HEAD_DIM = 64
D_FF = 4 * D_MODEL
EPS = 1e-6
RET_QK_W = RET_HEADS * RET_QK_DIM
RET_V_W = RET_HEADS * RET_V_DIM
SWA_Q_W = SWA_Q_HEADS * SWA_HEAD_DIM
SWA_KV_W = SWA_KV_HEADS * SWA_HEAD_DIM
D_IN = 2 * RET_QK_W + 2 * RET_V_W + SWA_Q_W + 2 * SWA_KV_W + 2 * D_MODEL

OFF_RQ = 0
OFF_RK = OFF_RQ + RET_QK_W
OFF_RV = OFF_RK + RET_QK_W
OFF_RG = OFF_RV + RET_V_W
OFF_SQ = OFF_RG + RET_V_W
OFF_SK = OFF_SQ + SWA_Q_W
OFF_SV = OFF_SK + SWA_KV_W
OFF_GR = OFF_SV + SWA_KV_W
OFF_GS = OFF_GR + D_MODEL

MIX_SQ = 0
MIX_RV = MIX_SQ + SWA_Q_W
MIX_RQ = MIX_RV + RET_V_W
MIX_SK = MIX_RQ + RET_QK_W
MIX_SV = MIX_SK + SWA_KV_W
MIX_W = MIX_SV + SWA_KV_W
GATE_RG = 2 * D_MODEL
GATE_W = GATE_RG + RET_V_W

V7X_VMEM_LIMIT_BYTES = 64 * 1024 * 1024
LANES = 128
assert LANES == 2 * SWA_HEAD_DIM and CHUNK == LANES and RET_V_W == D_MODEL

F32 = jnp.float32
BF16 = jnp.bfloat16

BLOCK_ROWS = 1024
RET_CHUNK = 256
assert BLOCK_ROWS % RET_CHUNK == 0 and BLOCK_ROWS % CHUNK == 0


def _sigmoid(x):
    return 1.0 / (1.0 + jnp.exp(-x))


def _dot(a, b):
    return jnp.dot(a, b, preferred_element_type=F32)


def _dot_nt(a, b):
    return lax.dot_general(a, b, (((1,), (1,)), ((), ())), preferred_element_type=F32)


def _retention_log_gamma():
    return np.log1p(-(2.0 ** (-5.0 - np.arange(RET_HEADS, dtype=np.float64))))


IN_N_CHUNK = 512


def _head_pair_rms(y, g2):
    rows, width = y.shape
    low = lax.broadcasted_iota(jnp.int32, (rows, LANES), 1) < SWA_HEAD_DIM
    out = []
    for c in range(0, width, LANES):
        yb = y[:, c:c + LANES]
        y2 = yb * yb
        ms_lo = jnp.sum(jnp.where(low, y2, 0.0), axis=-1, keepdims=True) * (1.0 / SWA_HEAD_DIM)
        ms_hi = jnp.sum(jnp.where(low, 0.0, y2), axis=-1, keepdims=True) * (1.0 / SWA_HEAD_DIM)
        scale = jnp.where(low, lax.rsqrt(ms_lo + EPS), lax.rsqrt(ms_hi + EPS))
        out.append(yb * scale * g2)
    return jnp.concatenate(out, axis=1) if len(out) > 1 else out[0]


def _inproj_kernel(x_ref, g_ref, qg_ref, kg_ref, dec_ref, w_ref, mix_ref, kdt_ref, gate_ref):
    x = x_ref[...]
    ms = jnp.mean(x * x, axis=-1, keepdims=True)
    hn = (x * lax.rsqrt(ms + EPS) * g_ref[...]).astype(BF16)

    def proj(lo, width):
        return _dot(hn, w_ref[:, lo:lo + width])

    for c in range(0, SWA_Q_W, IN_N_CHUNK):
        mix_ref[:, MIX_SQ + c:MIX_SQ + c + IN_N_CHUNK] = _head_pair_rms(
            proj(OFF_SQ + c, IN_N_CHUNK), qg_ref[...]).astype(BF16)
    for c in range(0, RET_V_W, IN_N_CHUNK):
        mix_ref[:, MIX_RV + c:MIX_RV + c + IN_N_CHUNK] = proj(OFF_RV + c, IN_N_CHUNK).astype(BF16)
    mix_ref[:, MIX_RQ:MIX_RQ + RET_QK_W] = (proj(OFF_RQ, RET_QK_W) * dec_ref[:, 0:RET_QK_W]).astype(BF16)
    kd = proj(OFF_RK, RET_QK_W) * dec_ref[:, RET_QK_W:2 * RET_QK_W]
    kdt_ref[...] = kd.T.astype(BF16)
    mix_ref[:, MIX_SK:MIX_SK + SWA_KV_W] = _head_pair_rms(proj(OFF_SK, SWA_KV_W), kg_ref[...]).astype(BF16)
    mix_ref[:, MIX_SV:MIX_SV + SWA_KV_W] = proj(OFF_SV, SWA_KV_W).astype(BF16)
    for c in range(0, 2 * D_MODEL, IN_N_CHUNK):
        gate_ref[:, c:c + IN_N_CHUNK] = proj(OFF_GR + c, IN_N_CHUNK).astype(BF16)
    for c in range(0, RET_V_W, IN_N_CHUNK):
        gate_ref[:, GATE_RG + c:GATE_RG + c + IN_N_CHUNK] = proj(OFF_RG + c, IN_N_CHUNK).astype(BF16)


def _inproj_cast_kernel(x_ref, g_ref, qg_ref, kg_ref, dec_ref, w_ref,
                        wr_ref, ws_ref, wo_ref, wu_ref, wd_ref,
                        mix_ref, kdt_ref, gate_ref, wbo_out_ref, wu_out_ref, wd_out_ref):
    _inproj_kernel(x_ref, g_ref, qg_ref, kg_ref, dec_ref, w_ref, mix_ref, kdt_ref, gate_ref)
    for k, src in enumerate((wr_ref, ws_ref, wo_ref)):
        wbo_out_ref[k] = src[...].astype(BF16)
    wu_out_ref[...] = wu_ref[...].astype(BF16)
    wd_out_ref[...] = wd_ref[...].astype(BF16)


BF16_ROW_TILE = 16


def _inproj(x3d, g, qg2, kg2, dec, w_bf16, block_rows, cast_weights=None):
    b, seq, _ = x3d.shape
    assert seq % block_rows == 0 and dec.shape == (block_rows, 2 * RET_QK_W)
    steps = seq // block_rows
    n_steps = b * steps
    const = lambda shape: pl.BlockSpec(shape, lambda i: (0, 0))
    out_shape = [jax.ShapeDtypeStruct((b, seq, MIX_W), BF16),
                 jax.ShapeDtypeStruct((b, RET_QK_W, seq), BF16),
                 jax.ShapeDtypeStruct((b, seq, GATE_W), BF16)]
    out_specs = [pl.BlockSpec((None, block_rows, MIX_W), lambda i: (i // steps, i % steps, 0)),
                 pl.BlockSpec((None, RET_QK_W, block_rows), lambda i: (i // steps, 0, i % steps)),
                 pl.BlockSpec((None, block_rows, GATE_W), lambda i: (i // steps, i % steps, 0))]
    extra_in, extra_specs = [], []
    if cast_weights is not None:
        band = lambda w: w.shape[1] // n_steps
        for w in cast_weights:
            assert w.shape[1] % n_steps == 0 and band(w) % BF16_ROW_TILE == 0
            extra_in.append(w)
            extra_specs.append(pl.BlockSpec((None, band(w), w.shape[2]), lambda i: (0, i, 0)))
        w_r, _, _, w_u, w_d = cast_weights
        out_shape += [jax.ShapeDtypeStruct((3,) + w_r.shape[1:], BF16),
                      jax.ShapeDtypeStruct(w_u.shape[1:], BF16),
                      jax.ShapeDtypeStruct(w_d.shape[1:], BF16)]
        out_specs += [pl.BlockSpec((3, band(w_r), w_r.shape[2]), lambda i: (0, i, 0)),
                      pl.BlockSpec((band(w_u), w_u.shape[2]), lambda i: (i, 0)),
                      pl.BlockSpec((band(w_d), w_d.shape[2]), lambda i: (i, 0))]
    return pl.pallas_call(
        _inproj_kernel if cast_weights is None else _inproj_cast_kernel,
        out_shape=tuple(out_shape),
        grid=(n_steps,),
        in_specs=[
            pl.BlockSpec((None, block_rows, D_MODEL), lambda i: (i // steps, i % steps, 0)),
            const((1, D_MODEL)),
            const((1, LANES)),
            const((1, LANES)),
            const((block_rows, 2 * RET_QK_W)),
            pl.BlockSpec((D_MODEL, D_IN), lambda i: (0, 0), pipeline_mode=pl.Buffered(1)),
        ] + extra_specs,
        out_specs=tuple(out_specs),
        compiler_params=pltpu.CompilerParams(
            dimension_semantics=("arbitrary",),
            vmem_limit_bytes=V7X_VMEM_LIMIT_BYTES,
        ),
        name="inproj",
    )(x3d, g, qg2, kg2, dec, w_bf16, *extra_in)


def _decay_tables(block_rows):
    lg = _retention_log_gamma()
    pos = (np.arange(block_rows) % RET_CHUNK).astype(np.float64) + 1.0
    qd = np.exp(lg[None, :] * pos[:, None])
    kd = np.exp(-lg[None, :] * pos[:, None]) * (RET_QK_DIM ** -0.5)
    rep = lambda t: np.repeat(t, RET_QK_DIM, axis=1)
    return np.concatenate([rep(qd), rep(kd)], axis=1).astype(np.float32)


def _meta_decay_tables():
    lg = _retention_log_gamma()
    pos = np.arange(CHUNK, dtype=np.float64)
    kd = np.exp(lg[None, :] * (CHUNK - 1.0 - pos[:, None])) * (RET_QK_DIM ** -0.5)
    rep = lambda t: np.repeat(t, RET_QK_DIM, axis=1)
    return np.concatenate([np.ones((CHUNK, RET_QK_W)), rep(kd)], axis=1).astype(np.float32)


RET_CHUNK_DECAY = [float(c) for c in np.exp(_retention_log_gamma() * RET_CHUNK)]
PAIR_ROWS = 2 * CHUNK
BACK_ROWS = 512
DENSE_ROWS = 512
assert BACK_ROWS % RET_CHUNK == 0 and BLOCK_ROWS % BACK_ROWS == 0 and BACK_ROWS % DENSE_ROWS == 0
CHUNKS_PER_BLOCK = BACK_ROWS // CHUNK
RET_PER_BLOCK = BACK_ROWS // RET_CHUNK
KV_SLOTS = CHUNKS_PER_BLOCK + 1
FF_CHUNK = 1024


def _swa_consts():
    slopes = 2.0 ** (-8.0 * np.arange(1, SWA_Q_HEADS + 1, dtype=np.float64) / SWA_Q_HEADS)
    slopes = slopes.astype(np.float32)
    i = np.arange(CHUNK)[:, None]
    j = np.arange(CHUNK)[None, :]
    band_dist = np.where(j <= i, i - j, i - j + CHUNK).astype(np.float32)
    meta_dist = np.where(j < 2 * N_META, i - (j % N_META) + N_META, np.inf).astype(np.float32)
    band_dist = np.stack([np.where(j <= i, band_dist, np.inf), band_dist]).astype(np.float32)
    return [float(s) for s in slopes], band_dist, meta_dist


ALIBI_SLOPES = _swa_consts()[0]


def _back_kernel(blocks_per_batch,
                 sink_ref,
                 mix_ref, kdt_ref, rg_ref, meta_ref, mkdt_ref, dist_ref, gains_ref,
                 x_ref, gate_ref, wbo_ref, wu_ref, wd_ref,
                 o_ref,
                 bd_ref, vbuf_ref, bdm_ref, vm_ref, state_ref, ys_ref, yr_ref, hn_ref):
    sq_ref = mix_ref.at[:, MIX_SQ:MIX_SQ + SWA_Q_W]
    rv_ref = mix_ref.at[:, MIX_RV:MIX_RV + RET_V_W]
    qd_ref = mix_ref.at[:, MIX_RQ:MIX_RQ + RET_QK_W]
    sk_ref = mix_ref.at[:, MIX_SK:MIX_SK + SWA_KV_W]
    sv_ref = mix_ref.at[:, MIX_SV:MIX_SV + SWA_KV_W]
    mk_ref = meta_ref.at[0, PAD:CHUNK, MIX_SK:MIX_SK + SWA_KV_W]
    mv_ref = meta_ref.at[0, PAD:CHUNK, MIX_SV:MIX_SV + SWA_KV_W]
    mrv_ref = meta_ref.at[0, :, MIX_RV:MIX_RV + RET_V_W]
    mkdt_ref = mkdt_ref.at[0]
    band_dist_ref = dist_ref.at[0:2]
    meta_dist_ref = dist_ref.at[2]
    rn_ref = gains_ref.at[0:1, :]
    g_ref = gains_ref.at[1:2, :]
    gr_ref = gate_ref.at[:, 0:D_MODEL]
    gs_ref = gate_ref.at[:, D_MODEL:2 * D_MODEL]
    wr_ref, ws_ref, wo_ref = wbo_ref.at[0], wbo_ref.at[1], wbo_ref.at[2]
    t = pl.program_id(0)
    block = jnp.minimum(t, pl.num_programs(0) - 2)
    block_in_batch = block % blocks_per_batch
    parity = t & 1
    last_slot = CHUNKS_PER_BLOCK - 1 + parity
    carried_slot = CHUNKS_PER_BLOCK - parity
    zeros_head = jnp.zeros((CHUNK, SWA_HEAD_DIM), BF16)

    @pl.when(t == 0)
    def _():
        ys_ref[...] = jnp.zeros_like(ys_ref)
        yr_ref[...] = jnp.zeros_like(yr_ref)

    @pl.when(block_in_batch == 0)
    def _():
        bd_ref[...] = jnp.zeros_like(bd_ref)
        vbuf_ref[...] = jnp.zeros_like(vbuf_ref)
        bdm_ref[...] = jnp.zeros_like(bdm_ref)
        vm_ref[...] = jnp.zeros_like(vm_ref)
        zm = jnp.zeros((N_META, SWA_HEAD_DIM), BF16)
        for g in range(SWA_KV_HEADS):
            cols = slice(g * SWA_HEAD_DIM, (g + 1) * SWA_HEAD_DIM)
            km = mk_ref[:, cols]
            bdm_ref[g, 0:N_META, :] = jnp.concatenate([km, zm], axis=1)
            bdm_ref[g, N_META:2 * N_META, :] = jnp.concatenate([zm, km], axis=1)
            vmeta = mv_ref[:, cols]
            vm_ref[g, 0:N_META, :] = jnp.concatenate([vmeta, zm], axis=1)
            vm_ref[g, N_META:2 * N_META, :] = jnp.concatenate([zm, vmeta], axis=1)
        for h in range(RET_HEADS):
            state_ref[h] = _dot(mkdt_ref[h * RET_QK_DIM:(h + 1) * RET_QK_DIM, :],
                                mrv_ref[:, h * RET_V_DIM:(h + 1) * RET_V_DIM])

    slabs = [slice(r, r + DENSE_ROWS) for r in range(0, BACK_ROWS, DENSE_ROWS)]
    for rs in slabs:
        merged = (_sigmoid(gr_ref[rs, :].astype(F32)) * _dot(yr_ref[rs, :], wr_ref[...])
                  + _sigmoid(gs_ref[rs, :].astype(F32)) * _dot(ys_ref[rs, :], ws_ref[...]))
        o_ref[rs, :] = x_ref[rs, :] + _dot(merged.astype(BF16), wo_ref[...])
        h1 = o_ref[rs, :]
        hn_ref[rs, :] = (h1 * lax.rsqrt(jnp.mean(h1 * h1, axis=-1, keepdims=True) + EPS)
                         * g_ref[...]).astype(BF16)

    ret_row = lax.broadcasted_iota(jnp.int32, (RET_CHUNK, RET_CHUNK), 0)
    ret_col = lax.broadcasted_iota(jnp.int32, (RET_CHUNK, RET_CHUNK), 1)
    ret_causal = ret_col <= ret_row

    def retention_head(sub, h):
        rows = slice(sub * RET_CHUNK, (sub + 1) * RET_CHUNK)
        vcols = slice(h * RET_V_DIM, (h + 1) * RET_V_DIM)
        qd = qd_ref[rows, h * RET_QK_DIM:(h + 1) * RET_QK_DIM]
        kdt = kdt_ref[h * RET_QK_DIM:(h + 1) * RET_QK_DIM, rows]
        v = rv_ref[rows, vcols]
        state = state_ref[h]
        s = jnp.where(ret_causal, _dot(qd, kdt), 0.0).astype(BF16)
        o = _dot(s, v) + _dot(qd, state.astype(BF16))
        state_ref[h] = RET_CHUNK_DECAY[h] * (state + _dot(kdt, v))
        ms = jnp.mean(o * o, axis=-1, keepdims=True)
        gate = rg_ref[rows, vcols].astype(F32)
        yr_ref[rows, vcols] = (o * lax.rsqrt(ms + EPS) * rn_ref[:, vcols]
                               * (gate * _sigmoid(gate))).astype(BF16)

    row = lax.broadcasted_iota(jnp.int32, (CHUNK, CHUNK), 0)
    col = lax.broadcasted_iota(jnp.int32, (CHUNK, CHUNK), 1)
    own = col <= row
    own2 = jnp.concatenate([own, own], axis=1)
    own4 = jnp.concatenate([own2, own2], axis=0)
    meta_seg = [col < N_META, (col >= N_META) & (col < 2 * N_META)]
    low_half = col < SWA_HEAD_DIM
    neg_inf = jnp.float32(-jnp.inf)

    first_chunk = (CHUNKS_PER_BLOCK * block_in_batch).astype(F32)
    chunks = []
    for c in range(CHUNKS_PER_BLOCK):
        slot = c if c < CHUNKS_PER_BLOCK - 1 else last_slot
        prev_slot = c - 1 if c > 0 else carried_slot
        bias_sel = jnp.minimum(block_in_batch, 1) if c == 0 else 1
        chunks.append((slice(c * CHUNK, (c + 1) * CHUNK), slot, prev_slot, bias_sel, first_chunk + float(c)))

    for rows, slot, _, _, _ in chunks:
        for g in range(SWA_KV_HEADS):
            kg = sk_ref[rows, g * SWA_HEAD_DIM:(g + 1) * SWA_HEAD_DIM]
            bd_ref[slot, g, 0:CHUNK, :] = jnp.concatenate([kg, zeros_head], axis=1)
            bd_ref[slot, g, CHUNK:2 * CHUNK, :] = jnp.concatenate([zeros_head, kg], axis=1)
            vg = sv_ref[rows, g * SWA_HEAD_DIM:(g + 1) * SWA_HEAD_DIM]
            vbuf_ref[slot, g] = jnp.concatenate([vg, vg], axis=1)

    def scores(chunk, g):
        rows, slot, prev_slot, bias_sel, chunk_idx = chunk
        q2 = jnp.concatenate(
            [sq_ref[rows, (2 * g + p) * LANES:(2 * g + p + 1) * LANES] for p in range(2)],
            axis=0)
        s_own = _dot_nt(q2, bd_ref[slot, g])
        s_prev = _dot_nt(q2, bd_ref[prev_slot, g])
        dist = band_dist_ref[bias_sel]
        bias = jnp.concatenate(
            [jnp.concatenate([ALIBI_SLOPES[g * SWA_GROUP + 2 * p + e] * dist for e in range(2)], axis=1)
             for p in range(2)], axis=0)
        s = jnp.where(own4, s_own, s_prev) - bias
        dist_meta = meta_dist_ref[...] + float(CHUNK) * chunk_idx
        bias_meta = jnp.concatenate(
            [jnp.where(meta_seg[0], ALIBI_SLOPES[g * SWA_GROUP + 2 * p],
                       ALIBI_SLOPES[g * SWA_GROUP + 2 * p + 1]) * dist_meta for p in range(2)], axis=0)
        s_meta = _dot_nt(q2, bdm_ref[g]) - bias_meta
        return s, s_meta

    def softmax(g, s, s_meta):
        lhs = []
        lhs_meta = []
        inv = []
        for p in range(2):
            sm_p = s_meta[p * CHUNK:(p + 1) * CHUNK]
            pm_pair = None
            for e in range(2):
                h = g * SWA_GROUP + 2 * p + e
                sink = sink_ref[h]
                sb = s[p * CHUNK:(p + 1) * CHUNK, e * CHUNK:(e + 1) * CHUNK]
                sm = jnp.where(meta_seg[e], sm_p, neg_inf)
                m = jnp.maximum(jnp.max(jnp.maximum(sb, sm), axis=-1, keepdims=True), sink)
                pb = jnp.exp(sb - m)
                pm = jnp.exp(sm - m)
                denom = jnp.sum(pb + pm, axis=-1, keepdims=True) + jnp.exp(sink - m)
                inv.append(1.0 / denom)
                lhs.append(jnp.concatenate(
                    [jnp.where(own, pb, 0.0).astype(BF16), jnp.where(own, 0.0, pb).astype(BF16)], axis=1))
                pm_pair = pm if pm_pair is None else pm_pair + pm
            lhs_meta.append(pm_pair.astype(BF16))
        return jnp.concatenate(lhs, axis=0), jnp.concatenate(lhs_meta, axis=0), inv

    def weighted_values(chunk, g, lhs, lhs_meta, inv):
        rows, slot, prev_slot, _, _ = chunk
        vv = jnp.concatenate([vbuf_ref[slot, g], vbuf_ref[prev_slot, g]], axis=0)
        acc = _dot(lhs, vv)
        acc_meta = _dot(lhs_meta, vm_ref[g])
        for p in range(2):
            first, second = 2 * p, 2 * p + 1
            pair = jnp.where(low_half, acc[first * CHUNK:(first + 1) * CHUNK],
                             acc[second * CHUNK:(second + 1) * CHUNK])
            scale = jnp.where(low_half, inv[first], inv[second])
            tile = (pair + acc_meta[p * CHUNK:(p + 1) * CHUNK]) * scale
            ys_ref[rows, (2 * g + p) * LANES:(2 * g + p + 1) * LANES] = tile.astype(BF16)

    tasks = [(chunk, g) for chunk in chunks for g in range(SWA_KV_HEADS)]
    ret_tasks = [(sub, h) for sub in range(RET_PER_BLOCK) for h in range(RET_HEADS)]
    assert len(tasks) == 2 * len(ret_tasks)

    def mlp_phase(k):
        c = k * FF_CHUNK
        for rs in slabs:
            u = jnp.maximum(_dot(hn_ref[rs, :], wu_ref[:, c:c + FF_CHUNK]), 0.0)
            o_ref[rs, :] += _dot((u * u).astype(BF16), wd_ref[c:c + FF_CHUNK, :])

    n_mlp_phases = D_FF // FF_CHUNK
    phase_at_task = {i * len(tasks) // n_mlp_phases: i for i in range(n_mlp_phases)}

    pending_scores = scores(*tasks[0])
    pending_probs = None
    for i, (chunk, g) in enumerate(tasks):
        current_scores = pending_scores
        if i + 1 < len(tasks):
            pending_scores = scores(*tasks[i + 1])
        if i in phase_at_task:
            mlp_phase(phase_at_task[i])
        probs = softmax(g, *current_scores)
        if i % 2 == 0:
            retention_head(*ret_tasks[i // 2])
        if pending_probs is not None:
            weighted_values(*tasks[i - 1], *pending_probs)
        pending_probs = probs
    weighted_values(*tasks[-1], *pending_probs)


def _back(x2d, mix2d, kdt, gates2d, meta_mix, meta_kdt, sinks, norm_gains, w_branch_out, w_up, w_down,
          blocks_per_batch):
    rows = x2d.shape[0]
    assert rows % BACK_ROWS == 0
    n_blocks = rows // BACK_ROWS
    _, band_dist, meta_dist = _swa_consts()

    cur = lambda t: jnp.minimum(t, n_blocks - 1)
    prev = lambda t: jnp.maximum(t - 1, 0)

    def resident(shape):
        zeros = (0,) * len(shape)
        return pl.BlockSpec(shape, lambda t, s: zeros, pipeline_mode=pl.Buffered(1))

    grid_spec = pltpu.PrefetchScalarGridSpec(
        num_scalar_prefetch=1,
        grid=(n_blocks + 1,),
        in_specs=[
            pl.BlockSpec((BACK_ROWS, MIX_W), lambda t, s: (cur(t), 0)),
            pl.BlockSpec((None, RET_QK_W, BACK_ROWS),
                         lambda t, s: (cur(t) // blocks_per_batch, 0, cur(t) % blocks_per_batch)),
            pl.BlockSpec((BACK_ROWS, RET_V_W), lambda t, s: (cur(t), GATE_RG // RET_V_W)),
            resident((1, CHUNK, MIX_W)),
            resident((1, RET_QK_W, CHUNK)),
            resident((3, CHUNK, CHUNK)),
            resident((2, D_MODEL)),
            pl.BlockSpec((BACK_ROWS, D_MODEL), lambda t, s: (prev(t), 0)),
            pl.BlockSpec((BACK_ROWS, 2 * D_MODEL), lambda t, s: (prev(t), 0)),
            resident((3, D_MODEL, D_MODEL)),
            resident((D_MODEL, D_FF)),
            resident((D_FF, D_MODEL)),
        ],
        out_specs=pl.BlockSpec((BACK_ROWS, D_MODEL), lambda t, s: (prev(t), 0)),
        scratch_shapes=[
            pltpu.VMEM((KV_SLOTS, SWA_KV_HEADS, PAIR_ROWS, LANES), BF16),
            pltpu.VMEM((KV_SLOTS, SWA_KV_HEADS, CHUNK, LANES), BF16),
            pltpu.VMEM((SWA_KV_HEADS, CHUNK, LANES), BF16),
            pltpu.VMEM((SWA_KV_HEADS, CHUNK, LANES), BF16),
            pltpu.VMEM((RET_HEADS, RET_QK_DIM, RET_V_DIM), F32),
            pltpu.VMEM((BACK_ROWS, SWA_Q_W), BF16),
            pltpu.VMEM((BACK_ROWS, RET_V_W), BF16),
            pltpu.VMEM((BACK_ROWS, D_MODEL), BF16),
        ],
    )
    return pl.pallas_call(
        functools.partial(_back_kernel, blocks_per_batch),
        out_shape=jax.ShapeDtypeStruct((rows, D_MODEL), F32),
        grid_spec=grid_spec,
        compiler_params=pltpu.CompilerParams(
            dimension_semantics=("arbitrary",),
            vmem_limit_bytes=V7X_VMEM_LIMIT_BYTES,
        ),
        name="back",
    )(sinks.astype(F32), mix2d, kdt, gates2d, meta_mix, meta_kdt,
      jnp.asarray(np.concatenate([band_dist, meta_dist[None]])), norm_gains,
      x2d, gates2d, w_branch_out, w_up, w_down)


def kernel(x, meta_tokens, mix_norm_g, w_in, ret_norm_g, q_norm_g, k_norm_g, sinks,
           w_ret_branch, w_swa_branch, w_out, mlp_norm_g, w_up, w_down):
    b, l, d = x.shape
    assert d == D_MODEL and l % BLOCK_ROWS == 0 and mix_norm_g.shape[0] == 1
    mix_g = mix_norm_g[0].reshape(1, d).astype(F32)
    w_in_b = w_in[0].astype(BF16)
    qg2 = jnp.tile(q_norm_g[0].astype(F32) * (SWA_HEAD_DIM ** -0.5), 2).reshape(1, LANES)
    kg2 = jnp.tile(k_norm_g[0].astype(F32), 2).reshape(1, LANES)

    mix, kdt, gates, w_branch_out, w_up_b, w_down_b = _inproj(
        x, mix_g, qg2, kg2, jnp.asarray(_decay_tables(BLOCK_ROWS)), w_in_b, BLOCK_ROWS,
        cast_weights=(w_ret_branch, w_swa_branch, w_out, w_up, w_down))
    meta_chunk = jnp.pad(meta_tokens.astype(F32), ((PAD, 0), (0, 0)))[None]
    meta_mix, meta_kdt, _ = _inproj(meta_chunk, mix_g, qg2, kg2, jnp.asarray(_meta_decay_tables()),
                                    w_in_b, CHUNK)

    norm_gains = jnp.stack([ret_norm_g[0].reshape(RET_V_W), mlp_norm_g[0]]).astype(F32)
    out = _back(
        x.reshape(b * l, d), mix.reshape(b * l, MIX_W), kdt, gates.reshape(b * l, GATE_W),
        meta_mix, meta_kdt, sinks[0], norm_gains, w_branch_out, w_up_b, w_down_b, l // BACK_ROWS)
    return out.reshape(b, l, d)
```

```python
import functools

import numpy as np
import jax
import jax.numpy as jnp
from jax import lax
from jax.experimental import pallas as pl
from jax.experimental.pallas import tpu as pltpu

D_MODEL = 1024
N_META = 16
CHUNK = 128
PAD = CHUNK - N_META
RET_HEADS = 4
RET_QK_DIM = 128
RET_V_DIM = 256
SWA_Q_HEADS = 16
SWA_KV_HEADS = 4
SWA_GROUP = SWA_Q_HEADS // SWA_KV_HEADS
SWA_HEAD_DIM = 64
D_FF = 4 * D_MODEL
EPS = 1e-6
RET_QK_W = RET_HEADS * RET_QK_DIM
RET_V_W = RET_HEADS * RET_V_DIM
SWA_Q_W = SWA_Q_HEADS * SWA_HEAD_DIM
SWA_KV_W = SWA_KV_HEADS * SWA_HEAD_DIM
D_IN = 2 * RET_QK_W + 2 * RET_V_W + SWA_Q_W + 2 * SWA_KV_W + 2 * D_MODEL

OFF_RQ = 0
OFF_RK = OFF_RQ + RET_QK_W
OFF_RV = OFF_RK + RET_QK_W
OFF_RG = OFF_RV + RET_V_W
OFF_SQ = OFF_RG + RET_V_W
OFF_SK = OFF_SQ + SWA_Q_W
OFF_SV = OFF_SK + SWA_KV_W
OFF_GR = OFF_SV + SWA_KV_W
OFF_GS = OFF_GR + D_MODEL

MIX_SQ = 0
MIX_RV = MIX_SQ + SWA_Q_W
MIX_RQ = MIX_RV + RET_V_W
MIX_SV = MIX_RQ + RET_QK_W
MIX_W = MIX_SV + SWA_KV_W
KT_SK = RET_QK_W
KT_W = KT_SK + SWA_KV_W
GATE_RG = 2 * D_MODEL
GATE_W = GATE_RG + RET_V_W

V7X_VMEM_LIMIT_BYTES = 64 * 1024 * 1024
LANES = 128
assert LANES == 2 * SWA_HEAD_DIM and CHUNK == LANES and RET_V_W == D_MODEL

F32 = jnp.float32
BF16 = jnp.bfloat16

BLOCK_ROWS = 1024
RET_CHUNK = 256
assert BLOCK_ROWS % RET_CHUNK == 0 and BLOCK_ROWS % CHUNK == 0


def _sigmoid(x):
    return 1.0 / (1.0 + jnp.exp(-x))


def _dot(a, b):
    return jnp.dot(a, b, preferred_element_type=F32)


def _retention_log_gamma():
    return np.log1p(-(2.0 ** (-5.0 - np.arange(RET_HEADS, dtype=np.float64))))


IN_N_CHUNK = 512


def _head_pair_rms(y, g2):
    rows, width = y.shape
    low = lax.broadcasted_iota(jnp.int32, (rows, LANES), 1) < SWA_HEAD_DIM
    out = []
    for c in range(0, width, LANES):
        yb = y[:, c:c + LANES]
        y2 = yb * yb
        ms_lo = jnp.sum(jnp.where(low, y2, 0.0), axis=-1, keepdims=True) * (1.0 / SWA_HEAD_DIM)
        ms_hi = jnp.sum(jnp.where(low, 0.0, y2), axis=-1, keepdims=True) * (1.0 / SWA_HEAD_DIM)
        scale = jnp.where(low, lax.rsqrt(ms_lo + EPS), lax.rsqrt(ms_hi + EPS))
        out.append(yb * scale * g2)
    return jnp.concatenate(out, axis=1) if len(out) > 1 else out[0]


def _inproj_kernel(x_ref, g_ref, qg_ref, kg_ref, dec_ref, w_ref, mix_ref, kt_ref, gate_ref):
    x = x_ref[...]
    ms = jnp.mean(x * x, axis=-1, keepdims=True)
    hn = (x * lax.rsqrt(ms + EPS) * g_ref[...]).astype(BF16)

    def proj(lo, width):
        return _dot(hn, w_ref[:, lo:lo + width])

    for c in range(0, SWA_Q_W, IN_N_CHUNK):
        mix_ref[:, MIX_SQ + c:MIX_SQ + c + IN_N_CHUNK] = _head_pair_rms(
            proj(OFF_SQ + c, IN_N_CHUNK), qg_ref[...]).astype(BF16)
    for c in range(0, RET_V_W, IN_N_CHUNK):
        mix_ref[:, MIX_RV + c:MIX_RV + c + IN_N_CHUNK] = proj(OFF_RV + c, IN_N_CHUNK).astype(BF16)
    mix_ref[:, MIX_RQ:MIX_RQ + RET_QK_W] = (proj(OFF_RQ, RET_QK_W) * dec_ref[:, 0:RET_QK_W]).astype(BF16)
    kd = proj(OFF_RK, RET_QK_W) * dec_ref[:, RET_QK_W:2 * RET_QK_W]
    kt_ref[0:KT_SK, :] = kd.T.astype(BF16)
    kt_ref[KT_SK:KT_W, :] = _head_pair_rms(proj(OFF_SK, SWA_KV_W), kg_ref[...]).T.astype(BF16)
    mix_ref[:, MIX_SV:MIX_SV + SWA_KV_W] = proj(OFF_SV, SWA_KV_W).astype(BF16)
    for c in range(0, 2 * D_MODEL, IN_N_CHUNK):
        gate_ref[:, c:c + IN_N_CHUNK] = proj(OFF_GR + c, IN_N_CHUNK).astype(BF16)
    for c in range(0, RET_V_W, IN_N_CHUNK):
        gate_ref[:, GATE_RG + c:GATE_RG + c + IN_N_CHUNK] = proj(OFF_RG + c, IN_N_CHUNK).astype(BF16)


def _inproj_cast_kernel(x_ref, g_ref, qg_ref, kg_ref, dec_ref, w_ref,
                        wr_ref, ws_ref, wo_ref, wu_ref, wd_ref,
                        mix_ref, kt_ref, gate_ref, wbo_out_ref, wu_out_ref, wd_out_ref):
    _inproj_kernel(x_ref, g_ref, qg_ref, kg_ref, dec_ref, w_ref, mix_ref, kt_ref, gate_ref)
    for k, src in enumerate((wr_ref, ws_ref, wo_ref)):
        wbo_out_ref[k] = src[...].astype(BF16)
    wu_out_ref[...] = wu_ref[...].astype(BF16)
    wd_out_ref[...] = wd_ref[...].astype(BF16)


BF16_ROW_TILE = 16


def _inproj(x3d, g, qg2, kg2, dec, w_bf16, block_rows, cast_weights=None):
    b, seq, _ = x3d.shape
    assert seq % block_rows == 0 and dec.shape == (block_rows, 2 * RET_QK_W)
    steps = seq // block_rows
    n_steps = b * steps
    const = lambda shape: pl.BlockSpec(shape, lambda i: (0, 0))
    out_shape = [jax.ShapeDtypeStruct((b, seq, MIX_W), BF16),
                 jax.ShapeDtypeStruct((b, KT_W, seq), BF16),
                 jax.ShapeDtypeStruct((b, seq, GATE_W), BF16)]
    out_specs = [pl.BlockSpec((None, block_rows, MIX_W), lambda i: (i // steps, i % steps, 0)),
                 pl.BlockSpec((None, KT_W, block_rows), lambda i: (i // steps, 0, i % steps)),
                 pl.BlockSpec((None, block_rows, GATE_W), lambda i: (i // steps, i % steps, 0))]
    extra_in, extra_specs = [], []
    if cast_weights is not None:
        band = lambda w: w.shape[1] // n_steps
        for w in cast_weights:
            assert w.shape[1] % n_steps == 0 and band(w) % BF16_ROW_TILE == 0
            extra_in.append(w)
            extra_specs.append(pl.BlockSpec((None, band(w), w.shape[2]), lambda i: (0, i, 0)))
        w_r, _, _, w_u, w_d = cast_weights
        out_shape += [jax.ShapeDtypeStruct((3,) + w_r.shape[1:], BF16),
                      jax.ShapeDtypeStruct(w_u.shape[1:], BF16),
                      jax.ShapeDtypeStruct(w_d.shape[1:], BF16)]
        out_specs += [pl.BlockSpec((3, band(w_r), w_r.shape[2]), lambda i: (0, i, 0)),
                      pl.BlockSpec((band(w_u), w_u.shape[2]), lambda i: (i, 0)),
                      pl.BlockSpec((band(w_d), w_d.shape[2]), lambda i: (i, 0))]
    return pl.pallas_call(
        _inproj_kernel if cast_weights is None else _inproj_cast_kernel,
        out_shape=tuple(out_shape),
        grid=(n_steps,),
        in_specs=[
            pl.BlockSpec((None, block_rows, D_MODEL), lambda i: (i // steps, i % steps, 0)),
            const((1, D_MODEL)),
            const((1, LANES)),
            const((1, LANES)),
            const((block_rows, 2 * RET_QK_W)),
            pl.BlockSpec((D_MODEL, D_IN), lambda i: (0, 0), pipeline_mode=pl.Buffered(1)),
        ] + extra_specs,
        out_specs=tuple(out_specs),
        compiler_params=pltpu.CompilerParams(
            dimension_semantics=("arbitrary",),
            vmem_limit_bytes=V7X_VMEM_LIMIT_BYTES,
        ),
        name="inproj",
    )(x3d, g, qg2, kg2, dec, w_bf16, *extra_in)


def _decay_tables(block_rows):
    lg = _retention_log_gamma()
    pos = (np.arange(block_rows) % RET_CHUNK).astype(np.float64) + 1.0
    qd = np.exp(lg[None, :] * pos[:, None])
    kd = np.exp(-lg[None, :] * pos[:, None]) * (RET_QK_DIM ** -0.5)
    rep = lambda t: np.repeat(t, RET_QK_DIM, axis=1)
    return np.concatenate([rep(qd), rep(kd)], axis=1).astype(np.float32)


def _meta_decay_tables():
    lg = _retention_log_gamma()
    pos = np.arange(CHUNK, dtype=np.float64)
    kd = np.exp(lg[None, :] * (CHUNK - 1.0 - pos[:, None])) * (RET_QK_DIM ** -0.5)
    rep = lambda t: np.repeat(t, RET_QK_DIM, axis=1)
    return np.concatenate([np.ones((CHUNK, RET_QK_W)), rep(kd)], axis=1).astype(np.float32)


RET_CHUNK_DECAY = [float(c) for c in np.exp(_retention_log_gamma() * RET_CHUNK)]
PAIR_ROWS = 2 * CHUNK
BACK_ROWS = 512
DENSE_ROWS = 512
assert BACK_ROWS % RET_CHUNK == 0 and BLOCK_ROWS % BACK_ROWS == 0 and BACK_ROWS % DENSE_ROWS == 0
CHUNKS_PER_BLOCK = BACK_ROWS // CHUNK
RET_PER_BLOCK = BACK_ROWS // RET_CHUNK
KV_SLOTS = CHUNKS_PER_BLOCK + 1
FF_CHUNK = 1024


def _swa_consts():
    slopes = 2.0 ** (-8.0 * np.arange(1, SWA_Q_HEADS + 1, dtype=np.float64) / SWA_Q_HEADS)
    slopes = slopes.astype(np.float32)
    i = np.arange(CHUNK)[:, None]
    j = np.arange(CHUNK)[None, :]
    band_dist = np.where(j <= i, i - j, i - j + CHUNK).astype(np.float32)
    meta_dist = np.where(j < 2 * N_META, i - (j % N_META) + N_META, np.inf).astype(np.float32)
    band_dist = np.stack([np.where(j <= i, band_dist, np.inf), band_dist]).astype(np.float32)
    return [float(s) for s in slopes], band_dist, meta_dist


ALIBI_SLOPES = _swa_consts()[0]


def _back_kernel(blocks_per_batch,
                 sink_ref,
                 mix_ref, kt_ref, rg_ref, meta_ref, mkt_ref, dist_ref, gains_ref,
                 x_ref, gate_ref, wbo_ref, wu_ref, wd_ref,
                 o_ref,
                 bd_ref, vbuf_ref, bdm_ref, vm_ref, state_ref, ys_ref, yr_ref, hn_ref):
    sq_ref = mix_ref.at[:, MIX_SQ:MIX_SQ + SWA_Q_W]
    rv_ref = mix_ref.at[:, MIX_RV:MIX_RV + RET_V_W]
    qd_ref = mix_ref.at[:, MIX_RQ:MIX_RQ + RET_QK_W]
    sv_ref = mix_ref.at[:, MIX_SV:MIX_SV + SWA_KV_W]
    kdt_ref = kt_ref.at[0:KT_SK, :]
    skt_ref = kt_ref.at[KT_SK:KT_W, :]
    mv_ref = meta_ref.at[0, PAD:CHUNK, MIX_SV:MIX_SV + SWA_KV_W]
    mrv_ref = meta_ref.at[0, :, MIX_RV:MIX_RV + RET_V_W]
    mkdt_ref = mkt_ref.at[0, 0:KT_SK, :]
    mskt_ref = mkt_ref.at[0, KT_SK:KT_W, :]
    band_dist_ref = dist_ref.at[0:2]
    meta_dist_ref = dist_ref.at[2]
    rn_ref = gains_ref.at[0:1, :]
    g_ref = gains_ref.at[1:2, :]
    gr_ref = gate_ref.at[:, 0:D_MODEL]
    gs_ref = gate_ref.at[:, D_MODEL:2 * D_MODEL]
    wr_ref, ws_ref, wo_ref = wbo_ref.at[0], wbo_ref.at[1], wbo_ref.at[2]
    t = pl.program_id(0)
    block = jnp.minimum(t, pl.num_programs(0) - 2)
    block_in_batch = block % blocks_per_batch
    parity = t & 1
    last_slot = CHUNKS_PER_BLOCK - 1 + parity
    carried_slot = CHUNKS_PER_BLOCK - parity

    @pl.when(t == 0)
    def _():
        ys_ref[...] = jnp.zeros_like(ys_ref)
        yr_ref[...] = jnp.zeros_like(yr_ref)

    @pl.when(block_in_batch == 0)
    def _():
        bd_ref[...] = jnp.zeros_like(bd_ref)
        vbuf_ref[...] = jnp.zeros_like(vbuf_ref)
        bdm_ref[...] = jnp.zeros_like(bdm_ref)
        vm_ref[...] = jnp.zeros_like(vm_ref)
        zm = jnp.zeros((N_META, SWA_HEAD_DIM), BF16)
        for g in range(SWA_KV_HEADS):
            cols = slice(g * SWA_HEAD_DIM, (g + 1) * SWA_HEAD_DIM)
            kmt = mskt_ref[cols, :][:, PAD:CHUNK]
            z = lambda width: jnp.zeros((SWA_HEAD_DIM, width), BF16)
            bdm_ref[g, 0:SWA_HEAD_DIM, :] = jnp.concatenate([kmt, z(CHUNK - N_META)], axis=1)
            bdm_ref[g, SWA_HEAD_DIM:2 * SWA_HEAD_DIM, :] = jnp.concatenate(
                [z(N_META), kmt, z(CHUNK - 2 * N_META)], axis=1)
            vmeta = mv_ref[:, cols]
            vm_ref[g, 0:N_META, :] = jnp.concatenate([vmeta, zm], axis=1)
            vm_ref[g, N_META:2 * N_META, :] = jnp.concatenate([zm, vmeta], axis=1)
        for h in range(RET_HEADS):
            state_ref[h] = _dot(mkdt_ref[h * RET_QK_DIM:(h + 1) * RET_QK_DIM, :],
                                mrv_ref[:, h * RET_V_DIM:(h + 1) * RET_V_DIM])

    slabs = [slice(r, r + DENSE_ROWS) for r in range(0, BACK_ROWS, DENSE_ROWS)]
    for rs in slabs:
        merged = (_sigmoid(gr_ref[rs, :].astype(F32)) * _dot(yr_ref[rs, :], wr_ref[...])
                  + _sigmoid(gs_ref[rs, :].astype(F32)) * _dot(ys_ref[rs, :], ws_ref[...]))
        o_ref[rs, :] = x_ref[rs, :] + _dot(merged.astype(BF16), wo_ref[...])
        h1 = o_ref[rs, :]
        hn_ref[rs, :] = (h1 * lax.rsqrt(jnp.mean(h1 * h1, axis=-1, keepdims=True) + EPS)
                         * g_ref[...]).astype(BF16)

    ret_row = lax.broadcasted_iota(jnp.int32, (RET_CHUNK, RET_CHUNK), 0)
    ret_col = lax.broadcasted_iota(jnp.int32, (RET_CHUNK, RET_CHUNK), 1)
    ret_causal = ret_col <= ret_row

    def retention_head(sub, h):
        rows = slice(sub * RET_CHUNK, (sub + 1) * RET_CHUNK)
        vcols = slice(h * RET_V_DIM, (h + 1) * RET_V_DIM)
        qd = qd_ref[rows, h * RET_QK_DIM:(h + 1) * RET_QK_DIM]
        kdt = kdt_ref[h * RET_QK_DIM:(h + 1) * RET_QK_DIM, rows]
        v = rv_ref[rows, vcols]
        state = state_ref[h]
        s = jnp.where(ret_causal, _dot(qd, kdt), 0.0).astype(BF16)
        o = _dot(s, v) + _dot(qd, state.astype(BF16))
        state_ref[h] = RET_CHUNK_DECAY[h] * (state + _dot(kdt, v))
        ms = jnp.mean(o * o, axis=-1, keepdims=True)
        gate = rg_ref[rows, vcols].astype(F32)
        yr_ref[rows, vcols] = (o * lax.rsqrt(ms + EPS) * rn_ref[:, vcols]
                               * (gate * _sigmoid(gate))).astype(BF16)

    row = lax.broadcasted_iota(jnp.int32, (CHUNK, CHUNK), 0)
    col = lax.broadcasted_iota(jnp.int32, (CHUNK, CHUNK), 1)
    own = col <= row
    own2 = jnp.concatenate([own, own], axis=1)
    own4 = jnp.concatenate([own2, own2], axis=0)
    meta_seg = [col < N_META, (col >= N_META) & (col < 2 * N_META)]
    low_half = col < SWA_HEAD_DIM
    neg_inf = jnp.float32(-jnp.inf)

    first_chunk = (CHUNKS_PER_BLOCK * block_in_batch).astype(F32)
    chunks = []
    for c in range(CHUNKS_PER_BLOCK):
        slot = c if c < CHUNKS_PER_BLOCK - 1 else last_slot
        prev_slot = c - 1 if c > 0 else carried_slot
        bias_sel = jnp.minimum(block_in_batch, 1) if c == 0 else 1
        chunks.append((slice(c * CHUNK, (c + 1) * CHUNK), slot, prev_slot, bias_sel, first_chunk + float(c)))

    for rows, slot, _, _, _ in chunks:
        for g in range(SWA_KV_HEADS):
            kgt = skt_ref[g * SWA_HEAD_DIM:(g + 1) * SWA_HEAD_DIM, rows]
            bd_ref[slot, g, 0:SWA_HEAD_DIM, 0:CHUNK] = kgt
            bd_ref[slot, g, SWA_HEAD_DIM:2 * SWA_HEAD_DIM, CHUNK:2 * CHUNK] = kgt
            vg = sv_ref[rows, g * SWA_HEAD_DIM:(g + 1) * SWA_HEAD_DIM]
            vbuf_ref[slot, g] = jnp.concatenate([vg, vg], axis=1)

    def scores(chunk, g):
        rows, slot, prev_slot, bias_sel, chunk_idx = chunk
        q2 = jnp.concatenate(
            [sq_ref[rows, (2 * g + p) * LANES:(2 * g + p + 1) * LANES] for p in range(2)],
            axis=0)
        s_own = _dot(q2, bd_ref[slot, g])
        s_prev = _dot(q2, bd_ref[prev_slot, g])
        dist = band_dist_ref[bias_sel]
        bias = jnp.concatenate(
            [jnp.concatenate([ALIBI_SLOPES[g * SWA_GROUP + 2 * p + e] * dist for e in range(2)], axis=1)
             for p in range(2)], axis=0)
        s = jnp.where(own4, s_own, s_prev) - bias
        dist_meta = meta_dist_ref[...] + float(CHUNK) * chunk_idx
        bias_meta = jnp.concatenate(
            [jnp.where(meta_seg[0], ALIBI_SLOPES[g * SWA_GROUP + 2 * p],
                       ALIBI_SLOPES[g * SWA_GROUP + 2 * p + 1]) * dist_meta for p in range(2)], axis=0)
        s_meta = _dot(q2, bdm_ref[g]) - bias_meta
        return s, s_meta

    def softmax(g, s, s_meta):
        lhs = []
        lhs_meta = []
        inv = []
        for p in range(2):
            sm_p = s_meta[p * CHUNK:(p + 1) * CHUNK]
            pm_pair = None
            for e in range(2):
                h = g * SWA_GROUP + 2 * p + e
                sink = sink_ref[h]
                sb = s[p * CHUNK:(p + 1) * CHUNK, e * CHUNK:(e + 1) * CHUNK]
                sm = jnp.where(meta_seg[e], sm_p, neg_inf)
                m = jnp.maximum(jnp.max(jnp.maximum(sb, sm), axis=-1, keepdims=True), sink)
                pb = jnp.exp(sb - m)
                pm = jnp.exp(sm - m)
                denom = jnp.sum(pb + pm, axis=-1, keepdims=True) + jnp.exp(sink - m)
                inv.append(1.0 / denom)
                lhs.append(jnp.concatenate(
                    [jnp.where(own, pb, 0.0).astype(BF16), jnp.where(own, 0.0, pb).astype(BF16)], axis=1))
                pm_pair = pm if pm_pair is None else pm_pair + pm
            lhs_meta.append(pm_pair.astype(BF16))
        return jnp.concatenate(lhs, axis=0), jnp.concatenate(lhs_meta, axis=0), inv

    def weighted_values(chunk, g, lhs, lhs_meta, inv):
        rows, slot, prev_slot, _, _ = chunk
        vv = jnp.concatenate([vbuf_ref[slot, g], vbuf_ref[prev_slot, g]], axis=0)
        acc = _dot(lhs, vv)
        acc_meta = _dot(lhs_meta, vm_ref[g])
        for p in range(2):
            first, second = 2 * p, 2 * p + 1
            pair = jnp.where(low_half, acc[first * CHUNK:(first + 1) * CHUNK],
                             acc[second * CHUNK:(second + 1) * CHUNK])
            scale = jnp.where(low_half, inv[first], inv[second])
            tile = (pair + acc_meta[p * CHUNK:(p + 1) * CHUNK]) * scale
            ys_ref[rows, (2 * g + p) * LANES:(2 * g + p + 1) * LANES] = tile.astype(BF16)

    tasks = [(chunk, g) for chunk in chunks for g in range(SWA_KV_HEADS)]
    ret_tasks = [(sub, h) for sub in range(RET_PER_BLOCK) for h in range(RET_HEADS)]
    assert len(tasks) == 2 * len(ret_tasks)

    def mlp_phase(k):
        c = k * FF_CHUNK
        for rs in slabs:
            u = jnp.maximum(_dot(hn_ref[rs, :], wu_ref[:, c:c + FF_CHUNK]), 0.0)
            o_ref[rs, :] += _dot((u * u).astype(BF16), wd_ref[c:c + FF_CHUNK, :])

    n_mlp_phases = D_FF // FF_CHUNK
    phase_at_task = {i * len(tasks) // n_mlp_phases: i for i in range(n_mlp_phases)}

    pending_scores = scores(*tasks[0])
    pending_probs = None
    for i, (chunk, g) in enumerate(tasks):
        current_scores = pending_scores
        if i + 1 < len(tasks):
            pending_scores = scores(*tasks[i + 1])
        if i in phase_at_task:
            mlp_phase(phase_at_task[i])
        probs = softmax(g, *current_scores)
        if i % 2 == 0:
            retention_head(*ret_tasks[i // 2])
        if pending_probs is not None:
            weighted_values(*tasks[i - 1], *pending_probs)
        pending_probs = probs
    weighted_values(*tasks[-1], *pending_probs)


def _back(x2d, mix2d, kt, gates2d, meta_mix, meta_kt, sinks, norm_gains, w_branch_out, w_up, w_down,
          blocks_per_batch):
    rows = x2d.shape[0]
    assert rows % BACK_ROWS == 0
    n_blocks = rows // BACK_ROWS
    _, band_dist, meta_dist = _swa_consts()

    cur = lambda t: jnp.minimum(t, n_blocks - 1)
    prev = lambda t: jnp.maximum(t - 1, 0)

    def resident(shape):
        zeros = (0,) * len(shape)
        return pl.BlockSpec(shape, lambda t, s: zeros, pipeline_mode=pl.Buffered(1))

    grid_spec = pltpu.PrefetchScalarGridSpec(
        num_scalar_prefetch=1,
        grid=(n_blocks + 1,),
        in_specs=[
            pl.BlockSpec((BACK_ROWS, MIX_W), lambda t, s: (cur(t), 0)),
            pl.BlockSpec((None, KT_W, BACK_ROWS),
                         lambda t, s: (cur(t) // blocks_per_batch, 0, cur(t) % blocks_per_batch)),
            pl.BlockSpec((BACK_ROWS, RET_V_W), lambda t, s: (cur(t), GATE_RG // RET_V_W)),
            resident((1, CHUNK, MIX_W)),
            resident((1, KT_W, CHUNK)),
            resident((3, CHUNK, CHUNK)),
            resident((2, D_MODEL)),
            pl.BlockSpec((BACK_ROWS, D_MODEL), lambda t, s: (prev(t), 0)),
            pl.BlockSpec((BACK_ROWS, 2 * D_MODEL), lambda t, s: (prev(t), 0)),
            resident((3, D_MODEL, D_MODEL)),
            resident((D_MODEL, D_FF)),
            resident((D_FF, D_MODEL)),
        ],
        out_specs=pl.BlockSpec((BACK_ROWS, D_MODEL), lambda t, s: (prev(t), 0)),
        scratch_shapes=[
            pltpu.VMEM((KV_SLOTS, SWA_KV_HEADS, LANES, PAIR_ROWS), BF16),
            pltpu.VMEM((KV_SLOTS, SWA_KV_HEADS, CHUNK, LANES), BF16),
            pltpu.VMEM((SWA_KV_HEADS, LANES, CHUNK), BF16),
            pltpu.VMEM((SWA_KV_HEADS, CHUNK, LANES), BF16),
            pltpu.VMEM((RET_HEADS, RET_QK_DIM, RET_V_DIM), F32),
            pltpu.VMEM((BACK_ROWS, SWA_Q_W), BF16),
            pltpu.VMEM((BACK_ROWS, RET_V_W), BF16),
            pltpu.VMEM((BACK_ROWS, D_MODEL), BF16),
        ],
    )
    return pl.pallas_call(
        functools.partial(_back_kernel, blocks_per_batch),
        out_shape=jax.ShapeDtypeStruct((rows, D_MODEL), F32),
        grid_spec=grid_spec,
        compiler_params=pltpu.CompilerParams(
            dimension_semantics=("arbitrary",),
            vmem_limit_bytes=V7X_VMEM_LIMIT_BYTES,
        ),
        name="back",
    )(sinks.astype(F32), mix2d, kt, gates2d, meta_mix, meta_kt,
      jnp.asarray(np.concatenate([band_dist, meta_dist[None]])), norm_gains,
      x2d, gates2d, w_branch_out, w_up, w_down)


def kernel(x, meta_tokens, mix_norm_g, w_in, ret_norm_g, q_norm_g, k_norm_g, sinks,
           w_ret_branch, w_swa_branch, w_out, mlp_norm_g, w_up, w_down):
    b, l, d = x.shape
    assert d == D_MODEL and l % BLOCK_ROWS == 0 and mix_norm_g.shape[0] == 1
    mix_g = mix_norm_g[0].reshape(1, d).astype(F32)
    w_in_b = w_in[0].astype(BF16)
    qg2 = jnp.tile(q_norm_g[0].astype(F32) * (SWA_HEAD_DIM ** -0.5), 2).reshape(1, LANES)
    kg2 = jnp.tile(k_norm_g[0].astype(F32), 2).reshape(1, LANES)

    mix, kt, gates, w_branch_out, w_up_b, w_down_b = _inproj(
        x, mix_g, qg2, kg2, jnp.asarray(_decay_tables(BLOCK_ROWS)), w_in_b, BLOCK_ROWS,
        cast_weights=(w_ret_branch, w_swa_branch, w_out, w_up, w_down))
    meta_chunk = jnp.pad(meta_tokens.astype(F32), ((PAD, 0), (0, 0)))[None]
    meta_mix, meta_kt, _ = _inproj(meta_chunk, mix_g, qg2, kg2, jnp.asarray(_meta_decay_tables()),
                                    w_in_b, CHUNK)

    norm_gains = jnp.stack([ret_norm_g[0].reshape(RET_V_W), mlp_norm_g[0]]).astype(F32)
    out = _back(
        x.reshape(b * l, d), mix.reshape(b * l, MIX_W), kt, gates.reshape(b * l, GATE_W),
        meta_mix, meta_kt, sinks[0], norm_gains, w_branch_out, w_up_b, w_down_b, l // BACK_ROWS)
    return out.reshape(b, l, d)
```

```python
import functools

import numpy as np
import jax
import jax.numpy as jnp
from jax import lax
from jax.experimental import pallas as pl
from jax.experimental.pallas import tpu as pltpu

D_MODEL = 1024
N_META = 16
CHUNK = 128
PAD = CHUNK - N_META
RET_HEADS = 4
RET_QK_DIM = 128
RET_V_DIM = 256
SWA_Q_HEADS = 16
SWA_KV_HEADS = 4
SWA_GROUP = SWA_Q_HEADS // SWA_KV_HEADS
SWA_HEAD_DIM = 64
D_FF = 4 * D_MODEL
EPS = 1e-6
RET_QK_W = RET_HEADS * RET_QK_DIM
RET_V_W = RET_HEADS * RET_V_DIM
SWA_Q_W = SWA_Q_HEADS * SWA_HEAD_DIM
SWA_KV_W = SWA_KV_HEADS * SWA_HEAD_DIM
D_IN = 2 * RET_QK_W + 2 * RET_V_W + SWA_Q_W + 2 * SWA_KV_W + 2 * D_MODEL

OFF_RQ = 0
OFF_RK = OFF_RQ + RET_QK_W
OFF_RV = OFF_RK + RET_QK_W
OFF_RG = OFF_RV + RET_V_W
OFF_SQ = OFF_RG + RET_V_W
OFF_SK = OFF_SQ + SWA_Q_W
OFF_SV = OFF_SK + SWA_KV_W
OFF_GR = OFF_SV + SWA_KV_W
OFF_GS = OFF_GR + D_MODEL

MIX_SQ = 0
MIX_RV = MIX_SQ + SWA_Q_W
MIX_RQ = MIX_RV + RET_V_W
MIX_SK = MIX_RQ + RET_QK_W
MIX_SV = MIX_SK + SWA_KV_W
MIX_W = MIX_SV + SWA_KV_W
GATE_RG = 2 * D_MODEL
GATE_W = GATE_RG + RET_V_W

V7X_VMEM_LIMIT_BYTES = 64 * 1024 * 1024
LANES = 128
assert LANES == 2 * SWA_HEAD_DIM and CHUNK == LANES and RET_V_W == D_MODEL

F32 = jnp.float32
BF16 = jnp.bfloat16

BLOCK_ROWS = 1024
RET_CHUNK = 256
assert BLOCK_ROWS % RET_CHUNK == 0 and BLOCK_ROWS % CHUNK == 0


def _sigmoid(x):
    return 1.0 / (1.0 + jnp.exp(-x))


def _dot(a, b):
    return jnp.dot(a, b, preferred_element_type=F32)


def _dot_nt(a, b):
    return lax.dot_general(a, b, (((1,), (1,)), ((), ())), preferred_element_type=F32)


def _retention_log_gamma():
    return np.log1p(-(2.0 ** (-5.0 - np.arange(RET_HEADS, dtype=np.float64))))


IN_N_CHUNK = 512


def _head_pair_rms(y, g2):
    rows, width = y.shape
    low = lax.broadcasted_iota(jnp.int32, (rows, LANES), 1) < SWA_HEAD_DIM
    out = []
    for c in range(0, width, LANES):
        yb = y[:, c:c + LANES]
        y2 = yb * yb
        ms_lo = jnp.sum(jnp.where(low, y2, 0.0), axis=-1, keepdims=True) * (1.0 / SWA_HEAD_DIM)
        ms_hi = jnp.sum(jnp.where(low, 0.0, y2), axis=-1, keepdims=True) * (1.0 / SWA_HEAD_DIM)
        scale = jnp.where(low, lax.rsqrt(ms_lo + EPS), lax.rsqrt(ms_hi + EPS))
        out.append(yb * scale * g2)
    return jnp.concatenate(out, axis=1) if len(out) > 1 else out[0]


def _inproj_kernel(x_ref, g_ref, qg_ref, kg_ref, dec_ref, w_ref, mix_ref, kdt_ref, gate_ref):
    x = x_ref[...]
    ms = jnp.mean(x * x, axis=-1, keepdims=True)
    hn = (x * lax.rsqrt(ms + EPS) * g_ref[...]).astype(BF16)

    def proj(lo, width):
        return _dot(hn, w_ref[:, lo:lo + width])

    for c in range(0, SWA_Q_W, IN_N_CHUNK):
        mix_ref[:, MIX_SQ + c:MIX_SQ + c + IN_N_CHUNK] = _head_pair_rms(
            proj(OFF_SQ + c, IN_N_CHUNK), qg_ref[...]).astype(BF16)
    for c in range(0, RET_V_W, IN_N_CHUNK):
        mix_ref[:, MIX_RV + c:MIX_RV + c + IN_N_CHUNK] = proj(OFF_RV + c, IN_N_CHUNK).astype(BF16)
    mix_ref[:, MIX_RQ:MIX_RQ + RET_QK_W] = (proj(OFF_RQ, RET_QK_W) * dec_ref[:, 0:RET_QK_W]).astype(BF16)
    kd = proj(OFF_RK, RET_QK_W) * dec_ref[:, RET_QK_W:2 * RET_QK_W]
    kdt_ref[...] = kd.T.astype(BF16)
    mix_ref[:, MIX_SK:MIX_SK + SWA_KV_W] = _head_pair_rms(proj(OFF_SK, SWA_KV_W), kg_ref[...]).astype(BF16)
    mix_ref[:, MIX_SV:MIX_SV + SWA_KV_W] = proj(OFF_SV, SWA_KV_W).astype(BF16)
    for c in range(0, 2 * D_MODEL, IN_N_CHUNK):
        gate_ref[:, c:c + IN_N_CHUNK] = proj(OFF_GR + c, IN_N_CHUNK).astype(BF16)
    for c in range(0, RET_V_W, IN_N_CHUNK):
        gate_ref[:, GATE_RG + c:GATE_RG + c + IN_N_CHUNK] = proj(OFF_RG + c, IN_N_CHUNK).astype(BF16)


def _inproj_cast_kernel(x_ref, g_ref, qg_ref, kg_ref, dec_ref, w_ref,
                        wr_ref, ws_ref, wo_ref, wu_ref, wd_ref,
                        mix_ref, kdt_ref, gate_ref, wbo_out_ref, wu_out_ref, wd_out_ref):
    _inproj_kernel(x_ref, g_ref, qg_ref, kg_ref, dec_ref, w_ref, mix_ref, kdt_ref, gate_ref)
    for k, src in enumerate((wr_ref, ws_ref, wo_ref)):
        wbo_out_ref[k] = src[...].astype(BF16)
    wu_out_ref[...] = wu_ref[...].astype(BF16)
    wd_out_ref[...] = wd_ref[...].astype(BF16)


BF16_ROW_TILE = 16


def _inproj(x3d, g, qg2, kg2, dec, w_bf16, block_rows, cast_weights=None):
    b, seq, _ = x3d.shape
    assert seq % block_rows == 0 and dec.shape == (block_rows, 2 * RET_QK_W)
    steps = seq // block_rows
    n_steps = b * steps
    const = lambda shape: pl.BlockSpec(shape, lambda i: (0, 0))
    out_shape = [jax.ShapeDtypeStruct((b, seq, MIX_W), BF16),
                 jax.ShapeDtypeStruct((b, RET_QK_W, seq), BF16),
                 jax.ShapeDtypeStruct((b, seq, GATE_W), BF16)]
    out_specs = [pl.BlockSpec((None, block_rows, MIX_W), lambda i: (i // steps, i % steps, 0)),
                 pl.BlockSpec((None, RET_QK_W, block_rows), lambda i: (i // steps, 0, i % steps)),
                 pl.BlockSpec((None, block_rows, GATE_W), lambda i: (i // steps, i % steps, 0))]
    extra_in, extra_specs = [], []
    if cast_weights is not None:
        band = lambda w: w.shape[1] // n_steps
        for w in cast_weights:
            assert w.shape[1] % n_steps == 0 and band(w) % BF16_ROW_TILE == 0
            extra_in.append(w)
            extra_specs.append(pl.BlockSpec((None, band(w), w.shape[2]), lambda i: (0, i, 0)))
        w_r, _, _, w_u, w_d = cast_weights
        out_shape += [jax.ShapeDtypeStruct((3,) + w_r.shape[1:], BF16),
                      jax.ShapeDtypeStruct(w_u.shape[1:], BF16),
                      jax.ShapeDtypeStruct(w_d.shape[1:], BF16)]
        out_specs += [pl.BlockSpec((3, band(w_r), w_r.shape[2]), lambda i: (0, i, 0)),
                      pl.BlockSpec((band(w_u), w_u.shape[2]), lambda i: (i, 0)),
                      pl.BlockSpec((band(w_d), w_d.shape[2]), lambda i: (i, 0))]
    return pl.pallas_call(
        _inproj_kernel if cast_weights is None else _inproj_cast_kernel,
        out_shape=tuple(out_shape),
        grid=(n_steps,),
        in_specs=[
            pl.BlockSpec((None, block_rows, D_MODEL), lambda i: (i // steps, i % steps, 0)),
            const((1, D_MODEL)),
            const((1, LANES)),
            const((1, LANES)),
            const((block_rows, 2 * RET_QK_W)),
            pl.BlockSpec((D_MODEL, D_IN), lambda i: (0, 0), pipeline_mode=pl.Buffered(1)),
        ] + extra_specs,
        out_specs=tuple(out_specs),
        compiler_params=pltpu.CompilerParams(
            dimension_semantics=("arbitrary",),
            vmem_limit_bytes=V7X_VMEM_LIMIT_BYTES,
        ),
        name="inproj",
    )(x3d, g, qg2, kg2, dec, w_bf16, *extra_in)


def _decay_tables(block_rows):
    lg = _retention_log_gamma()
    pos = (np.arange(block_rows) % RET_CHUNK).astype(np.float64) + 1.0
    qd = np.exp(lg[None, :] * pos[:, None])
    kd = np.exp(-lg[None, :] * pos[:, None]) * (RET_QK_DIM ** -0.5)
    rep = lambda t: np.repeat(t, RET_QK_DIM, axis=1)
    return np.concatenate([rep(qd), rep(kd)], axis=1).astype(np.float32)


def _meta_decay_tables():
    lg = _retention_log_gamma()
    pos = np.arange(CHUNK, dtype=np.float64)
    kd = np.exp(lg[None, :] * (CHUNK - 1.0 - pos[:, None])) * (RET_QK_DIM ** -0.5)
    rep = lambda t: np.repeat(t, RET_QK_DIM, axis=1)
    return np.concatenate([np.ones((CHUNK, RET_QK_W)), rep(kd)], axis=1).astype(np.float32)


RET_CHUNK_DECAY = [float(c) for c in np.exp(_retention_log_gamma() * RET_CHUNK)]
PAIR_ROWS = 2 * CHUNK
BACK_ROWS = 512
DENSE_ROWS = 512
assert BACK_ROWS % RET_CHUNK == 0 and BLOCK_ROWS % BACK_ROWS == 0 and BACK_ROWS % DENSE_ROWS == 0
CHUNKS_PER_BLOCK = BACK_ROWS // CHUNK
RET_PER_BLOCK = BACK_ROWS // RET_CHUNK
KV_SLOTS = CHUNKS_PER_BLOCK + 1
FF_CHUNK = 1024


def _swa_consts():
    slopes = 2.0 ** (-8.0 * np.arange(1, SWA_Q_HEADS + 1, dtype=np.float64) / SWA_Q_HEADS)
    slopes = slopes.astype(np.float32)
    i = np.arange(CHUNK)[:, None]
    j = np.arange(CHUNK)[None, :]
    band_dist = np.where(j <= i, i - j, i - j + CHUNK).astype(np.float32)
    meta_dist = np.where(j < 2 * N_META, i - (j % N_META) + N_META, np.inf).astype(np.float32)
    band_dist = np.stack([np.where(j <= i, band_dist, np.inf), band_dist]).astype(np.float32)
    return [float(s) for s in slopes], band_dist, meta_dist


ALIBI_SLOPES = _swa_consts()[0]


def _back_kernel(blocks_per_batch,
                 sink_ref,
                 mix_ref, kdt_ref, rg_ref, meta_ref, mkdt_ref, dist_ref, gains_ref,
                 x_ref, gate_ref, wbo_ref, wu_ref, wd_ref,
                 o_ref,
                 bd_ref, vbuf_ref, bdm_ref, vm_ref, state_ref, ys_ref, yr_ref, hn_ref):
    sq_ref = mix_ref.at[:, MIX_SQ:MIX_SQ + SWA_Q_W]
    rv_ref = mix_ref.at[:, MIX_RV:MIX_RV + RET_V_W]
    qd_ref = mix_ref.at[:, MIX_RQ:MIX_RQ + RET_QK_W]
    sk_ref = mix_ref.at[:, MIX_SK:MIX_SK + SWA_KV_W]
    sv_ref = mix_ref.at[:, MIX_SV:MIX_SV + SWA_KV_W]
    mk_ref = meta_ref.at[0, PAD:CHUNK, MIX_SK:MIX_SK + SWA_KV_W]
    mv_ref = meta_ref.at[0, PAD:CHUNK, MIX_SV:MIX_SV + SWA_KV_W]
    mrv_ref = meta_ref.at[0, :, MIX_RV:MIX_RV + RET_V_W]
    mkdt_ref = mkdt_ref.at[0]
    band_dist_ref = dist_ref.at[0:2]
    meta_dist_ref = dist_ref.at[2]
    rn_ref = gains_ref.at[0:1, :]
    g_ref = gains_ref.at[1:2, :]
    gr_ref = gate_ref.at[:, 0:D_MODEL]
    gs_ref = gate_ref.at[:, D_MODEL:2 * D_MODEL]
    wr_ref, ws_ref, wo_ref = wbo_ref.at[0], wbo_ref.at[1], wbo_ref.at[2]
    t = pl.program_id(0)
    block = jnp.minimum(t, pl.num_programs(0) - 2)
    block_in_batch = block % blocks_per_batch
    zeros_head = jnp.zeros((CHUNK, SWA_HEAD_DIM), BF16)

    @pl.when(t == 0)
    def _():
        ys_ref[...] = jnp.zeros_like(ys_ref)
        yr_ref[...] = jnp.zeros_like(yr_ref)

    @pl.when(block_in_batch == 0)
    def _():
        bd_ref[...] = jnp.zeros_like(bd_ref)
        vbuf_ref[...] = jnp.zeros_like(vbuf_ref)
        bdm_ref[...] = jnp.zeros_like(bdm_ref)
        vm_ref[...] = jnp.zeros_like(vm_ref)
        zm = jnp.zeros((N_META, SWA_HEAD_DIM), BF16)
        for g in range(SWA_KV_HEADS):
            cols = slice(g * SWA_HEAD_DIM, (g + 1) * SWA_HEAD_DIM)
            km = mk_ref[:, cols]
            bdm_ref[g, 0:N_META, :] = jnp.concatenate([km, zm], axis=1)
            bdm_ref[g, N_META:2 * N_META, :] = jnp.concatenate([zm, km], axis=1)
            vmeta = mv_ref[:, cols]
            vm_ref[g, 0:N_META, :] = jnp.concatenate([vmeta, zm], axis=1)
            vm_ref[g, N_META:2 * N_META, :] = jnp.concatenate([zm, vmeta], axis=1)
        for h in range(RET_HEADS):
            state_ref[h] = _dot(mkdt_ref[h * RET_QK_DIM:(h + 1) * RET_QK_DIM, :],
                                mrv_ref[:, h * RET_V_DIM:(h + 1) * RET_V_DIM])

    slabs = [slice(r, r + DENSE_ROWS) for r in range(0, BACK_ROWS, DENSE_ROWS)]
    for rs in slabs:
        merged = (_sigmoid(gr_ref[rs, :].astype(F32)) * _dot(yr_ref[rs, :], wr_ref[...])
                  + _sigmoid(gs_ref[rs, :].astype(F32)) * _dot(ys_ref[rs, :], ws_ref[...]))
        o_ref[rs, :] = x_ref[rs, :] + _dot(merged.astype(BF16), wo_ref[...])
        h1 = o_ref[rs, :]
        hn_ref[rs, :] = (h1 * lax.rsqrt(jnp.mean(h1 * h1, axis=-1, keepdims=True) + EPS)
                         * g_ref[...]).astype(BF16)

    ret_row = lax.broadcasted_iota(jnp.int32, (RET_CHUNK, RET_CHUNK), 0)
    ret_col = lax.broadcasted_iota(jnp.int32, (RET_CHUNK, RET_CHUNK), 1)
    ret_causal = ret_col <= ret_row

    def retention_head(sub, h):
        rows = slice(sub * RET_CHUNK, (sub + 1) * RET_CHUNK)
        vcols = slice(h * RET_V_DIM, (h + 1) * RET_V_DIM)
        qd = qd_ref[rows, h * RET_QK_DIM:(h + 1) * RET_QK_DIM]
        kdt = kdt_ref[h * RET_QK_DIM:(h + 1) * RET_QK_DIM, rows]
        v = rv_ref[rows, vcols]
        state = state_ref[h]
        s = jnp.where(ret_causal, _dot(qd, kdt), 0.0).astype(BF16)
        o = _dot(s, v) + _dot(qd, state.astype(BF16))
        state_ref[h] = RET_CHUNK_DECAY[h] * (state + _dot(kdt, v))
        ms = jnp.mean(o * o, axis=-1, keepdims=True)
        gate = rg_ref[rows, vcols].astype(F32)
        yr_ref[rows, vcols] = (o * lax.rsqrt(ms + EPS) * rn_ref[:, vcols]
                               * (gate * _sigmoid(gate))).astype(BF16)

    row = lax.broadcasted_iota(jnp.int32, (CHUNK, CHUNK), 0)
    col = lax.broadcasted_iota(jnp.int32, (CHUNK, CHUNK), 1)
    own = col <= row
    own2 = jnp.concatenate([own, own], axis=1)
    own4 = jnp.concatenate([own2, own2], axis=0)
    meta_seg = [col < N_META, (col >= N_META) & (col < 2 * N_META)]
    low_half = col < SWA_HEAD_DIM
    neg_inf = jnp.float32(-jnp.inf)

    first_chunk = (CHUNKS_PER_BLOCK * block_in_batch).astype(F32)
    chunks = []
    for c in range(CHUNKS_PER_BLOCK):
        bias_sel = jnp.minimum(block_in_batch, 1) if c == 0 else 1
        chunks.append((slice(c * CHUNK, (c + 1) * CHUNK), c + 1, bias_sel, first_chunk + float(c)))

    for rows, slot, _, _ in chunks:
        for g in range(SWA_KV_HEADS):
            kg = sk_ref[rows, g * SWA_HEAD_DIM:(g + 1) * SWA_HEAD_DIM]
            bd_ref[slot, g, 0:CHUNK, :] = jnp.concatenate([kg, zeros_head], axis=1)
            bd_ref[slot, g, CHUNK:2 * CHUNK, :] = jnp.concatenate([zeros_head, kg], axis=1)
            vg = sv_ref[rows, g * SWA_HEAD_DIM:(g + 1) * SWA_HEAD_DIM]
            vbuf_ref[slot, g] = jnp.concatenate([vg, vg], axis=1)

    def scores(chunk, g):
        rows, slot, bias_sel, chunk_idx = chunk
        q2 = jnp.concatenate(
            [sq_ref[rows, (2 * g + p) * LANES:(2 * g + p + 1) * LANES] for p in range(2)],
            axis=0)
        keys = jnp.concatenate([bd_ref[slot - 1, g], bd_ref[slot, g], bdm_ref[g]], axis=0)
        s_all = _dot_nt(q2, keys)
        s_prev = s_all[:, 0:PAIR_ROWS]
        s_own = s_all[:, PAIR_ROWS:2 * PAIR_ROWS]
        dist = band_dist_ref[bias_sel]
        bias = jnp.concatenate(
            [jnp.concatenate([ALIBI_SLOPES[g * SWA_GROUP + 2 * p + e] * dist for e in range(2)], axis=1)
             for p in range(2)], axis=0)
        s = jnp.where(own4, s_own, s_prev) - bias
        dist_meta = meta_dist_ref[...] + float(CHUNK) * chunk_idx
        bias_meta = jnp.concatenate(
            [jnp.where(meta_seg[0], ALIBI_SLOPES[g * SWA_GROUP + 2 * p],
                       ALIBI_SLOPES[g * SWA_GROUP + 2 * p + 1]) * dist_meta for p in range(2)], axis=0)
        s_meta = s_all[:, 2 * PAIR_ROWS:] - bias_meta
        return s, s_meta

    def softmax(g, s, s_meta):
        lhs = []
        lhs_meta = []
        inv = []
        for p in range(2):
            sm_p = s_meta[p * CHUNK:(p + 1) * CHUNK]
            pm_pair = None
            for e in range(2):
                h = g * SWA_GROUP + 2 * p + e
                sink = sink_ref[h]
                sb = s[p * CHUNK:(p + 1) * CHUNK, e * CHUNK:(e + 1) * CHUNK]
                sm = jnp.where(meta_seg[e], sm_p, neg_inf)
                m = jnp.maximum(jnp.max(jnp.maximum(sb, sm), axis=-1, keepdims=True), sink)
                pb = jnp.exp(sb - m)
                pm = jnp.exp(sm - m)
                denom = jnp.sum(pb + pm, axis=-1, keepdims=True) + jnp.exp(sink - m)
                inv.append(1.0 / denom)
                lhs.append(jnp.concatenate(
                    [jnp.where(own, pb, 0.0).astype(BF16), jnp.where(own, 0.0, pb).astype(BF16)], axis=1))
                pm_pair = pm if pm_pair is None else pm_pair + pm
            lhs_meta.append(pm_pair.astype(BF16))
        return jnp.concatenate(lhs, axis=0), jnp.concatenate(lhs_meta, axis=0), inv

    def weighted_values(chunk, g, lhs, lhs_meta, inv):
        rows, slot, _, _ = chunk
        vv = jnp.concatenate([vbuf_ref[slot, g], vbuf_ref[slot - 1, g]], axis=0)
        acc = _dot(lhs, vv)
        acc_meta = _dot(lhs_meta, vm_ref[g])
        for p in range(2):
            first, second = 2 * p, 2 * p + 1
            pair = jnp.where(low_half, acc[first * CHUNK:(first + 1) * CHUNK],
                             acc[second * CHUNK:(second + 1) * CHUNK])
            scale = jnp.where(low_half, inv[first], inv[second])
            tile = (pair + acc_meta[p * CHUNK:(p + 1) * CHUNK]) * scale
            ys_ref[rows, (2 * g + p) * LANES:(2 * g + p + 1) * LANES] = tile.astype(BF16)

    tasks = [(chunk, g) for chunk in chunks for g in range(SWA_KV_HEADS)]
    ret_tasks = [(sub, h) for sub in range(RET_PER_BLOCK) for h in range(RET_HEADS)]
    assert len(tasks) == 2 * len(ret_tasks)

    def mlp_phase(k):
        c = k * FF_CHUNK
        for rs in slabs:
            u = jnp.maximum(_dot(hn_ref[rs, :], wu_ref[:, c:c + FF_CHUNK]), 0.0)
            o_ref[rs, :] += _dot((u * u).astype(BF16), wd_ref[c:c + FF_CHUNK, :])

    n_mlp_phases = D_FF // FF_CHUNK
    phase_at_task = {i * len(tasks) // n_mlp_phases: i for i in range(n_mlp_phases)}

    pending_scores = scores(*tasks[0])
    pending_probs = None
    for i, (chunk, g) in enumerate(tasks):
        current_scores = pending_scores
        if i + 1 < len(tasks):
            pending_scores = scores(*tasks[i + 1])
        if i in phase_at_task:
            mlp_phase(phase_at_task[i])
        probs = softmax(g, *current_scores)
        if i % 2 == 0:
            retention_head(*ret_tasks[i // 2])
        if pending_probs is not None:
            weighted_values(*tasks[i - 1], *pending_probs)
        pending_probs = probs
    weighted_values(*tasks[-1], *pending_probs)
    bd_ref[0] = bd_ref[CHUNKS_PER_BLOCK]
    vbuf_ref[0] = vbuf_ref[CHUNKS_PER_BLOCK]


def _back(x2d, mix2d, kdt, gates2d, meta_mix, meta_kdt, sinks, norm_gains, w_branch_out, w_up, w_down,
          blocks_per_batch):
    rows = x2d.shape[0]
    assert rows % BACK_ROWS == 0
    n_blocks = rows // BACK_ROWS
    _, band_dist, meta_dist = _swa_consts()

    cur = lambda t: jnp.minimum(t, n_blocks - 1)
    prev = lambda t: jnp.maximum(t - 1, 0)

    def resident(shape):
        zeros = (0,) * len(shape)
        return pl.BlockSpec(shape, lambda t, s: zeros, pipeline_mode=pl.Buffered(1))

    grid_spec = pltpu.PrefetchScalarGridSpec(
        num_scalar_prefetch=1,
        grid=(n_blocks + 1,),
        in_specs=[
            pl.BlockSpec((BACK_ROWS, MIX_W), lambda t, s: (cur(t), 0)),
            pl.BlockSpec((None, RET_QK_W, BACK_ROWS),
                         lambda t, s: (cur(t) // blocks_per_batch, 0, cur(t) % blocks_per_batch)),
            pl.BlockSpec((BACK_ROWS, RET_V_W), lambda t, s: (cur(t), GATE_RG // RET_V_W)),
            resident((1, CHUNK, MIX_W)),
            resident((1, RET_QK_W, CHUNK)),
            resident((3, CHUNK, CHUNK)),
            resident((2, D_MODEL)),
            pl.BlockSpec((BACK_ROWS, D_MODEL), lambda t, s: (prev(t), 0)),
            pl.BlockSpec((BACK_ROWS, 2 * D_MODEL), lambda t, s: (prev(t), 0)),
            resident((3, D_MODEL, D_MODEL)),
            resident((D_MODEL, D_FF)),
            resident((D_FF, D_MODEL)),
        ],
        out_specs=pl.BlockSpec((BACK_ROWS, D_MODEL), lambda t, s: (prev(t), 0)),
        scratch_shapes=[
            pltpu.VMEM((KV_SLOTS, SWA_KV_HEADS, PAIR_ROWS, LANES), BF16),
            pltpu.VMEM((KV_SLOTS, SWA_KV_HEADS, CHUNK, LANES), BF16),
            pltpu.VMEM((SWA_KV_HEADS, CHUNK, LANES), BF16),
            pltpu.VMEM((SWA_KV_HEADS, CHUNK, LANES), BF16),
            pltpu.VMEM((RET_HEADS, RET_QK_DIM, RET_V_DIM), F32),
            pltpu.VMEM((BACK_ROWS, SWA_Q_W), BF16),
            pltpu.VMEM((BACK_ROWS, RET_V_W), BF16),
            pltpu.VMEM((BACK_ROWS, D_MODEL), BF16),
        ],
    )
    return pl.pallas_call(
        functools.partial(_back_kernel, blocks_per_batch),
        out_shape=jax.ShapeDtypeStruct((rows, D_MODEL), F32),
        grid_spec=grid_spec,
        compiler_params=pltpu.CompilerParams(
            dimension_semantics=("arbitrary",),
            vmem_limit_bytes=V7X_VMEM_LIMIT_BYTES,
        ),
        name="back",
    )(sinks.astype(F32), mix2d, kdt, gates2d, meta_mix, meta_kdt,
      jnp.asarray(np.concatenate([band_dist, meta_dist[None]])), norm_gains,
      x2d, gates2d, w_branch_out, w_up, w_down)


def kernel(x, meta_tokens, mix_norm_g, w_in, ret_norm_g, q_norm_g, k_norm_g, sinks,
           w_ret_branch, w_swa_branch, w_out, mlp_norm_g, w_up, w_down):
    b, l, d = x.shape
    assert d == D_MODEL and l % BLOCK_ROWS == 0 and mix_norm_g.shape[0] == 1
    mix_g = mix_norm_g[0].reshape(1, d).astype(F32)
    w_in_b = w_in[0].astype(BF16)
    qg2 = jnp.tile(q_norm_g[0].astype(F32) * (SWA_HEAD_DIM ** -0.5), 2).reshape(1, LANES)
    kg2 = jnp.tile(k_norm_g[0].astype(F32), 2).reshape(1, LANES)

    mix, kdt, gates, w_branch_out, w_up_b, w_down_b = _inproj(
        x, mix_g, qg2, kg2, jnp.asarray(_decay_tables(BLOCK_ROWS)), w_in_b, BLOCK_ROWS,
        cast_weights=(w_ret_branch, w_swa_branch, w_out, w_up, w_down))
    meta_chunk = jnp.pad(meta_tokens.astype(F32), ((PAD, 0), (0, 0)))[None]
    meta_mix, meta_kdt, _ = _inproj(meta_chunk, mix_g, qg2, kg2, jnp.asarray(_meta_decay_tables()),
                                    w_in_b, CHUNK)

    norm_gains = jnp.stack([ret_norm_g[0].reshape(RET_V_W), mlp_norm_g[0]]).astype(F32)
    out = _back(
        x.reshape(b * l, d), mix.reshape(b * l, MIX_W), kdt, gates.reshape(b * l, GATE_W),
        meta_mix, meta_kdt, sinks[0], norm_gains, w_branch_out, w_up_b, w_down_b, l // BACK_ROWS)
    return out.reshape(b, l, d)
```

```python
import functools

import numpy as np
import jax
import jax.numpy as jnp
from jax import lax
from jax.experimental import pallas as pl
from jax.experimental.pallas import tpu as pltpu

D_MODEL = 1024
N_META = 16
CHUNK = 128
PAD = CHUNK - N_META
RET_HEADS = 4
RET_QK_DIM = 128
RET_V_DIM = 256
SWA_Q_HEADS = 16
SWA_KV_HEADS = 4
SWA_GROUP = SWA_Q_HEADS // SWA_KV_HEADS
SWA_HEAD_DIM = 64
D_FF = 4 * D_MODEL
EPS = 1e-6
RET_QK_W = RET_HEADS * RET_QK_DIM
RET_V_W = RET_HEADS * RET_V_DIM
SWA_Q_W = SWA_Q_HEADS * SWA_HEAD_DIM
SWA_KV_W = SWA_KV_HEADS * SWA_HEAD_DIM
D_IN = 2 * RET_QK_W + 2 * RET_V_W + SWA_Q_W + 2 * SWA_KV_W + 2 * D_MODEL

OFF_RQ = 0
OFF_RK = OFF_RQ + RET_QK_W
OFF_RV = OFF_RK + RET_QK_W
OFF_RG = OFF_RV + RET_V_W
OFF_SQ = OFF_RG + RET_V_W
OFF_SK = OFF_SQ + SWA_Q_W
OFF_SV = OFF_SK + SWA_KV_W
OFF_GR = OFF_SV + SWA_KV_W
OFF_GS = OFF_GR + D_MODEL

MIX_SQ = 0
MIX_RV = MIX_SQ + SWA_Q_W
MIX_RQ = MIX_RV + RET_V_W
MIX_SK = MIX_RQ + RET_QK_W
MIX_SV = MIX_SK + SWA_KV_W
MIX_W = MIX_SV + SWA_KV_W
GATE_RG = 2 * D_MODEL
GATE_W = GATE_RG + RET_V_W

V7X_VMEM_LIMIT_BYTES = 64 * 1024 * 1024
LANES = 128
assert LANES == 2 * SWA_HEAD_DIM and CHUNK == LANES and RET_V_W == D_MODEL

F32 = jnp.float32
BF16 = jnp.bfloat16

BLOCK_ROWS = 1024
RET_CHUNK = 256
assert BLOCK_ROWS % RET_CHUNK == 0 and BLOCK_ROWS % CHUNK == 0


def _sigmoid(x):
    return 1.0 / (1.0 + jnp.exp(-x))


def _dot(a, b):
    return jnp.dot(a, b, preferred_element_type=F32)


def _dot_nt(a, b):
    return lax.dot_general(a, b, (((1,), (1,)), ((), ())), preferred_element_type=F32)


def _retention_log_gamma():
    return np.log1p(-(2.0 ** (-5.0 - np.arange(RET_HEADS, dtype=np.float64))))


IN_N_CHUNK = 512


def _head_pair_rms(y, g2):
    rows, width = y.shape
    low = lax.broadcasted_iota(jnp.int32, (rows, LANES), 1) < SWA_HEAD_DIM
    out = []
    for c in range(0, width, LANES):
        yb = y[:, c:c + LANES]
        y2 = yb * yb
        ms_lo = jnp.sum(jnp.where(low, y2, 0.0), axis=-1, keepdims=True) * (1.0 / SWA_HEAD_DIM)
        ms_hi = jnp.sum(jnp.where(low, 0.0, y2), axis=-1, keepdims=True) * (1.0 / SWA_HEAD_DIM)
        scale = jnp.where(low, lax.rsqrt(ms_lo + EPS), lax.rsqrt(ms_hi + EPS))
        out.append(yb * scale * g2)
    return jnp.concatenate(out, axis=1) if len(out) > 1 else out[0]


def _inproj_kernel(x_ref, g_ref, qg_ref, kg_ref, dec_ref, w_ref, mix_ref, kdt_ref, gate_ref):
    x = x_ref[...]
    ms = jnp.mean(x * x, axis=-1, keepdims=True)
    hn = (x * lax.rsqrt(ms + EPS) * g_ref[...]).astype(BF16)

    def proj(lo, width):
        return _dot(hn, w_ref[:, lo:lo + width])

    for c in range(0, SWA_Q_W, IN_N_CHUNK):
        mix_ref[:, MIX_SQ + c:MIX_SQ + c + IN_N_CHUNK] = _head_pair_rms(
            proj(OFF_SQ + c, IN_N_CHUNK), qg_ref[...]).astype(BF16)
    for c in range(0, RET_V_W, IN_N_CHUNK):
        mix_ref[:, MIX_RV + c:MIX_RV + c + IN_N_CHUNK] = proj(OFF_RV + c, IN_N_CHUNK).astype(BF16)
    mix_ref[:, MIX_RQ:MIX_RQ + RET_QK_W] = (proj(OFF_RQ, RET_QK_W) * dec_ref[:, 0:RET_QK_W]).astype(BF16)
    kd = proj(OFF_RK, RET_QK_W) * dec_ref[:, RET_QK_W:2 * RET_QK_W]
    kdt_ref[...] = kd.T.astype(BF16)
    mix_ref[:, MIX_SK:MIX_SK + SWA_KV_W] = _head_pair_rms(proj(OFF_SK, SWA_KV_W), kg_ref[...]).astype(BF16)
    mix_ref[:, MIX_SV:MIX_SV + SWA_KV_W] = proj(OFF_SV, SWA_KV_W).astype(BF16)
    for c in range(0, 2 * D_MODEL, IN_N_CHUNK):
        gate_ref[:, c:c + IN_N_CHUNK] = proj(OFF_GR + c, IN_N_CHUNK).astype(BF16)
    for c in range(0, RET_V_W, IN_N_CHUNK):
        gate_ref[:, GATE_RG + c:GATE_RG + c + IN_N_CHUNK] = proj(OFF_RG + c, IN_N_CHUNK).astype(BF16)


def _inproj_cast_kernel(x_ref, g_ref, qg_ref, kg_ref, dec_ref, w_ref,
                        wr_ref, ws_ref, wo_ref, wu_ref, wd_ref,
                        mix_ref, kdt_ref, gate_ref, wbo_out_ref, wu_out_ref, wd_out_ref):
    _inproj_kernel(x_ref, g_ref, qg_ref, kg_ref, dec_ref, w_ref, mix_ref, kdt_ref, gate_ref)
    for k, src in enumerate((wr_ref, ws_ref, wo_ref)):
        wbo_out_ref[k] = src[...].astype(BF16)
    wu_out_ref[...] = wu_ref[...].astype(BF16)
    wd_out_ref[...] = wd_ref[...].astype(BF16)


BF16_ROW_TILE = 16


def _inproj(x3d, g, qg2, kg2, dec, w_bf16, block_rows, cast_weights=None):
    b, seq, _ = x3d.shape
    assert seq % block_rows == 0 and dec.shape == (block_rows, 2 * RET_QK_W)
    steps = seq // block_rows
    n_steps = b * steps
    const = lambda shape: pl.BlockSpec(shape, lambda i: (0, 0))
    out_shape = [jax.ShapeDtypeStruct((b, seq, MIX_W), BF16),
                 jax.ShapeDtypeStruct((b, RET_QK_W, seq), BF16),
                 jax.ShapeDtypeStruct((b, seq, GATE_W), BF16)]
    out_specs = [pl.BlockSpec((None, block_rows, MIX_W), lambda i: (i // steps, i % steps, 0)),
                 pl.BlockSpec((None, RET_QK_W, block_rows), lambda i: (i // steps, 0, i % steps)),
                 pl.BlockSpec((None, block_rows, GATE_W), lambda i: (i // steps, i % steps, 0))]
    extra_in, extra_specs = [], []
    if cast_weights is not None:
        band = lambda w: w.shape[1] // n_steps
        for w in cast_weights:
            assert w.shape[1] % n_steps == 0 and band(w) % BF16_ROW_TILE == 0
            extra_in.append(w)
            extra_specs.append(pl.BlockSpec((None, band(w), w.shape[2]), lambda i: (0, i, 0)))
        w_r, _, _, w_u, w_d = cast_weights
        out_shape += [jax.ShapeDtypeStruct((3,) + w_r.shape[1:], BF16),
                      jax.ShapeDtypeStruct(w_u.shape[1:], BF16),
                      jax.ShapeDtypeStruct(w_d.shape[1:], BF16)]
        out_specs += [pl.BlockSpec((3, band(w_r), w_r.shape[2]), lambda i: (0, i, 0)),
                      pl.BlockSpec((band(w_u), w_u.shape[2]), lambda i: (i, 0)),
                      pl.BlockSpec((band(w_d), w_d.shape[2]), lambda i: (i, 0))]
    return pl.pallas_call(
        _inproj_kernel if cast_weights is None else _inproj_cast_kernel,
        out_shape=tuple(out_shape),
        grid=(n_steps,),
        in_specs=[
            pl.BlockSpec((None, block_rows, D_MODEL), lambda i: (i // steps, i % steps, 0)),
            const((1, D_MODEL)),
            const((1, LANES)),
            const((1, LANES)),
            const((block_rows, 2 * RET_QK_W)),
            pl.BlockSpec((D_MODEL, D_IN), lambda i: (0, 0), pipeline_mode=pl.Buffered(1)),
        ] + extra_specs,
        out_specs=tuple(out_specs),
        compiler_params=pltpu.CompilerParams(
            dimension_semantics=("arbitrary",),
            vmem_limit_bytes=V7X_VMEM_LIMIT_BYTES,
        ),
        name="inproj",
    )(x3d, g, qg2, kg2, dec, w_bf16, *extra_in)


def _decay_tables(block_rows):
    lg = _retention_log_gamma()
    pos = (np.arange(block_rows) % RET_CHUNK).astype(np.float64) + 1.0
    qd = np.exp(lg[None, :] * pos[:, None])
    kd = np.exp(-lg[None, :] * pos[:, None]) * (RET_QK_DIM ** -0.5)
    rep = lambda t: np.repeat(t, RET_QK_DIM, axis=1)
    return np.concatenate([rep(qd), rep(kd)], axis=1).astype(np.float32)


def _meta_decay_tables():
    lg = _retention_log_gamma()
    pos = np.arange(CHUNK, dtype=np.float64)
    kd = np.exp(lg[None, :] * (CHUNK - 1.0 - pos[:, None])) * (RET_QK_DIM ** -0.5)
    rep = lambda t: np.repeat(t, RET_QK_DIM, axis=1)
    return np.concatenate([np.ones((CHUNK, RET_QK_W)), rep(kd)], axis=1).astype(np.float32)


RET_CHUNK_DECAY = [float(c) for c in np.exp(_retention_log_gamma() * RET_CHUNK)]
PAIR_ROWS = 2 * CHUNK
BACK_ROWS = 512
DENSE_ROWS = 512
assert BACK_ROWS % RET_CHUNK == 0 and BLOCK_ROWS % BACK_ROWS == 0 and BACK_ROWS % DENSE_ROWS == 0
CHUNKS_PER_BLOCK = BACK_ROWS // CHUNK
RET_PER_BLOCK = BACK_ROWS // RET_CHUNK
KV_SLOTS = CHUNKS_PER_BLOCK + 1
FF_CHUNK = 1024


def _swa_consts():
    slopes = 2.0 ** (-8.0 * np.arange(1, SWA_Q_HEADS + 1, dtype=np.float64) / SWA_Q_HEADS)
    slopes = slopes.astype(np.float32)
    i = np.arange(CHUNK)[:, None]
    j = np.arange(CHUNK)[None, :]
    band_dist = np.where(j <= i, i - j, i - j + CHUNK).astype(np.float32)
    meta_dist = np.where(j < 2 * N_META, i - (j % N_META) + N_META, np.inf).astype(np.float32)
    band_dist = np.stack([np.where(j <= i, band_dist, np.inf), band_dist]).astype(np.float32)
    return [float(s) for s in slopes], band_dist, meta_dist


ALIBI_SLOPES = _swa_consts()[0]


def _back_kernel(blocks_per_batch,
                 sink_ref,
                 mix_ref, kdt_ref, rg_ref, meta_ref, mkdt_ref, dist_ref, gains_ref,
                 x_ref, gate_ref, wbo_ref, wu_ref, wd_ref,
                 o_ref,
                 bd_ref, vbuf_ref, bdm_ref, vm_ref, state_ref, ys_ref, yr_ref, hn_ref):
    sq_ref = mix_ref.at[:, MIX_SQ:MIX_SQ + SWA_Q_W]
    rv_ref = mix_ref.at[:, MIX_RV:MIX_RV + RET_V_W]
    qd_ref = mix_ref.at[:, MIX_RQ:MIX_RQ + RET_QK_W]
    sk_ref = mix_ref.at[:, MIX_SK:MIX_SK + SWA_KV_W]
    sv_ref = mix_ref.at[:, MIX_SV:MIX_SV + SWA_KV_W]
    mk_ref = meta_ref.at[0, PAD:CHUNK, MIX_SK:MIX_SK + SWA_KV_W]
    mv_ref = meta_ref.at[0, PAD:CHUNK, MIX_SV:MIX_SV + SWA_KV_W]
    mrv_ref = meta_ref.at[0, :, MIX_RV:MIX_RV + RET_V_W]
    mkdt_ref = mkdt_ref.at[0]
    band_dist_ref = dist_ref.at[0:2]
    meta_dist_ref = dist_ref.at[2]
    rn_ref = gains_ref.at[0:1, :]
    g_ref = gains_ref.at[1:2, :]
    gr_ref = gate_ref.at[:, 0:D_MODEL]
    gs_ref = gate_ref.at[:, D_MODEL:2 * D_MODEL]
    wr_ref, ws_ref, wo_ref = wbo_ref.at[0], wbo_ref.at[1], wbo_ref.at[2]
    t = pl.program_id(0)
    block = jnp.minimum(t, pl.num_programs(0) - 2)
    block_in_batch = block % blocks_per_batch
    parity = t & 1
    last_slot = CHUNKS_PER_BLOCK - 1 + parity
    carried_slot = CHUNKS_PER_BLOCK - parity
    zeros_head = jnp.zeros((CHUNK, SWA_HEAD_DIM), BF16)

    @pl.when(t == 0)
    def _():
        ys_ref[...] = jnp.zeros_like(ys_ref)
        yr_ref[...] = jnp.zeros_like(yr_ref)

    @pl.when(block_in_batch == 0)
    def _():
        bd_ref[...] = jnp.zeros_like(bd_ref)
        vbuf_ref[...] = jnp.zeros_like(vbuf_ref)
        bdm_ref[...] = jnp.zeros_like(bdm_ref)
        vm_ref[...] = jnp.zeros_like(vm_ref)
        zm = jnp.zeros((N_META, SWA_HEAD_DIM), BF16)
        for g in range(SWA_KV_HEADS):
            cols = slice(g * SWA_HEAD_DIM, (g + 1) * SWA_HEAD_DIM)
            km = mk_ref[:, cols]
            bdm_ref[g, 0:N_META, :] = jnp.concatenate([km, zm], axis=1)
            bdm_ref[g, N_META:2 * N_META, :] = jnp.concatenate([zm, km], axis=1)
            vmeta = mv_ref[:, cols]
            vm_ref[g, 0:N_META, :] = jnp.concatenate([vmeta, zm], axis=1)
            vm_ref[g, N_META:2 * N_META, :] = jnp.concatenate([zm, vmeta], axis=1)
        for h in range(RET_HEADS):
            state_ref[h] = _dot(mkdt_ref[h * RET_QK_DIM:(h + 1) * RET_QK_DIM, :],
                                mrv_ref[:, h * RET_V_DIM:(h + 1) * RET_V_DIM])

    slabs = [slice(r, r + DENSE_ROWS) for r in range(0, BACK_ROWS, DENSE_ROWS)]
    for rs in slabs:
        merged = (_sigmoid(gr_ref[rs, :].astype(F32)) * _dot(yr_ref[rs, :], wr_ref[...])
                  + _sigmoid(gs_ref[rs, :].astype(F32)) * _dot(ys_ref[rs, :], ws_ref[...]))
        o_ref[rs, :] = x_ref[rs, :] + _dot(merged.astype(BF16), wo_ref[...])
        h1 = o_ref[rs, :]
        hn_ref[rs, :] = (h1 * lax.rsqrt(jnp.mean(h1 * h1, axis=-1, keepdims=True) + EPS)
                         * g_ref[...]).astype(BF16)

    ret_row = lax.broadcasted_iota(jnp.int32, (RET_CHUNK, RET_CHUNK), 0)
    ret_col = lax.broadcasted_iota(jnp.int32, (RET_CHUNK, RET_CHUNK), 1)
    ret_causal = ret_col <= ret_row

    def retention_head(sub, h):
        rows = slice(sub * RET_CHUNK, (sub + 1) * RET_CHUNK)
        vcols = slice(h * RET_V_DIM, (h + 1) * RET_V_DIM)
        qd = qd_ref[rows, h * RET_QK_DIM:(h + 1) * RET_QK_DIM]
        kdt = kdt_ref[h * RET_QK_DIM:(h + 1) * RET_QK_DIM, rows]
        v = rv_ref[rows, vcols]
        state = state_ref[h]
        s = jnp.where(ret_causal, _dot(qd, kdt), 0.0).astype(BF16)
        o = _dot(s, v) + _dot(qd, state.astype(BF16))
        state_ref[h] = RET_CHUNK_DECAY[h] * (state + _dot(kdt, v))
        ms = jnp.mean(o * o, axis=-1, keepdims=True)
        gate = rg_ref[rows, vcols].astype(F32)
        yr_ref[rows, vcols] = (o * lax.rsqrt(ms + EPS) * rn_ref[:, vcols]
                               * (gate * _sigmoid(gate))).astype(BF16)

    row = lax.broadcasted_iota(jnp.int32, (CHUNK, CHUNK), 0)
    col = lax.broadcasted_iota(jnp.int32, (CHUNK, CHUNK), 1)
    own = col <= row
    own2 = jnp.concatenate([own, own], axis=1)
    own4 = jnp.concatenate([own2, own2], axis=0)
    meta_seg = [col < N_META, (col >= N_META) & (col < 2 * N_META)]
    low_half = col < SWA_HEAD_DIM
    neg_inf = jnp.float32(-jnp.inf)

    first_chunk = (CHUNKS_PER_BLOCK * block_in_batch).astype(F32)
    chunks = []
    for c in range(CHUNKS_PER_BLOCK):
        slot = c if c < CHUNKS_PER_BLOCK - 1 else last_slot
        prev_slot = c - 1 if c > 0 else carried_slot
        bias_sel = jnp.minimum(block_in_batch, 1) if c == 0 else 1
        chunks.append((slice(c * CHUNK, (c + 1) * CHUNK), slot, prev_slot, bias_sel, first_chunk + float(c)))

    for rows, slot, _, _, _ in chunks:
        for g in range(SWA_KV_HEADS):
            kg = sk_ref[rows, g * SWA_HEAD_DIM:(g + 1) * SWA_HEAD_DIM]
            bd_ref[slot, g, 0:CHUNK, :] = jnp.concatenate([kg, zeros_head], axis=1)
            bd_ref[slot, g, CHUNK:2 * CHUNK, :] = jnp.concatenate([zeros_head, kg], axis=1)
            vg = sv_ref[rows, g * SWA_HEAD_DIM:(g + 1) * SWA_HEAD_DIM]
            vbuf_ref[slot, g] = jnp.concatenate([vg, vg], axis=1)

    def pair_slabs(rows, g):
        return jnp.concatenate(
            [sq_ref[rows, (2 * g + p) * LANES:(2 * g + p + 1) * LANES] for p in range(2)], axis=0)

    carried_scores = {}

    def scores(chunk, g):
        rows, slot, prev_slot, bias_sel, chunk_idx = chunk
        c = rows.start // CHUNK
        q2 = pair_slabs(rows, g)
        if c + 1 < CHUNKS_PER_BLOCK:
            both = _dot_nt(jnp.concatenate([q2, pair_slabs(chunks[c + 1][0], g)], axis=0), bd_ref[slot, g])
            s_own = both[0:PAIR_ROWS]
            carried_scores[(c + 1, g)] = both[PAIR_ROWS:2 * PAIR_ROWS]
        else:
            s_own = _dot_nt(q2, bd_ref[slot, g])
        s_prev = carried_scores.pop((c, g)) if c > 0 else _dot_nt(q2, bd_ref[prev_slot, g])
        dist = band_dist_ref[bias_sel]
        bias = jnp.concatenate(
            [jnp.concatenate([ALIBI_SLOPES[g * SWA_GROUP + 2 * p + e] * dist for e in range(2)], axis=1)
             for p in range(2)], axis=0)
        s = jnp.where(own4, s_own, s_prev) - bias
        dist_meta = meta_dist_ref[...] + float(CHUNK) * chunk_idx
        bias_meta = jnp.concatenate(
            [jnp.where(meta_seg[0], ALIBI_SLOPES[g * SWA_GROUP + 2 * p],
                       ALIBI_SLOPES[g * SWA_GROUP + 2 * p + 1]) * dist_meta for p in range(2)], axis=0)
        s_meta = _dot_nt(q2, bdm_ref[g]) - bias_meta
        return s, s_meta

    def softmax(g, s, s_meta):
        lhs = []
        lhs_meta = []
        inv = []
        for p in range(2):
            sm_p = s_meta[p * CHUNK:(p + 1) * CHUNK]
            pm_pair = None
            for e in range(2):
                h = g * SWA_GROUP + 2 * p + e
                sink = sink_ref[h]
                sb = s[p * CHUNK:(p + 1) * CHUNK, e * CHUNK:(e + 1) * CHUNK]
                sm = jnp.where(meta_seg[e], sm_p, neg_inf)
                m = jnp.maximum(jnp.max(jnp.maximum(sb, sm), axis=-1, keepdims=True), sink)
                pb = jnp.exp(sb - m)
                pm = jnp.exp(sm - m)
                denom = jnp.sum(pb + pm, axis=-1, keepdims=True) + jnp.exp(sink - m)
                inv.append(1.0 / denom)
                lhs.append(jnp.concatenate(
                    [jnp.where(own, pb, 0.0).astype(BF16), jnp.where(own, 0.0, pb).astype(BF16)], axis=1))
                pm_pair = pm if pm_pair is None else pm_pair + pm
            lhs_meta.append(pm_pair.astype(BF16))
        return jnp.concatenate(lhs, axis=0), jnp.concatenate(lhs_meta, axis=0), inv

    def weighted_values(chunk, g, lhs, lhs_meta, inv):
        rows, slot, prev_slot, _, _ = chunk
        vv = jnp.concatenate([vbuf_ref[slot, g], vbuf_ref[prev_slot, g]], axis=0)
        acc = _dot(lhs, vv)
        acc_meta = _dot(lhs_meta, vm_ref[g])
        for p in range(2):
            first, second = 2 * p, 2 * p + 1
            pair = jnp.where(low_half, acc[first * CHUNK:(first + 1) * CHUNK],
                             acc[second * CHUNK:(second + 1) * CHUNK])
            scale = jnp.where(low_half, inv[first], inv[second])
            tile = (pair + acc_meta[p * CHUNK:(p + 1) * CHUNK]) * scale
            ys_ref[rows, (2 * g + p) * LANES:(2 * g + p + 1) * LANES] = tile.astype(BF16)

    tasks = [(chunk, g) for chunk in chunks for g in range(SWA_KV_HEADS)]
    ret_tasks = [(sub, h) for sub in range(RET_PER_BLOCK) for h in range(RET_HEADS)]
    assert len(tasks) == 2 * len(ret_tasks)

    def mlp_phase(k):
        c = k * FF_CHUNK
        for rs in slabs:
            u = jnp.maximum(_dot(hn_ref[rs, :], wu_ref[:, c:c + FF_CHUNK]), 0.0)
            o_ref[rs, :] += _dot((u * u).astype(BF16), wd_ref[c:c + FF_CHUNK, :])

    n_mlp_phases = D_FF // FF_CHUNK
    phase_at_task = {i * len(tasks) // n_mlp_phases: i for i in range(n_mlp_phases)}

    pending_scores = scores(*tasks[0])
    pending_probs = None
    for i, (chunk, g) in enumerate(tasks):
        current_scores = pending_scores
        if i + 1 < len(tasks):
            pending_scores = scores(*tasks[i + 1])
        if i in phase_at_task:
            mlp_phase(phase_at_task[i])
        probs = softmax(g, *current_scores)
        if i % 2 == 0:
            retention_head(*ret_tasks[i // 2])
        if pending_probs is not None:
            weighted_values(*tasks[i - 1], *pending_probs)
        pending_probs = probs
    weighted_values(*tasks[-1], *pending_probs)


def _back(x2d, mix2d, kdt, gates2d, meta_mix, meta_kdt, sinks, norm_gains, w_branch_out, w_up, w_down,
          blocks_per_batch):
    rows = x2d.shape[0]
    assert rows % BACK_ROWS == 0
    n_blocks = rows // BACK_ROWS
    _, band_dist, meta_dist = _swa_consts()

    cur = lambda t: jnp.minimum(t, n_blocks - 1)
    prev = lambda t: jnp.maximum(t - 1, 0)

    def resident(shape):
        zeros = (0,) * len(shape)
        return pl.BlockSpec(shape, lambda t, s: zeros, pipeline_mode=pl.Buffered(1))

    grid_spec = pltpu.PrefetchScalarGridSpec(
        num_scalar_prefetch=1,
        grid=(n_blocks + 1,),
        in_specs=[
            pl.BlockSpec((BACK_ROWS, MIX_W), lambda t, s: (cur(t), 0)),
            pl.BlockSpec((None, RET_QK_W, BACK_ROWS),
                         lambda t, s: (cur(t) // blocks_per_batch, 0, cur(t) % blocks_per_batch)),
            pl.BlockSpec((BACK_ROWS, RET_V_W), lambda t, s: (cur(t), GATE_RG // RET_V_W)),
            resident((1, CHUNK, MIX_W)),
            resident((1, RET_QK_W, CHUNK)),
            resident((3, CHUNK, CHUNK)),
            resident((2, D_MODEL)),
            pl.BlockSpec((BACK_ROWS, D_MODEL), lambda t, s: (prev(t), 0)),
            pl.BlockSpec((BACK_ROWS, 2 * D_MODEL), lambda t, s: (prev(t), 0)),
            resident((3, D_MODEL, D_MODEL)),
            resident((D_MODEL, D_FF)),
            resident((D_FF, D_MODEL)),
        ],
        out_specs=pl.BlockSpec((BACK_ROWS, D_MODEL), lambda t, s: (prev(t), 0)),
        scratch_shapes=[
            pltpu.VMEM((KV_SLOTS, SWA_KV_HEADS, PAIR_ROWS, LANES), BF16),
            pltpu.VMEM((KV_SLOTS, SWA_KV_HEADS, CHUNK, LANES), BF16),
            pltpu.VMEM((SWA_KV_HEADS, CHUNK, LANES), BF16),
            pltpu.VMEM((SWA_KV_HEADS, CHUNK, LANES), BF16),
            pltpu.VMEM((RET_HEADS, RET_QK_DIM, RET_V_DIM), F32),
            pltpu.VMEM((BACK_ROWS, SWA_Q_W), BF16),
            pltpu.VMEM((BACK_ROWS, RET_V_W), BF16),
            pltpu.VMEM((BACK_ROWS, D_MODEL), BF16),
        ],
    )
    return pl.pallas_call(
        functools.partial(_back_kernel, blocks_per_batch),
        out_shape=jax.ShapeDtypeStruct((rows, D_MODEL), F32),
        grid_spec=grid_spec,
        compiler_params=pltpu.CompilerParams(
            dimension_semantics=("arbitrary",),
            vmem_limit_bytes=V7X_VMEM_LIMIT_BYTES,
        ),
        name="back",
    )(sinks.astype(F32), mix2d, kdt, gates2d, meta_mix, meta_kdt,
      jnp.asarray(np.concatenate([band_dist, meta_dist[None]])), norm_gains,
      x2d, gates2d, w_branch_out, w_up, w_down)


def kernel(x, meta_tokens, mix_norm_g, w_in, ret_norm_g, q_norm_g, k_norm_g, sinks,
           w_ret_branch, w_swa_branch, w_out, mlp_norm_g, w_up, w_down):
    b, l, d = x.shape
    assert d == D_MODEL and l % BLOCK_ROWS == 0 and mix_norm_g.shape[0] == 1
    mix_g = mix_norm_g[0].reshape(1, d).astype(F32)
    w_in_b = w_in[0].astype(BF16)
    qg2 = jnp.tile(q_norm_g[0].astype(F32) * (SWA_HEAD_DIM ** -0.5), 2).reshape(1, LANES)
    kg2 = jnp.tile(k_norm_g[0].astype(F32), 2).reshape(1, LANES)

    mix, kdt, gates, w_branch_out, w_up_b, w_down_b = _inproj(
        x, mix_g, qg2, kg2, jnp.asarray(_decay_tables(BLOCK_ROWS)), w_in_b, BLOCK_ROWS,
        cast_weights=(w_ret_branch, w_swa_branch, w_out, w_up, w_down))
    meta_chunk = jnp.pad(meta_tokens.astype(F32), ((PAD, 0), (0, 0)))[None]
    meta_mix, meta_kdt, _ = _inproj(meta_chunk, mix_g, qg2, kg2, jnp.asarray(_meta_decay_tables()),
                                    w_in_b, CHUNK)

    norm_gains = jnp.stack([ret_norm_g[0].reshape(RET_V_W), mlp_norm_g[0]]).astype(F32)
    out = _back(
        x.reshape(b * l, d), mix.reshape(b * l, MIX_W), kdt, gates.reshape(b * l, GATE_W),
        meta_mix, meta_kdt, sinks[0], norm_gains, w_branch_out, w_up_b, w_down_b, l // BACK_ROWS)
    return out.reshape(b, l, d)
```

```python
import functools

import numpy as np
import jax
import jax.numpy as jnp
from jax import lax
from jax.experimental import pallas as pl
from jax.experimental.pallas import tpu as pltpu

D_MODEL = 1024
N_META = 16
CHUNK = 128
PAD = CHUNK - N_META
RET_HEADS = 4
RET_QK_DIM = 128
RET_V_DIM = 256
SWA_Q_HEADS = 16
SWA_KV_HEADS = 4
SWA_GROUP = SWA_Q_HEADS // SWA_KV_HEADS
SWA_HEAD_DIM = 64
D_FF = 4 * D_MODEL
EPS = 1e-6
RET_QK_W = RET_HEADS * RET_QK_DIM
RET_V_W = RET_HEADS * RET_V_DIM
SWA_Q_W = SWA_Q_HEADS * SWA_HEAD_DIM
SWA_KV_W = SWA_KV_HEADS * SWA_HEAD_DIM
D_IN = 2 * RET_QK_W + 2 * RET_V_W + SWA_Q_W + 2 * SWA_KV_W + 2 * D_MODEL

OFF_RQ = 0
OFF_RK = OFF_RQ + RET_QK_W
OFF_RV = OFF_RK + RET_QK_W
OFF_RG = OFF_RV + RET_V_W
OFF_SQ = OFF_RG + RET_V_W
OFF_SK = OFF_SQ + SWA_Q_W
OFF_SV = OFF_SK + SWA_KV_W
OFF_GR = OFF_SV + SWA_KV_W
OFF_GS = OFF_GR + D_MODEL

MIX_SQ = 0
MIX_RV = MIX_SQ + SWA_Q_W
MIX_RQ = MIX_RV + RET_V_W
MIX_SK = MIX_RQ + RET_QK_W
MIX_SV = MIX_SK + SWA_KV_W
MIX_W = MIX_SV + SWA_KV_W
GATE_RG = 2 * D_MODEL
GATE_W = GATE_RG + RET_V_W

V7X_VMEM_LIMIT_BYTES = 64 * 1024 * 1024
LANES = 128
assert LANES == 2 * SWA_HEAD_DIM and CHUNK == LANES and RET_V_W == D_MODEL

F32 = jnp.float32
BF16 = jnp.bfloat16

BLOCK_ROWS = 1024
RET_CHUNK = 256
assert BLOCK_ROWS % RET_CHUNK == 0 and BLOCK_ROWS % CHUNK == 0


def _sigmoid(x):
    return 1.0 / (1.0 + jnp.exp(-x))


def _dot(a, b):
    return jnp.dot(a, b, preferred_element_type=F32)


def _dot_nt(a, b):
    return lax.dot_general(a, b, (((1,), (1,)), ((), ())), preferred_element_type=F32)


def _retention_log_gamma():
    return np.log1p(-(2.0 ** (-5.0 - np.arange(RET_HEADS, dtype=np.float64))))


IN_N_CHUNK = 512


def _head_pair_rms(y, g2):
    rows, width = y.shape
    low = lax.broadcasted_iota(jnp.int32, (rows, LANES), 1) < SWA_HEAD_DIM
    out = []
    for c in range(0, width, LANES):
        yb = y[:, c:c + LANES]
        y2 = yb * yb
        ms_lo = jnp.sum(jnp.where(low, y2, 0.0), axis=-1, keepdims=True) * (1.0 / SWA_HEAD_DIM)
        ms_hi = jnp.sum(jnp.where(low, 0.0, y2), axis=-1, keepdims=True) * (1.0 / SWA_HEAD_DIM)
        scale = jnp.where(low, lax.rsqrt(ms_lo + EPS), lax.rsqrt(ms_hi + EPS))
        out.append(yb * scale * g2)
    return jnp.concatenate(out, axis=1) if len(out) > 1 else out[0]


def _inproj_kernel(x_ref, g_ref, qg_ref, kg_ref, dec_ref, w_ref, mix_ref, kdt_ref, gate_ref):
    x = x_ref[...]
    ms = jnp.mean(x * x, axis=-1, keepdims=True)
    hn = (x * lax.rsqrt(ms + EPS) * g_ref[...]).astype(BF16)

    def proj(lo, width):
        return _dot(hn, w_ref[:, lo:lo + width])

    for c in range(0, SWA_Q_W, IN_N_CHUNK):
        mix_ref[:, MIX_SQ + c:MIX_SQ + c + IN_N_CHUNK] = _head_pair_rms(
            proj(OFF_SQ + c, IN_N_CHUNK), qg_ref[...]).astype(BF16)
    for c in range(0, RET_V_W, IN_N_CHUNK):
        mix_ref[:, MIX_RV + c:MIX_RV + c + IN_N_CHUNK] = proj(OFF_RV + c, IN_N_CHUNK).astype(BF16)
    mix_ref[:, MIX_RQ:MIX_RQ + RET_QK_W] = (proj(OFF_RQ, RET_QK_W) * dec_ref[:, 0:RET_QK_W]).astype(BF16)
    kd = proj(OFF_RK, RET_QK_W) * dec_ref[:, RET_QK_W:2 * RET_QK_W]
    kdt_ref[...] = kd.T.astype(BF16)
    mix_ref[:, MIX_SK:MIX_SK + SWA_KV_W] = _head_pair_rms(proj(OFF_SK, SWA_KV_W), kg_ref[...]).astype(BF16)
    mix_ref[:, MIX_SV:MIX_SV + SWA_KV_W] = proj(OFF_SV, SWA_KV_W).astype(BF16)
    for c in range(0, 2 * D_MODEL, IN_N_CHUNK):
        gate_ref[:, c:c + IN_N_CHUNK] = proj(OFF_GR + c, IN_N_CHUNK).astype(BF16)
    for c in range(0, RET_V_W, IN_N_CHUNK):
        gate_ref[:, GATE_RG + c:GATE_RG + c + IN_N_CHUNK] = proj(OFF_RG + c, IN_N_CHUNK).astype(BF16)


def _inproj_cast_kernel(x_ref, g_ref, qg_ref, kg_ref, dec_ref, w_ref,
                        wr_ref, ws_ref, wo_ref, wu_ref, wd_ref,
                        mix_ref, kdt_ref, gate_ref, wbo_out_ref, wu_out_ref, wd_out_ref):
    _inproj_kernel(x_ref, g_ref, qg_ref, kg_ref, dec_ref, w_ref, mix_ref, kdt_ref, gate_ref)
    for k, src in enumerate((wr_ref, ws_ref, wo_ref)):
        wbo_out_ref[k] = src[...].astype(BF16)
    wu_out_ref[...] = wu_ref[...].astype(BF16)
    wd_out_ref[...] = wd_ref[...].astype(BF16)


BF16_ROW_TILE = 16


def _inproj(x3d, g, qg2, kg2, dec, w_bf16, block_rows, cast_weights=None):
    b, seq, _ = x3d.shape
    assert seq % block_rows == 0 and dec.shape == (block_rows, 2 * RET_QK_W)
    steps = seq // block_rows
    n_steps = b * steps
    const = lambda shape: pl.BlockSpec(shape, lambda i: (0, 0))
    out_shape = [jax.ShapeDtypeStruct((b, seq, MIX_W), BF16),
                 jax.ShapeDtypeStruct((b, RET_QK_W, seq), BF16),
                 jax.ShapeDtypeStruct((b, seq, GATE_W), BF16)]
    out_specs = [pl.BlockSpec((None, block_rows, MIX_W), lambda i: (i // steps, i % steps, 0)),
                 pl.BlockSpec((None, RET_QK_W, block_rows), lambda i: (i // steps, 0, i % steps)),
                 pl.BlockSpec((None, block_rows, GATE_W), lambda i: (i // steps, i % steps, 0))]
    extra_in, extra_specs = [], []
    if cast_weights is not None:
        band = lambda w: w.shape[1] // n_steps
        for w in cast_weights:
            assert w.shape[1] % n_steps == 0 and band(w) % BF16_ROW_TILE == 0
            extra_in.append(w)
            extra_specs.append(pl.BlockSpec((None, band(w), w.shape[2]), lambda i: (0, i, 0)))
        w_r, _, _, w_u, w_d = cast_weights
        out_shape += [jax.ShapeDtypeStruct((3,) + w_r.shape[1:], BF16),
                      jax.ShapeDtypeStruct(w_u.shape[1:], BF16),
                      jax.ShapeDtypeStruct(w_d.shape[1:], BF16)]
        out_specs += [pl.BlockSpec((3, band(w_r), w_r.shape[2]), lambda i: (0, i, 0)),
                      pl.BlockSpec((band(w_u), w_u.shape[2]), lambda i: (i, 0)),
                      pl.BlockSpec((band(w_d), w_d.shape[2]), lambda i: (i, 0))]
    return pl.pallas_call(
        _inproj_kernel if cast_weights is None else _inproj_cast_kernel,
        out_shape=tuple(out_shape),
        grid=(n_steps,),
        in_specs=[
            pl.BlockSpec((None, block_rows, D_MODEL), lambda i: (i // steps, i % steps, 0)),
            const((1, D_MODEL)),
            const((1, LANES)),
            const((1, LANES)),
            const((block_rows, 2 * RET_QK_W)),
            pl.BlockSpec((D_MODEL, D_IN), lambda i: (0, 0), pipeline_mode=pl.Buffered(1)),
        ] + extra_specs,
        out_specs=tuple(out_specs),
        compiler_params=pltpu.CompilerParams(
            dimension_semantics=("arbitrary",),
            vmem_limit_bytes=V7X_VMEM_LIMIT_BYTES,
        ),
        name="inproj",
    )(x3d, g, qg2, kg2, dec, w_bf16, *extra_in)


def _decay_tables(block_rows):
    lg = _retention_log_gamma()
    pos = (np.arange(block_rows) % RET_CHUNK).astype(np.float64) + 1.0
    qd = np.exp(lg[None, :] * pos[:, None])
    kd = np.exp(-lg[None, :] * pos[:, None]) * (RET_QK_DIM ** -0.5)
    rep = lambda t: np.repeat(t, RET_QK_DIM, axis=1)
    return np.concatenate([rep(qd), rep(kd)], axis=1).astype(np.float32)


def _meta_decay_tables():
    lg = _retention_log_gamma()
    pos = np.arange(CHUNK, dtype=np.float64)
    kd = np.exp(lg[None, :] * (CHUNK - 1.0 - pos[:, None])) * (RET_QK_DIM ** -0.5)
    rep = lambda t: np.repeat(t, RET_QK_DIM, axis=1)
    return np.concatenate([np.ones((CHUNK, RET_QK_W)), rep(kd)], axis=1).astype(np.float32)


RET_CHUNK_DECAY = [float(c) for c in np.exp(_retention_log_gamma() * RET_CHUNK)]
PAIR_ROWS = 2 * CHUNK
BACK_ROWS = 512
DENSE_ROWS = 512
assert BACK_ROWS % RET_CHUNK == 0 and BLOCK_ROWS % BACK_ROWS == 0 and BACK_ROWS % DENSE_ROWS == 0
CHUNKS_PER_BLOCK = BACK_ROWS // CHUNK
RET_PER_BLOCK = BACK_ROWS // RET_CHUNK
KV_SLOTS = CHUNKS_PER_BLOCK + 1
FF_CHUNK = 512


def _swa_consts():
    slopes = 2.0 ** (-8.0 * np.arange(1, SWA_Q_HEADS + 1, dtype=np.float64) / SWA_Q_HEADS)
    slopes = slopes.astype(np.float32)
    i = np.arange(CHUNK)[:, None]
    j = np.arange(CHUNK)[None, :]
    band_dist = np.where(j <= i, i - j, i - j + CHUNK).astype(np.float32)
    meta_dist = np.where(j < 2 * N_META, i - (j % N_META) + N_META, np.inf).astype(np.float32)
    band_dist = np.stack([np.where(j <= i, band_dist, np.inf), band_dist]).astype(np.float32)
    return [float(s) for s in slopes], band_dist, meta_dist


ALIBI_SLOPES = _swa_consts()[0]


def _back_kernel(blocks_per_batch,
                 sink_ref,
                 mix_ref, kdt_ref, rg_ref, meta_ref, mkdt_ref, dist_ref, gains_ref,
                 x_ref, gate_ref, wbo_ref, wu_ref, wd_ref,
                 o_ref,
                 bd_ref, vbuf_ref, bdm_ref, vm_ref, state_ref, ys_ref, yr_ref, hn_ref):
    sq_ref = mix_ref.at[:, MIX_SQ:MIX_SQ + SWA_Q_W]
    rv_ref = mix_ref.at[:, MIX_RV:MIX_RV + RET_V_W]
    qd_ref = mix_ref.at[:, MIX_RQ:MIX_RQ + RET_QK_W]
    sk_ref = mix_ref.at[:, MIX_SK:MIX_SK + SWA_KV_W]
    sv_ref = mix_ref.at[:, MIX_SV:MIX_SV + SWA_KV_W]
    mk_ref = meta_ref.at[0, PAD:CHUNK, MIX_SK:MIX_SK + SWA_KV_W]
    mv_ref = meta_ref.at[0, PAD:CHUNK, MIX_SV:MIX_SV + SWA_KV_W]
    mrv_ref = meta_ref.at[0, :, MIX_RV:MIX_RV + RET_V_W]
    mkdt_ref = mkdt_ref.at[0]
    band_dist_ref = dist_ref.at[0:2]
    meta_dist_ref = dist_ref.at[2]
    rn_ref = gains_ref.at[0:1, :]
    g_ref = gains_ref.at[1:2, :]
    gr_ref = gate_ref.at[:, 0:D_MODEL]
    gs_ref = gate_ref.at[:, D_MODEL:2 * D_MODEL]
    wr_ref, ws_ref, wo_ref = wbo_ref.at[0], wbo_ref.at[1], wbo_ref.at[2]
    t = pl.program_id(0)
    block = jnp.minimum(t, pl.num_programs(0) - 2)
    block_in_batch = block % blocks_per_batch
    parity = t & 1
    last_slot = CHUNKS_PER_BLOCK - 1 + parity
    carried_slot = CHUNKS_PER_BLOCK - parity
    zeros_head = jnp.zeros((CHUNK, SWA_HEAD_DIM), BF16)

    @pl.when(t == 0)
    def _():
        ys_ref[...] = jnp.zeros_like(ys_ref)
        yr_ref[...] = jnp.zeros_like(yr_ref)

    @pl.when(block_in_batch == 0)
    def _():
        bd_ref[...] = jnp.zeros_like(bd_ref)
        vbuf_ref[...] = jnp.zeros_like(vbuf_ref)
        bdm_ref[...] = jnp.zeros_like(bdm_ref)
        vm_ref[...] = jnp.zeros_like(vm_ref)
        zm = jnp.zeros((N_META, SWA_HEAD_DIM), BF16)
        for g in range(SWA_KV_HEADS):
            cols = slice(g * SWA_HEAD_DIM, (g + 1) * SWA_HEAD_DIM)
            km = mk_ref[:, cols]
            bdm_ref[g, 0:N_META, :] = jnp.concatenate([km, zm], axis=1)
            bdm_ref[g, N_META:2 * N_META, :] = jnp.concatenate([zm, km], axis=1)
            vmeta = mv_ref[:, cols]
            vm_ref[g, 0:N_META, :] = jnp.concatenate([vmeta, zm], axis=1)
            vm_ref[g, N_META:2 * N_META, :] = jnp.concatenate([zm, vmeta], axis=1)
        for h in range(RET_HEADS):
            state_ref[h] = _dot(mkdt_ref[h * RET_QK_DIM:(h + 1) * RET_QK_DIM, :],
                                mrv_ref[:, h * RET_V_DIM:(h + 1) * RET_V_DIM])

    slabs = [slice(r, r + DENSE_ROWS) for r in range(0, BACK_ROWS, DENSE_ROWS)]
    for rs in slabs:
        merged = (_sigmoid(gr_ref[rs, :].astype(F32)) * _dot(yr_ref[rs, :], wr_ref[...])
                  + _sigmoid(gs_ref[rs, :].astype(F32)) * _dot(ys_ref[rs, :], ws_ref[...]))
        o_ref[rs, :] = x_ref[rs, :] + _dot(merged.astype(BF16), wo_ref[...])
        h1 = o_ref[rs, :]
        hn_ref[rs, :] = (h1 * lax.rsqrt(jnp.mean(h1 * h1, axis=-1, keepdims=True) + EPS)
                         * g_ref[...]).astype(BF16)

    ret_row = lax.broadcasted_iota(jnp.int32, (RET_CHUNK, RET_CHUNK), 0)
    ret_col = lax.broadcasted_iota(jnp.int32, (RET_CHUNK, RET_CHUNK), 1)
    ret_causal = ret_col <= ret_row

    def retention_head(sub, h):
        rows = slice(sub * RET_CHUNK, (sub + 1) * RET_CHUNK)
        vcols = slice(h * RET_V_DIM, (h + 1) * RET_V_DIM)
        qd = qd_ref[rows, h * RET_QK_DIM:(h + 1) * RET_QK_DIM]
        kdt = kdt_ref[h * RET_QK_DIM:(h + 1) * RET_QK_DIM, rows]
        v = rv_ref[rows, vcols]
        state = state_ref[h]
        s = jnp.where(ret_causal, _dot(qd, kdt), 0.0).astype(BF16)
        o = _dot(s, v) + _dot(qd, state.astype(BF16))
        state_ref[h] = RET_CHUNK_DECAY[h] * (state + _dot(kdt, v))
        ms = jnp.mean(o * o, axis=-1, keepdims=True)
        gate = rg_ref[rows, vcols].astype(F32)
        yr_ref[rows, vcols] = (o * lax.rsqrt(ms + EPS) * rn_ref[:, vcols]
                               * (gate * _sigmoid(gate))).astype(BF16)

    row = lax.broadcasted_iota(jnp.int32, (CHUNK, CHUNK), 0)
    col = lax.broadcasted_iota(jnp.int32, (CHUNK, CHUNK), 1)
    own = col <= row
    own2 = jnp.concatenate([own, own], axis=1)
    own4 = jnp.concatenate([own2, own2], axis=0)
    meta_seg = [col < N_META, (col >= N_META) & (col < 2 * N_META)]
    low_half = col < SWA_HEAD_DIM
    neg_inf = jnp.float32(-jnp.inf)

    first_chunk = (CHUNKS_PER_BLOCK * block_in_batch).astype(F32)
    chunks = []
    for c in range(CHUNKS_PER_BLOCK):
        slot = c if c < CHUNKS_PER_BLOCK - 1 else last_slot
        prev_slot = c - 1 if c > 0 else carried_slot
        bias_sel = jnp.minimum(block_in_batch, 1) if c == 0 else 1
        chunks.append((slice(c * CHUNK, (c + 1) * CHUNK), slot, prev_slot, bias_sel, first_chunk + float(c)))

    for rows, slot, _, _, _ in chunks:
        for g in range(SWA_KV_HEADS):
            kg = sk_ref[rows, g * SWA_HEAD_DIM:(g + 1) * SWA_HEAD_DIM]
            bd_ref[slot, g, 0:CHUNK, :] = jnp.concatenate([kg, zeros_head], axis=1)
            bd_ref[slot, g, CHUNK:2 * CHUNK, :] = jnp.concatenate([zeros_head, kg], axis=1)
            vg = sv_ref[rows, g * SWA_HEAD_DIM:(g + 1) * SWA_HEAD_DIM]
            vbuf_ref[slot, g] = jnp.concatenate([vg, vg], axis=1)

    def scores(chunk, g):
        rows, slot, prev_slot, bias_sel, chunk_idx = chunk
        q2 = jnp.concatenate(
            [sq_ref[rows, (2 * g + p) * LANES:(2 * g + p + 1) * LANES] for p in range(2)],
            axis=0)
        s_own = _dot_nt(q2, bd_ref[slot, g])
        s_prev = _dot_nt(q2, bd_ref[prev_slot, g])
        dist = band_dist_ref[bias_sel]
        bias = jnp.concatenate(
            [jnp.concatenate([ALIBI_SLOPES[g * SWA_GROUP + 2 * p + e] * dist for e in range(2)], axis=1)
             for p in range(2)], axis=0)
        s = jnp.where(own4, s_own, s_prev) - bias
        dist_meta = meta_dist_ref[...] + float(CHUNK) * chunk_idx
        bias_meta = jnp.concatenate(
            [jnp.where(meta_seg[0], ALIBI_SLOPES[g * SWA_GROUP + 2 * p],
                       ALIBI_SLOPES[g * SWA_GROUP + 2 * p + 1]) * dist_meta for p in range(2)], axis=0)
        s_meta = _dot_nt(q2, bdm_ref[g]) - bias_meta
        return s, s_meta

    def softmax(g, s, s_meta):
        lhs = []
        lhs_meta = []
        inv = []
        for p in range(2):
            sm_p = s_meta[p * CHUNK:(p + 1) * CHUNK]
            pm_pair = None
            for e in range(2):
                h = g * SWA_GROUP + 2 * p + e
                sink = sink_ref[h]
                sb = s[p * CHUNK:(p + 1) * CHUNK, e * CHUNK:(e + 1) * CHUNK]
                sm = jnp.where(meta_seg[e], sm_p, neg_inf)
                m = jnp.maximum(jnp.max(jnp.maximum(sb, sm), axis=-1, keepdims=True), sink)
                pb = jnp.exp(sb - m)
                pm = jnp.exp(sm - m)
                denom = jnp.sum(pb + pm, axis=-1, keepdims=True) + jnp.exp(sink - m)
                inv.append(1.0 / denom)
                lhs.append(jnp.concatenate(
                    [jnp.where(own, pb, 0.0).astype(BF16), jnp.where(own, 0.0, pb).astype(BF16)], axis=1))
                pm_pair = pm if pm_pair is None else pm_pair + pm
            lhs_meta.append(pm_pair.astype(BF16))
        return jnp.concatenate(lhs, axis=0), jnp.concatenate(lhs_meta, axis=0), inv

    def weighted_values(chunk, g, lhs, lhs_meta, inv):
        rows, slot, prev_slot, _, _ = chunk
        vv = jnp.concatenate([vbuf_ref[slot, g], vbuf_ref[prev_slot, g]], axis=0)
        acc = _dot(lhs, vv)
        acc_meta = _dot(lhs_meta, vm_ref[g])
        for p in range(2):
            first, second = 2 * p, 2 * p + 1
            pair = jnp.where(low_half, acc[first * CHUNK:(first + 1) * CHUNK],
                             acc[second * CHUNK:(second + 1) * CHUNK])
            scale = jnp.where(low_half, inv[first], inv[second])
            tile = (pair + acc_meta[p * CHUNK:(p + 1) * CHUNK]) * scale
            ys_ref[rows, (2 * g + p) * LANES:(2 * g + p + 1) * LANES] = tile.astype(BF16)

    tasks = [(chunk, g) for chunk in chunks for g in range(SWA_KV_HEADS)]
    ret_tasks = [(sub, h) for sub in range(RET_PER_BLOCK) for h in range(RET_HEADS)]
    assert len(tasks) == 2 * len(ret_tasks)

    def mlp_phase(k):
        c = k * FF_CHUNK
        for rs in slabs:
            u = jnp.maximum(_dot(hn_ref[rs, :], wu_ref[:, c:c + FF_CHUNK]), 0.0)
            o_ref[rs, :] += _dot((u * u).astype(BF16), wd_ref[c:c + FF_CHUNK, :])

    n_mlp_phases = D_FF // FF_CHUNK
    phase_at_task = {i * len(tasks) // n_mlp_phases: i for i in range(n_mlp_phases)}

    pending_scores = scores(*tasks[0])
    pending_probs = None
    for i, (chunk, g) in enumerate(tasks):
        current_scores = pending_scores
        if i + 1 < len(tasks):
            pending_scores = scores(*tasks[i + 1])
        if i in phase_at_task:
            mlp_phase(phase_at_task[i])
        probs = softmax(g, *current_scores)
        if i % 2 == 0:
            retention_head(*ret_tasks[i // 2])
        if pending_probs is not None:
            weighted_values(*tasks[i - 1], *pending_probs)
        pending_probs = probs
    weighted_values(*tasks[-1], *pending_probs)


def _back(x2d, mix2d, kdt, gates2d, meta_mix, meta_kdt, sinks, norm_gains, w_branch_out, w_up, w_down,
          blocks_per_batch):
    rows = x2d.shape[0]
    assert rows % BACK_ROWS == 0
    n_blocks = rows // BACK_ROWS
    _, band_dist, meta_dist = _swa_consts()

    cur = lambda t: jnp.minimum(t, n_blocks - 1)
    prev = lambda t: jnp.maximum(t - 1, 0)

    def resident(shape):
        zeros = (0,) * len(shape)
        return pl.BlockSpec(shape, lambda t, s: zeros, pipeline_mode=pl.Buffered(1))

    grid_spec = pltpu.PrefetchScalarGridSpec(
        num_scalar_prefetch=1,
        grid=(n_blocks + 1,),
        in_specs=[
            pl.BlockSpec((BACK_ROWS, MIX_W), lambda t, s: (cur(t), 0)),
            pl.BlockSpec((None, RET_QK_W, BACK_ROWS),
                         lambda t, s: (cur(t) // blocks_per_batch, 0, cur(t) % blocks_per_batch)),
            pl.BlockSpec((BACK_ROWS, RET_V_W), lambda t, s: (cur(t), GATE_RG // RET_V_W)),
            resident((1, CHUNK, MIX_W)),
            resident((1, RET_QK_W, CHUNK)),
            resident((3, CHUNK, CHUNK)),
            resident((2, D_MODEL)),
            pl.BlockSpec((BACK_ROWS, D_MODEL), lambda t, s: (prev(t), 0)),
            pl.BlockSpec((BACK_ROWS, 2 * D_MODEL), lambda t, s: (prev(t), 0)),
            resident((3, D_MODEL, D_MODEL)),
            resident((D_MODEL, D_FF)),
            resident((D_FF, D_MODEL)),
        ],
        out_specs=pl.BlockSpec((BACK_ROWS, D_MODEL), lambda t, s: (prev(t), 0)),
        scratch_shapes=[
            pltpu.VMEM((KV_SLOTS, SWA_KV_HEADS, PAIR_ROWS, LANES), BF16),
            pltpu.VMEM((KV_SLOTS, SWA_KV_HEADS, CHUNK, LANES), BF16),
            pltpu.VMEM((SWA_KV_HEADS, CHUNK, LANES), BF16),
            pltpu.VMEM((SWA_KV_HEADS, CHUNK, LANES), BF16),
            pltpu.VMEM((RET_HEADS, RET_QK_DIM, RET_V_DIM), F32),
            pltpu.VMEM((BACK_ROWS, SWA_Q_W), BF16),
            pltpu.VMEM((BACK_ROWS, RET_V_W), BF16),
            pltpu.VMEM((BACK_ROWS, D_MODEL), BF16),
        ],
    )
    return pl.pallas_call(
        functools.partial(_back_kernel, blocks_per_batch),
        out_shape=jax.ShapeDtypeStruct((rows, D_MODEL), F32),
        grid_spec=grid_spec,
        compiler_params=pltpu.CompilerParams(
            dimension_semantics=("arbitrary",),
            vmem_limit_bytes=V7X_VMEM_LIMIT_BYTES,
        ),
        name="back",
    )(sinks.astype(F32), mix2d, kdt, gates2d, meta_mix, meta_kdt,
      jnp.asarray(np.concatenate([band_dist, meta_dist[None]])), norm_gains,
      x2d, gates2d, w_branch_out, w_up, w_down)


def kernel(x, meta_tokens, mix_norm_g, w_in, ret_norm_g, q_norm_g, k_norm_g, sinks,
           w_ret_branch, w_swa_branch, w_out, mlp_norm_g, w_up, w_down):
    b, l, d = x.shape
    assert d == D_MODEL and l % BLOCK_ROWS == 0 and mix_norm_g.shape[0] == 1
    mix_g = mix_norm_g[0].reshape(1, d).astype(F32)
    w_in_b = w_in[0].astype(BF16)
    qg2 = jnp.tile(q_norm_g[0].astype(F32) * (SWA_HEAD_DIM ** -0.5), 2).reshape(1, LANES)
    kg2 = jnp.tile(k_norm_g[0].astype(F32), 2).reshape(1, LANES)

    mix, kdt, gates, w_branch_out, w_up_b, w_down_b = _inproj(
        x, mix_g, qg2, kg2, jnp.asarray(_decay_tables(BLOCK_ROWS)), w_in_b, BLOCK_ROWS,
        cast_weights=(w_ret_branch, w_swa_branch, w_out, w_up, w_down))
    meta_chunk = jnp.pad(meta_tokens.astype(F32), ((PAD, 0), (0, 0)))[None]
    meta_mix, meta_kdt, _ = _inproj(meta_chunk, mix_g, qg2, kg2, jnp.asarray(_meta_decay_tables()),
                                    w_in_b, CHUNK)

    norm_gains = jnp.stack([ret_norm_g[0].reshape(RET_V_W), mlp_norm_g[0]]).astype(F32)
    out = _back(
        x.reshape(b * l, d), mix.reshape(b * l, MIX_W), kdt, gates.reshape(b * l, GATE_W),
        meta_mix, meta_kdt, sinks[0], norm_gains, w_branch_out, w_up_b, w_down_b, l // BACK_ROWS)
    return out.reshape(b, l, d)
```

```python
import functools

import numpy as np
import jax
import jax.numpy as jnp
from jax import lax
from jax.experimental import pallas as pl
from jax.experimental.pallas import tpu as pltpu

D_MODEL = 1024
N_META = 16
CHUNK = 128
PAD = CHUNK - N_META
RET_HEADS = 4
RET_QK_DIM = 128
RET_V_DIM = 256
SWA_Q_HEADS = 16
SWA_KV_HEADS = 4
SWA_GROUP = SWA_Q_HEADS // SWA_KV_HEADS
SWA_HEAD_DIM = 64
D_FF = 4 * D_MODEL
EPS = 1e-6
RET_QK_W = RET_HEADS * RET_QK_DIM
RET_V_W = RET_HEADS * RET_V_DIM
SWA_Q_W = SWA_Q_HEADS * SWA_HEAD_DIM
SWA_KV_W = SWA_KV_HEADS * SWA_HEAD_DIM
D_IN = 2 * RET_QK_W + 2 * RET_V_W + SWA_Q_W + 2 * SWA_KV_W + 2 * D_MODEL

OFF_RQ = 0
OFF_RK = OFF_RQ + RET_QK_W
OFF_RV = OFF_RK + RET_QK_W
OFF_RG = OFF_RV + RET_V_W
OFF_SQ = OFF_RG + RET_V_W
OFF_SK = OFF_SQ + SWA_Q_W
OFF_SV = OFF_SK + SWA_KV_W
OFF_GR = OFF_SV + SWA_KV_W
OFF_GS = OFF_GR + D_MODEL
assert OFF_GS + D_MODEL == D_IN

MIX_SQ = 0
MIX_RV = MIX_SQ + SWA_Q_W
MIX_RQ = MIX_RV + RET_V_W
MIX_SK = MIX_RQ + RET_QK_W
MIX_SV = MIX_SK + SWA_KV_W
MIX_W = MIX_SV + SWA_KV_W
GATE_RG = 2 * D_MODEL
GATE_W = GATE_RG + RET_V_W

V7X_VMEM_LIMIT_BYTES = 64 * 1024 * 1024
LANES = 128
assert LANES == 2 * SWA_HEAD_DIM and CHUNK == LANES and RET_V_W == D_MODEL

F32 = jnp.float32
BF16 = jnp.bfloat16

BLOCK_ROWS = 1024
RET_CHUNK = 256
assert BLOCK_ROWS % RET_CHUNK == 0 and BLOCK_ROWS % CHUNK == 0


def _sigmoid(x):
    return 1.0 / (1.0 + jnp.exp(-x))


def _dot(a, b):
    return jnp.dot(a, b, preferred_element_type=F32)


def _dot_nt(a, b):
    return lax.dot_general(a, b, (((1,), (1,)), ((), ())), preferred_element_type=F32)


def _retention_log_gamma():
    return np.log1p(-(2.0 ** (-5.0 - np.arange(RET_HEADS, dtype=np.float64))))


IN_N_CHUNK = 512


def _head_pair_rms(y, g2):
    rows, width = y.shape
    low = lax.broadcasted_iota(jnp.int32, (rows, LANES), 1) < SWA_HEAD_DIM
    out = []
    for c in range(0, width, LANES):
        yb = y[:, c:c + LANES]
        y2 = yb * yb
        ms_lo = jnp.sum(jnp.where(low, y2, 0.0), axis=-1, keepdims=True) * (1.0 / SWA_HEAD_DIM)
        ms_hi = jnp.sum(jnp.where(low, 0.0, y2), axis=-1, keepdims=True) * (1.0 / SWA_HEAD_DIM)
        scale = jnp.where(low, lax.rsqrt(ms_lo + EPS), lax.rsqrt(ms_hi + EPS))
        out.append(yb * scale * g2)
    return jnp.concatenate(out, axis=1) if len(out) > 1 else out[0]


def _inproj_kernel(x_ref, g_ref, qg_ref, kg_ref, dec_ref, w_ref, mix_ref, kdt_ref, gate_ref):
    x = x_ref[...]
    ms = jnp.mean(x * x, axis=-1, keepdims=True)
    hn = (x * lax.rsqrt(ms + EPS) * g_ref[...]).astype(BF16)

    def proj(lo, width):
        return _dot(hn, w_ref[:, lo:lo + width])

    for c in range(0, SWA_Q_W, IN_N_CHUNK):
        mix_ref[:, MIX_SQ + c:MIX_SQ + c + IN_N_CHUNK] = _head_pair_rms(
            proj(OFF_SQ + c, IN_N_CHUNK), qg_ref[...]).astype(BF16)
    for c in range(0, RET_V_W, IN_N_CHUNK):
        mix_ref[:, MIX_RV + c:MIX_RV + c + IN_N_CHUNK] = proj(OFF_RV + c, IN_N_CHUNK).astype(BF16)
    mix_ref[:, MIX_RQ:MIX_RQ + RET_QK_W] = (proj(OFF_RQ, RET_QK_W) * dec_ref[:, 0:RET_QK_W]).astype(BF16)
    kd = proj(OFF_RK, RET_QK_W) * dec_ref[:, RET_QK_W:2 * RET_QK_W]
    kdt_ref[...] = kd.T.astype(BF16)
    mix_ref[:, MIX_SK:MIX_SK + SWA_KV_W] = _head_pair_rms(proj(OFF_SK, SWA_KV_W), kg_ref[...]).astype(BF16)
    mix_ref[:, MIX_SV:MIX_SV + SWA_KV_W] = proj(OFF_SV, SWA_KV_W).astype(BF16)
    for c in range(0, 2 * D_MODEL, IN_N_CHUNK):
        gate_ref[:, c:c + IN_N_CHUNK] = proj(OFF_GR + c, IN_N_CHUNK).astype(BF16)
    for c in range(0, RET_V_W, IN_N_CHUNK):
        gate_ref[:, GATE_RG + c:GATE_RG + c + IN_N_CHUNK] = proj(OFF_RG + c, IN_N_CHUNK).astype(BF16)


def _inproj_cast_kernel(x_ref, g_ref, qg_ref, kg_ref, dec_ref, w_ref,
                        wr_ref, ws_ref, wo_ref, wu_ref, wd_ref,
                        mix_ref, kdt_ref, gate_ref, wbo_out_ref, wu_out_ref, wd_out_ref):
    _inproj_kernel(x_ref, g_ref, qg_ref, kg_ref, dec_ref, w_ref, mix_ref, kdt_ref, gate_ref)
    for k, src in enumerate((wr_ref, ws_ref, wo_ref)):
        wbo_out_ref[k] = src[...].astype(BF16)
    wu_out_ref[...] = wu_ref[...].astype(BF16)
    wd_out_ref[...] = wd_ref[...].astype(BF16)


BF16_ROW_TILE = 16


def _inproj(x3d, g, qg2, kg2, dec, w_bf16, block_rows, cast_weights=None):
    b, seq, _ = x3d.shape
    assert seq % block_rows == 0 and dec.shape == (block_rows, 2 * RET_QK_W)
    steps = seq // block_rows
    n_steps = b * steps
    const = lambda shape: pl.BlockSpec(shape, lambda i: (0, 0))
    out_shape = [jax.ShapeDtypeStruct((b, seq, MIX_W), BF16),
                 jax.ShapeDtypeStruct((b, RET_QK_W, seq), BF16),
                 jax.ShapeDtypeStruct((b, seq, GATE_W), BF16)]
    out_specs = [pl.BlockSpec((None, block_rows, MIX_W), lambda i: (i // steps, i % steps, 0)),
                 pl.BlockSpec((None, RET_QK_W, block_rows), lambda i: (i // steps, 0, i % steps)),
                 pl.BlockSpec((None, block_rows, GATE_W), lambda i: (i // steps, i % steps, 0))]
    extra_in, extra_specs = [], []
    if cast_weights is not None:
        band = lambda w: w.shape[1] // n_steps
        for w in cast_weights:
            assert w.shape[1] % n_steps == 0 and band(w) % BF16_ROW_TILE == 0
            extra_in.append(w)
            extra_specs.append(pl.BlockSpec((None, band(w), w.shape[2]), lambda i: (0, i, 0)))
        w_r, _, _, w_u, w_d = cast_weights
        out_shape += [jax.ShapeDtypeStruct((3,) + w_r.shape[1:], BF16),
                      jax.ShapeDtypeStruct(w_u.shape[1:], BF16),
                      jax.ShapeDtypeStruct(w_d.shape[1:], BF16)]
        out_specs += [pl.BlockSpec((3, band(w_r), w_r.shape[2]), lambda i: (0, i, 0)),
                      pl.BlockSpec((band(w_u), w_u.shape[2]), lambda i: (i, 0)),
                      pl.BlockSpec((band(w_d), w_d.shape[2]), lambda i: (i, 0))]
    return pl.pallas_call(
        _inproj_kernel if cast_weights is None else _inproj_cast_kernel,
        out_shape=tuple(out_shape),
        grid=(n_steps,),
        in_specs=[
            pl.BlockSpec((None, block_rows, D_MODEL), lambda i: (i // steps, i % steps, 0)),
            const((1, D_MODEL)),
            const((1, LANES)),
            const((1, LANES)),
            const((block_rows, 2 * RET_QK_W)),
            pl.BlockSpec((D_MODEL, D_IN), lambda i: (0, 0), pipeline_mode=pl.Buffered(1)),
        ] + extra_specs,
        out_specs=tuple(out_specs),
        compiler_params=pltpu.CompilerParams(
            dimension_semantics=("arbitrary",),
            vmem_limit_bytes=V7X_VMEM_LIMIT_BYTES,
        ),
        name="inproj",
    )(x3d, g, qg2, kg2, dec, w_bf16, *extra_in)


def _decay_tables(block_rows):
    lg = _retention_log_gamma()
    pos = (np.arange(block_rows) % RET_CHUNK).astype(np.float64) + 1.0
    qd = np.exp(lg[None, :] * pos[:, None])
    kd = np.exp(-lg[None, :] * pos[:, None]) * (RET_QK_DIM ** -0.5)
    rep = lambda t: np.repeat(t, RET_QK_DIM, axis=1)
    return np.concatenate([rep(qd), rep(kd)], axis=1).astype(np.float32)


def _meta_decay_tables():
    lg = _retention_log_gamma()
    pos = np.arange(CHUNK, dtype=np.float64)
    kd = np.exp(lg[None, :] * (CHUNK - 1.0 - pos[:, None])) * (RET_QK_DIM ** -0.5)
    rep = lambda t: np.repeat(t, RET_QK_DIM, axis=1)
    return np.concatenate([np.ones((CHUNK, RET_QK_W)), rep(kd)], axis=1).astype(np.float32)


RET_CHUNK_DECAY = [float(c) for c in np.exp(_retention_log_gamma() * RET_CHUNK)]
PAIR_ROWS = 2 * CHUNK
BACK_ROWS = 512
DENSE_ROWS = 512
assert BACK_ROWS % RET_CHUNK == 0 and BLOCK_ROWS % BACK_ROWS == 0 and BACK_ROWS % DENSE_ROWS == 0
CHUNKS_PER_BLOCK = BACK_ROWS // CHUNK
RET_PER_BLOCK = BACK_ROWS // RET_CHUNK
KV_SLOTS = CHUNKS_PER_BLOCK + 1
FF_CHUNK = 512


def _swa_consts():
    slopes = 2.0 ** (-8.0 * np.arange(1, SWA_Q_HEADS + 1, dtype=np.float64) / SWA_Q_HEADS)
    slopes = slopes.astype(np.float32)
    i = np.arange(CHUNK)[:, None]
    j = np.arange(CHUNK)[None, :]
    band_dist = np.where(j <= i, i - j, i - j + CHUNK).astype(np.float32)
    meta_dist = np.where(j < 2 * N_META, i - (j % N_META) + N_META, np.inf).astype(np.float32)
    band_dist = np.stack([np.where(j <= i, band_dist, np.inf), band_dist]).astype(np.float32)
    return [float(s) for s in slopes], band_dist, meta_dist


ALIBI_SLOPES = _swa_consts()[0]


def _back_kernel(blocks_per_batch,
                 sink_ref,
                 mix_ref, kdt_ref, rg_ref, meta_ref, mkdt_ref, dist_ref, gains_ref,
                 x_ref, gate_ref, wbo_ref, wu_ref, wd_ref,
                 o_ref,
                 bd_ref, vbuf_ref, bdm_ref, vm_ref, state_ref, ys_ref, yr_ref, hn_ref):
    sq_ref = mix_ref.at[:, MIX_SQ:MIX_SQ + SWA_Q_W]
    rv_ref = mix_ref.at[:, MIX_RV:MIX_RV + RET_V_W]
    qd_ref = mix_ref.at[:, MIX_RQ:MIX_RQ + RET_QK_W]
    sk_ref = mix_ref.at[:, MIX_SK:MIX_SK + SWA_KV_W]
    sv_ref = mix_ref.at[:, MIX_SV:MIX_SV + SWA_KV_W]
    mk_ref = meta_ref.at[0, PAD:CHUNK, MIX_SK:MIX_SK + SWA_KV_W]
    mv_ref = meta_ref.at[0, PAD:CHUNK, MIX_SV:MIX_SV + SWA_KV_W]
    mrv_ref = meta_ref.at[0, :, MIX_RV:MIX_RV + RET_V_W]
    mkdt_ref = mkdt_ref.at[0]
    band_dist_ref = dist_ref.at[0:2]
    meta_dist_ref = dist_ref.at[2]
    rn_ref = gains_ref.at[0:1, :]
    g_ref = gains_ref.at[1:2, :]
    gr_ref = gate_ref.at[:, 0:D_MODEL]
    gs_ref = gate_ref.at[:, D_MODEL:2 * D_MODEL]
    wr_ref, ws_ref, wo_ref = wbo_ref.at[0], wbo_ref.at[1], wbo_ref.at[2]
    t = pl.program_id(0)
    block = jnp.minimum(t, pl.num_programs(0) - 2)
    block_in_batch = block % blocks_per_batch
    parity = t & 1
    last_slot = CHUNKS_PER_BLOCK - 1 + parity
    carried_slot = CHUNKS_PER_BLOCK - parity
    zeros_head = jnp.zeros((CHUNK, SWA_HEAD_DIM), BF16)

    @pl.when(t == 0)
    def _():
        ys_ref[...] = jnp.zeros_like(ys_ref)
        yr_ref[...] = jnp.zeros_like(yr_ref)

    @pl.when(block_in_batch == 0)
    def _():
        bd_ref[...] = jnp.zeros_like(bd_ref)
        vbuf_ref[...] = jnp.zeros_like(vbuf_ref)
        bdm_ref[...] = jnp.zeros_like(bdm_ref)
        vm_ref[...] = jnp.zeros_like(vm_ref)
        zm = jnp.zeros((N_META, SWA_HEAD_DIM), BF16)
        for g in range(SWA_KV_HEADS):
            cols = slice(g * SWA_HEAD_DIM, (g + 1) * SWA_HEAD_DIM)
            km = mk_ref[:, cols]
            bdm_ref[g, 0:N_META, :] = jnp.concatenate([km, zm], axis=1)
            bdm_ref[g, N_META:2 * N_META, :] = jnp.concatenate([zm, km], axis=1)
            vmeta = mv_ref[:, cols]
            vm_ref[g, 0:N_META, :] = jnp.concatenate([vmeta, zm], axis=1)
            vm_ref[g, N_META:2 * N_META, :] = jnp.concatenate([zm, vmeta], axis=1)
        for h in range(RET_HEADS):
            state_ref[h] = _dot(mkdt_ref[h * RET_QK_DIM:(h + 1) * RET_QK_DIM, :],
                                mrv_ref[:, h * RET_V_DIM:(h + 1) * RET_V_DIM])

    slabs = [slice(r, r + DENSE_ROWS) for r in range(0, BACK_ROWS, DENSE_ROWS)]
    for rs in slabs:
        merged = (_sigmoid(gr_ref[rs, :].astype(F32)) * _dot(yr_ref[rs, :], wr_ref[...])
                  + _sigmoid(gs_ref[rs, :].astype(F32)) * _dot(ys_ref[rs, :], ws_ref[...]))
        o_ref[rs, :] = x_ref[rs, :] + _dot(merged.astype(BF16), wo_ref[...])
        h1 = o_ref[rs, :]
        hn_ref[rs, :] = (h1 * lax.rsqrt(jnp.mean(h1 * h1, axis=-1, keepdims=True) + EPS)
                         * g_ref[...]).astype(BF16)

    ret_row = lax.broadcasted_iota(jnp.int32, (RET_CHUNK, RET_CHUNK), 0)
    ret_col = lax.broadcasted_iota(jnp.int32, (RET_CHUNK, RET_CHUNK), 1)
    ret_causal = ret_col <= ret_row

    def retention_head(sub, h):
        rows = slice(sub * RET_CHUNK, (sub + 1) * RET_CHUNK)
        vcols = slice(h * RET_V_DIM, (h + 1) * RET_V_DIM)
        qd = qd_ref[rows, h * RET_QK_DIM:(h + 1) * RET_QK_DIM]
        kdt = kdt_ref[h * RET_QK_DIM:(h + 1) * RET_QK_DIM, rows]
        v = rv_ref[rows, vcols]
        state = state_ref[h]
        s = jnp.where(ret_causal, _dot(qd, kdt), 0.0).astype(BF16)
        o = _dot(s, v) + _dot(qd, state.astype(BF16))
        state_ref[h] = RET_CHUNK_DECAY[h] * (state + _dot(kdt, v))
        ms = jnp.mean(o * o, axis=-1, keepdims=True)
        gate = rg_ref[rows, vcols].astype(F32)
        yr_ref[rows, vcols] = (o * lax.rsqrt(ms + EPS) * rn_ref[:, vcols]
                               * (gate * _sigmoid(gate))).astype(BF16)

    row = lax.broadcasted_iota(jnp.int32, (CHUNK, CHUNK), 0)
    col = lax.broadcasted_iota(jnp.int32, (CHUNK, CHUNK), 1)
    own = col <= row
    own2 = jnp.concatenate([own, own], axis=1)
    own4 = jnp.concatenate([own2, own2], axis=0)
    meta_seg = [col < N_META, (col >= N_META) & (col < 2 * N_META)]
    low_half = col < SWA_HEAD_DIM
    neg_inf = jnp.float32(-jnp.inf)

    first_chunk = (CHUNKS_PER_BLOCK * block_in_batch).astype(F32)
    chunks = []
    for c in range(CHUNKS_PER_BLOCK):
        slot = c if c < CHUNKS_PER_BLOCK - 1 else last_slot
        prev_slot = c - 1 if c > 0 else carried_slot
        bias_sel = jnp.minimum(block_in_batch, 1) if c == 0 else 1
        chunks.append((slice(c * CHUNK, (c + 1) * CHUNK), slot, prev_slot, bias_sel, first_chunk + float(c)))

    for rows, slot, _, _, _ in chunks:
        for g in range(SWA_KV_HEADS):
            kg = sk_ref[rows, g * SWA_HEAD_DIM:(g + 1) * SWA_HEAD_DIM]
            bd_ref[slot, g, 0:CHUNK, :] = jnp.concatenate([kg, zeros_head], axis=1)
            bd_ref[slot, g, CHUNK:2 * CHUNK, :] = jnp.concatenate([zeros_head, kg], axis=1)
            vg = sv_ref[rows, g * SWA_HEAD_DIM:(g + 1) * SWA_HEAD_DIM]
            vbuf_ref[slot, g] = jnp.concatenate([vg, vg], axis=1)

    def scores(chunk, g):
        rows, slot, prev_slot, bias_sel, chunk_idx = chunk
        q2 = jnp.concatenate(
            [sq_ref[rows, (2 * g + p) * LANES:(2 * g + p + 1) * LANES] for p in range(2)],
            axis=0)
        s_own = _dot_nt(q2, bd_ref[slot, g])
        s_prev = _dot_nt(q2, bd_ref[prev_slot, g])
        dist = band_dist_ref[bias_sel]
        bias = jnp.concatenate(
            [jnp.concatenate([ALIBI_SLOPES[g * SWA_GROUP + 2 * p + e] * dist for e in range(2)], axis=1)
             for p in range(2)], axis=0)
        s = jnp.where(own4, s_own, s_prev) - bias
        dist_meta = meta_dist_ref[...] + float(CHUNK) * chunk_idx
        bias_meta = jnp.concatenate(
            [jnp.where(meta_seg[0], ALIBI_SLOPES[g * SWA_GROUP + 2 * p],
                       ALIBI_SLOPES[g * SWA_GROUP + 2 * p + 1]) * dist_meta for p in range(2)], axis=0)
        s_meta = _dot_nt(q2, bdm_ref[g]) - bias_meta
        return s, s_meta

    def softmax(g, s, s_meta):
        lhs = []
        lhs_meta = []
        inv = []
        for p in range(2):
            sm_p = s_meta[p * CHUNK:(p + 1) * CHUNK]
            pm_pair = None
            for e in range(2):
                h = g * SWA_GROUP + 2 * p + e
                sink = sink_ref[h]
                sb = s[p * CHUNK:(p + 1) * CHUNK, e * CHUNK:(e + 1) * CHUNK]
                sm = jnp.where(meta_seg[e], sm_p, neg_inf)
                m = jnp.maximum(jnp.max(jnp.maximum(sb, sm), axis=-1, keepdims=True), sink)
                pb = jnp.exp(sb - m)
                pm = jnp.exp(sm - m)
                denom = jnp.sum(pb + pm, axis=-1, keepdims=True) + jnp.exp(sink - m)
                inv.append(1.0 / denom)
                lhs.append(jnp.concatenate(
                    [jnp.where(own, pb, 0.0).astype(BF16), jnp.where(own, 0.0, pb).astype(BF16)], axis=1))
                pm_pair = pm if pm_pair is None else pm_pair + pm
            lhs_meta.append(pm_pair.astype(BF16))
        return jnp.concatenate(lhs, axis=0), jnp.concatenate(lhs_meta, axis=0), inv

    def weighted_values(chunk, g, lhs, lhs_meta, inv):
        rows, slot, prev_slot, _, _ = chunk
        vv = jnp.concatenate([vbuf_ref[slot, g], vbuf_ref[prev_slot, g]], axis=0)
        acc = _dot(lhs, vv)
        acc_meta = _dot(lhs_meta, vm_ref[g])
        for p in range(2):
            first, second = 2 * p, 2 * p + 1
            pair = jnp.where(low_half, acc[first * CHUNK:(first + 1) * CHUNK],
                             acc[second * CHUNK:(second + 1) * CHUNK])
            scale = jnp.where(low_half, inv[first], inv[second])
            tile = (pair + acc_meta[p * CHUNK:(p + 1) * CHUNK]) * scale
            ys_ref[rows, (2 * g + p) * LANES:(2 * g + p + 1) * LANES] = tile.astype(BF16)

    tasks = [(chunk, g) for chunk in chunks for g in range(SWA_KV_HEADS)]
    ret_tasks = [(sub, h) for sub in range(RET_PER_BLOCK) for h in range(RET_HEADS)]
    assert len(tasks) == 2 * len(ret_tasks)

    def mlp_phase(k):
        c = k * FF_CHUNK
        for rs in slabs:
            u = jnp.maximum(_dot(hn_ref[rs, :], wu_ref[:, c:c + FF_CHUNK]), 0.0)
            o_ref[rs, :] += _dot((u * u).astype(BF16), wd_ref[c:c + FF_CHUNK, :])

    n_mlp_phases = D_FF // FF_CHUNK
    phase_at_task = {i * len(tasks) // n_mlp_phases: i for i in range(n_mlp_phases)}

    pending_scores = scores(*tasks[0])
    pending_probs = None
    for i, (chunk, g) in enumerate(tasks):
        current_scores = pending_scores
        if i + 1 < len(tasks):
            pending_scores = scores(*tasks[i + 1])
        if i in phase_at_task:
            mlp_phase(phase_at_task[i])
        probs = softmax(g, *current_scores)
        if i % 2 == 0:
            retention_head(*ret_tasks[i // 2])
        if pending_probs is not None:
            weighted_values(*tasks[i - 1], *pending_probs)
        pending_probs = probs
    weighted_values(*tasks[-1], *pending_probs)


def _back(x2d, mix2d, kdt, gates2d, meta_mix, meta_kdt, sinks, norm_gains, w_branch_out, w_up, w_down,
          blocks_per_batch):
    rows = x2d.shape[0]
    assert rows % BACK_ROWS == 0
    n_blocks = rows // BACK_ROWS
    _, band_dist, meta_dist = _swa_consts()

    cur = lambda t: jnp.minimum(t, n_blocks - 1)
    prev = lambda t: jnp.maximum(t - 1, 0)

    def resident(shape):
        zeros = (0,) * len(shape)
        return pl.BlockSpec(shape, lambda t, s: zeros, pipeline_mode=pl.Buffered(1))

    grid_spec = pltpu.PrefetchScalarGridSpec(
        num_scalar_prefetch=1,
        grid=(n_blocks + 1,),
        in_specs=[
            pl.BlockSpec((BACK_ROWS, MIX_W), lambda t, s: (cur(t), 0)),
            pl.BlockSpec((None, RET_QK_W, BACK_ROWS),
                         lambda t, s: (cur(t) // blocks_per_batch, 0, cur(t) % blocks_per_batch)),
            pl.BlockSpec((BACK_ROWS, RET_V_W), lambda t, s: (cur(t), GATE_RG // RET_V_W)),
            resident((1, CHUNK, MIX_W)),
            resident((1, RET_QK_W, CHUNK)),
            resident((3, CHUNK, CHUNK)),
            resident((2, D_MODEL)),
            pl.BlockSpec((BACK_ROWS, D_MODEL), lambda t, s: (prev(t), 0)),
            pl.BlockSpec((BACK_ROWS, 2 * D_MODEL), lambda t, s: (prev(t), 0)),
            resident((3, D_MODEL, D_MODEL)),
            resident((D_MODEL, D_FF)),
            resident((D_FF, D_MODEL)),
        ],
        out_specs=pl.BlockSpec((BACK_ROWS, D_MODEL), lambda t, s: (prev(t), 0)),
        scratch_shapes=[
            pltpu.VMEM((KV_SLOTS, SWA_KV_HEADS, PAIR_ROWS, LANES), BF16),
            pltpu.VMEM((KV_SLOTS, SWA_KV_HEADS, CHUNK, LANES), BF16),
            pltpu.VMEM((SWA_KV_HEADS, CHUNK, LANES), BF16),
            pltpu.VMEM((SWA_KV_HEADS, CHUNK, LANES), BF16),
            pltpu.VMEM((RET_HEADS, RET_QK_DIM, RET_V_DIM), F32),
            pltpu.VMEM((BACK_ROWS, SWA_Q_W), BF16),
            pltpu.VMEM((BACK_ROWS, RET_V_W), BF16),
            pltpu.VMEM((BACK_ROWS, D_MODEL), BF16),
        ],
    )
    return pl.pallas_call(
        functools.partial(_back_kernel, blocks_per_batch),
        out_shape=jax.ShapeDtypeStruct((rows, D_MODEL), F32),
        grid_spec=grid_spec,
        compiler_params=pltpu.CompilerParams(
            dimension_semantics=("arbitrary",),
            vmem_limit_bytes=V7X_VMEM_LIMIT_BYTES,
        ),
        name="back",
    )(sinks.astype(F32), mix2d, kdt, gates2d, meta_mix, meta_kdt,
      jnp.asarray(np.concatenate([band_dist, meta_dist[None]])), norm_gains,
      x2d, gates2d, w_branch_out, w_up, w_down)


def kernel(x, meta_tokens, mix_norm_g, w_in, ret_norm_g, q_norm_g, k_norm_g, sinks,
           w_ret_branch, w_swa_branch, w_out, mlp_norm_g, w_up, w_down):
    b, l, d = x.shape
    assert d == D_MODEL and l % BLOCK_ROWS == 0 and mix_norm_g.shape[0] == 1
    mix_g = mix_norm_g[0].reshape(1, d).astype(F32)
    w_in_b = w_in[0].astype(BF16)
    qg2 = jnp.tile(q_norm_g[0].astype(F32) * (SWA_HEAD_DIM ** -0.5), 2).reshape(1, LANES)
    kg2 = jnp.tile(k_norm_g[0].astype(F32), 2).reshape(1, LANES)

    mix, kdt, gates, w_branch_out, w_up_b, w_down_b = _inproj(
        x, mix_g, qg2, kg2, jnp.asarray(_decay_tables(BLOCK_ROWS)), w_in_b, BLOCK_ROWS,
        cast_weights=(w_ret_branch, w_swa_branch, w_out, w_up, w_down))
    meta_chunk = jnp.pad(meta_tokens.astype(F32), ((PAD, 0), (0, 0)))[None]
    meta_mix, meta_kdt, _ = _inproj(meta_chunk, mix_g, qg2, kg2, jnp.asarray(_meta_decay_tables()),
                                    w_in_b, CHUNK)

    norm_gains = jnp.stack([ret_norm_g[0].reshape(RET_V_W), mlp_norm_g[0]]).astype(F32)
    out = _back(
        x.reshape(b * l, d), mix.reshape(b * l, MIX_W), kdt, gates.reshape(b * l, GATE_W),
        meta_mix, meta_kdt, sinks[0], norm_gains, w_branch_out, w_up_b, w_down_b, l // BACK_ROWS)
    return out.reshape(b, l, d)
```

```python
import functools

import numpy as np
import jax
import jax.numpy as jnp
from jax import lax
from jax.experimental import pallas as pl
from jax.experimental.pallas import tpu as pltpu

D_MODEL = 1024
N_META = 16
CHUNK = 128
PAD = CHUNK - N_META
RET_HEADS = 4
RET_QK_DIM = 128
RET_V_DIM = 256
SWA_Q_HEADS = 16
SWA_KV_HEADS = 4
SWA_GROUP = SWA_Q_HEADS // SWA_KV_HEADS
SWA_HEAD_DIM = 64
D_FF = 4 * D_MODEL
EPS = 1e-6
RET_QK_W = RET_HEADS * RET_QK_DIM
RET_V_W = RET_HEADS * RET_V_DIM
SWA_Q_W = SWA_Q_HEADS * SWA_HEAD_DIM
SWA_KV_W = SWA_KV_HEADS * SWA_HEAD_DIM
D_IN = 2 * RET_QK_W + 2 * RET_V_W + SWA_Q_W + 2 * SWA_KV_W + 2 * D_MODEL

OFF_RQ = 0
OFF_RK = OFF_RQ + RET_QK_W
OFF_RV = OFF_RK + RET_QK_W
OFF_RG = OFF_RV + RET_V_W
OFF_SQ = OFF_RG + RET_V_W
OFF_SK = OFF_SQ + SWA_Q_W
OFF_SV = OFF_SK + SWA_KV_W
OFF_GR = OFF_SV + SWA_KV_W
OFF_GS = OFF_GR + D_MODEL
assert OFF_GS + D_MODEL == D_IN

MIX_SQ = 0
MIX_RV = MIX_SQ + SWA_Q_W
MIX_RQ = MIX_RV + RET_V_W
MIX_SK = MIX_RQ + RET_QK_W
MIX_SV = MIX_SK + SWA_KV_W
MIX_W = MIX_SV + SWA_KV_W
GATE_RG = 2 * D_MODEL
GATE_W = GATE_RG + RET_V_W

V7X_VMEM_LIMIT_BYTES = 64 * 1024 * 1024
LANES = 128
assert LANES == 2 * SWA_HEAD_DIM and CHUNK == LANES and RET_V_W == D_MODEL

F32 = jnp.float32
BF16 = jnp.bfloat16

BLOCK_ROWS = 1024
RET_CHUNK = 256
assert BLOCK_ROWS % RET_CHUNK == 0 and BLOCK_ROWS % CHUNK == 0


def _sigmoid(x):
    return 1.0 / (1.0 + jnp.exp(-x))


def _dot(a, b):
    return jnp.dot(a, b, preferred_element_type=F32)


def _dot_nt(a, b):
    return lax.dot_general(a, b, (((1,), (1,)), ((), ())), preferred_element_type=F32)


def _retention_log_gamma():
    return np.log1p(-(2.0 ** (-5.0 - np.arange(RET_HEADS, dtype=np.float64))))


IN_N_CHUNK = 512


def _head_pair_rms(y, g2):
    rows, width = y.shape
    low = lax.broadcasted_iota(jnp.int32, (rows, LANES), 1) < SWA_HEAD_DIM
    out = []
    for c in range(0, width, LANES):
        yb = y[:, c:c + LANES]
        y2 = yb * yb
        ms_lo = jnp.sum(jnp.where(low, y2, 0.0), axis=-1, keepdims=True) * (1.0 / SWA_HEAD_DIM)
        ms_hi = jnp.sum(jnp.where(low, 0.0, y2), axis=-1, keepdims=True) * (1.0 / SWA_HEAD_DIM)
        scale = jnp.where(low, lax.rsqrt(ms_lo + EPS), lax.rsqrt(ms_hi + EPS))
        out.append(yb * scale * g2)
    return jnp.concatenate(out, axis=1) if len(out) > 1 else out[0]


def _inproj_kernel(x_ref, g_ref, qg_ref, kg_ref, dec_ref, w_ref, mix_ref, kdt_ref, gate_ref):
    x = x_ref[...]
    ms = jnp.mean(x * x, axis=-1, keepdims=True)
    hn = (x * lax.rsqrt(ms + EPS) * g_ref[...]).astype(BF16)

    def proj(lo, width):
        return _dot(hn, w_ref[:, lo:lo + width])

    for c in range(0, SWA_Q_W, IN_N_CHUNK):
        mix_ref[:, MIX_SQ + c:MIX_SQ + c + IN_N_CHUNK] = _head_pair_rms(
            proj(OFF_SQ + c, IN_N_CHUNK), qg_ref[...]).astype(BF16)
    for c in range(0, RET_V_W, IN_N_CHUNK):
        mix_ref[:, MIX_RV + c:MIX_RV + c + IN_N_CHUNK] = proj(OFF_RV + c, IN_N_CHUNK).astype(BF16)
    mix_ref[:, MIX_RQ:MIX_RQ + RET_QK_W] = (proj(OFF_RQ, RET_QK_W) * dec_ref[:, 0:RET_QK_W]).astype(BF16)
    kd = proj(OFF_RK, RET_QK_W) * dec_ref[:, RET_QK_W:2 * RET_QK_W]
    kdt_ref[...] = kd.T.astype(BF16)
    mix_ref[:, MIX_SK:MIX_SK + SWA_KV_W] = _head_pair_rms(proj(OFF_SK, SWA_KV_W), kg_ref[...]).astype(BF16)
    mix_ref[:, MIX_SV:MIX_SV + SWA_KV_W] = proj(OFF_SV, SWA_KV_W).astype(BF16)
    for c in range(0, 2 * D_MODEL, IN_N_CHUNK):
        gate_ref[:, c:c + IN_N_CHUNK] = proj(OFF_GR + c, IN_N_CHUNK).astype(BF16)
    for c in range(0, RET_V_W, IN_N_CHUNK):
        gate_ref[:, GATE_RG + c:GATE_RG + c + IN_N_CHUNK] = proj(OFF_RG + c, IN_N_CHUNK).astype(BF16)


def _inproj_cast_kernel(x_ref, g_ref, qg_ref, kg_ref, dec_ref, w_ref,
                        wr_ref, ws_ref, wo_ref, wu_ref, wd_ref,
                        mix_ref, kdt_ref, gate_ref, wbo_out_ref, wu_out_ref, wd_out_ref):
    _inproj_kernel(x_ref, g_ref, qg_ref, kg_ref, dec_ref, w_ref, mix_ref, kdt_ref, gate_ref)
    for k, src in enumerate((wr_ref, ws_ref, wo_ref)):
        wbo_out_ref[k] = src[...].astype(BF16)
    wu_out_ref[...] = wu_ref[...].astype(BF16)
    wd_out_ref[...] = wd_ref[...].astype(BF16)


BF16_ROW_TILE = 16


def _inproj(x3d, g, qg2, kg2, dec, w_bf16, block_rows, cast_weights=None):
    b, seq, _ = x3d.shape
    assert seq % block_rows == 0 and dec.shape == (block_rows, 2 * RET_QK_W)
    steps = seq // block_rows
    n_steps = b * steps
    const = lambda shape: pl.BlockSpec(shape, lambda i: (0, 0))
    out_shape = [jax.ShapeDtypeStruct((b, seq, MIX_W), BF16),
                 jax.ShapeDtypeStruct((b, RET_QK_W, seq), BF16),
                 jax.ShapeDtypeStruct((b, seq, GATE_W), BF16)]
    out_specs = [pl.BlockSpec((None, block_rows, MIX_W), lambda i: (i // steps, i % steps, 0)),
                 pl.BlockSpec((None, RET_QK_W, block_rows), lambda i: (i // steps, 0, i % steps)),
                 pl.BlockSpec((None, block_rows, GATE_W), lambda i: (i // steps, i % steps, 0))]
    extra_in, extra_specs = [], []
    if cast_weights is not None:
        band = lambda w: w.shape[1] // n_steps
        for w in cast_weights:
            assert w.shape[1] % n_steps == 0 and band(w) % BF16_ROW_TILE == 0
            extra_in.append(w)
            extra_specs.append(pl.BlockSpec((None, band(w), w.shape[2]), lambda i: (0, i, 0)))
        w_r, _, _, w_u, w_d = cast_weights
        out_shape += [jax.ShapeDtypeStruct((3,) + w_r.shape[1:], BF16),
                      jax.ShapeDtypeStruct(w_u.shape[1:], BF16),
                      jax.ShapeDtypeStruct(w_d.shape[1:], BF16)]
        out_specs += [pl.BlockSpec((3, band(w_r), w_r.shape[2]), lambda i: (0, i, 0)),
                      pl.BlockSpec((band(w_u), w_u.shape[2]), lambda i: (i, 0)),
                      pl.BlockSpec((band(w_d), w_d.shape[2]), lambda i: (i, 0))]
    return pl.pallas_call(
        _inproj_kernel if cast_weights is None else _inproj_cast_kernel,
        out_shape=tuple(out_shape),
        grid=(n_steps,),
        in_specs=[
            pl.BlockSpec((None, block_rows, D_MODEL), lambda i: (i // steps, i % steps, 0)),
            const((1, D_MODEL)),
            const((1, LANES)),
            const((1, LANES)),
            const((block_rows, 2 * RET_QK_W)),
            pl.BlockSpec((D_MODEL, D_IN), lambda i: (0, 0), pipeline_mode=pl.Buffered(1)),
        ] + extra_specs,
        out_specs=tuple(out_specs),
        compiler_params=pltpu.CompilerParams(
            dimension_semantics=("arbitrary",),
            vmem_limit_bytes=V7X_VMEM_LIMIT_BYTES,
        ),
        name="inproj",
    )(x3d, g, qg2, kg2, dec, w_bf16, *extra_in)


def _decay_tables(block_rows):
    lg = _retention_log_gamma()
    pos = (np.arange(block_rows) % RET_CHUNK).astype(np.float64) + 1.0
    qd = np.exp(lg[None, :] * pos[:, None])
    kd = np.exp(-lg[None, :] * pos[:, None]) * (RET_QK_DIM ** -0.5)
    rep = lambda t: np.repeat(t, RET_QK_DIM, axis=1)
    return np.concatenate([rep(qd), rep(kd)], axis=1).astype(np.float32)


def _meta_decay_tables():
    lg = _retention_log_gamma()
    pos = np.arange(CHUNK, dtype=np.float64)
    kd = np.exp(lg[None, :] * (CHUNK - 1.0 - pos[:, None])) * (RET_QK_DIM ** -0.5)
    rep = lambda t: np.repeat(t, RET_QK_DIM, axis=1)
    return np.concatenate([np.ones((CHUNK, RET_QK_W)), rep(kd)], axis=1).astype(np.float32)


RET_CHUNK_DECAY = [float(c) for c in np.exp(_retention_log_gamma() * RET_CHUNK)]
PAIR_ROWS = 2 * CHUNK
BACK_ROWS = 512
DENSE_ROWS = 512
assert BACK_ROWS % RET_CHUNK == 0 and BLOCK_ROWS % BACK_ROWS == 0 and BACK_ROWS % DENSE_ROWS == 0
CHUNKS_PER_BLOCK = BACK_ROWS // CHUNK
RET_PER_BLOCK = BACK_ROWS // RET_CHUNK
KV_SLOTS = CHUNKS_PER_BLOCK + 1
FF_CHUNK = 512


def _swa_consts():
    slopes = 2.0 ** (-8.0 * np.arange(1, SWA_Q_HEADS + 1, dtype=np.float64) / SWA_Q_HEADS)
    slopes = slopes.astype(np.float32)
    i = np.arange(CHUNK)[:, None]
    j = np.arange(CHUNK)[None, :]
    band_dist = np.where(j <= i, i - j, i - j + CHUNK).astype(np.float32)
    meta_dist = np.where(j < 2 * N_META, i - (j % N_META) + N_META, np.inf).astype(np.float32)
    band_dist = np.stack([np.where(j <= i, band_dist, np.inf), band_dist]).astype(np.float32)
    return [float(s) for s in slopes], band_dist, meta_dist


ALIBI_SLOPES = _swa_consts()[0]


def _back_kernel(blocks_per_batch,
                 sink_ref,
                 mix_ref, kdt_ref, rg_ref, meta_ref, mkdt_ref, dist_ref, gains_ref,
                 x_ref, gate_ref, wbo_ref, wu_ref, wd_ref,
                 o_ref,
                 bd_ref, vbuf_ref, bdm_ref, vm_ref, state_ref, ys_ref, yr_ref, hn_ref):
    sq_ref = mix_ref.at[:, MIX_SQ:MIX_SQ + SWA_Q_W]
    rv_ref = mix_ref.at[:, MIX_RV:MIX_RV + RET_V_W]
    qd_ref = mix_ref.at[:, MIX_RQ:MIX_RQ + RET_QK_W]
    sk_ref = mix_ref.at[:, MIX_SK:MIX_SK + SWA_KV_W]
    sv_ref = mix_ref.at[:, MIX_SV:MIX_SV + SWA_KV_W]
    mk_ref = meta_ref.at[0, PAD:CHUNK, MIX_SK:MIX_SK + SWA_KV_W]
    mv_ref = meta_ref.at[0, PAD:CHUNK, MIX_SV:MIX_SV + SWA_KV_W]
    mrv_ref = meta_ref.at[0, :, MIX_RV:MIX_RV + RET_V_W]
    mkdt_ref = mkdt_ref.at[0]
    band_dist_ref = dist_ref.at[0:2]
    meta_dist_ref = dist_ref.at[2]
    rn_ref = gains_ref.at[0:1, :]
    g_ref = gains_ref.at[1:2, :]
    gr_ref = gate_ref.at[:, 0:D_MODEL]
    gs_ref = gate_ref.at[:, D_MODEL:2 * D_MODEL]
    wr_ref, ws_ref, wo_ref = wbo_ref.at[0], wbo_ref.at[1], wbo_ref.at[2]
    t = pl.program_id(0)
    block = jnp.minimum(t, pl.num_programs(0) - 2)
    block_in_batch = block % blocks_per_batch
    parity = t & 1
    last_slot = CHUNKS_PER_BLOCK - 1 + parity
    carried_slot = CHUNKS_PER_BLOCK - parity
    zeros_head = jnp.zeros((CHUNK, SWA_HEAD_DIM), BF16)

    @pl.when(t == 0)
    def _():
        ys_ref[...] = jnp.zeros_like(ys_ref)
        yr_ref[...] = jnp.zeros_like(yr_ref)

    @pl.when(block_in_batch == 0)
    def _():
        bd_ref[...] = jnp.zeros_like(bd_ref)
        vbuf_ref[...] = jnp.zeros_like(vbuf_ref)
        bdm_ref[...] = jnp.zeros_like(bdm_ref)
        vm_ref[...] = jnp.zeros_like(vm_ref)
        zm = jnp.zeros((N_META, SWA_HEAD_DIM), BF16)
        for g in range(SWA_KV_HEADS):
            cols = slice(g * SWA_HEAD_DIM, (g + 1) * SWA_HEAD_DIM)
            km = mk_ref[:, cols]
            bdm_ref[g, 0:N_META, :] = jnp.concatenate([km, zm], axis=1)
            bdm_ref[g, N_META:2 * N_META, :] = jnp.concatenate([zm, km], axis=1)
            vmeta = mv_ref[:, cols]
            vm_ref[g, 0:N_META, :] = jnp.concatenate([vmeta, zm], axis=1)
            vm_ref[g, N_META:2 * N_META, :] = jnp.concatenate([zm, vmeta], axis=1)
        for h in range(RET_HEADS):
            state_ref[h] = _dot(mkdt_ref[h * RET_QK_DIM:(h + 1) * RET_QK_DIM, :],
                                mrv_ref[:, h * RET_V_DIM:(h + 1) * RET_V_DIM])

    slabs = [slice(r, r + DENSE_ROWS) for r in range(0, BACK_ROWS, DENSE_ROWS)]
    for rs in slabs:
        merged = (_sigmoid(gr_ref[rs, :].astype(F32)) * _dot(yr_ref[rs, :], wr_ref[...])
                  + _sigmoid(gs_ref[rs, :].astype(F32)) * _dot(ys_ref[rs, :], ws_ref[...]))
        o_ref[rs, :] = x_ref[rs, :] + _dot(merged.astype(BF16), wo_ref[...])
        h1 = o_ref[rs, :]
        hn_ref[rs, :] = (h1 * lax.rsqrt(jnp.mean(h1 * h1, axis=-1, keepdims=True) + EPS)
                         * g_ref[...]).astype(BF16)

    ret_row = lax.broadcasted_iota(jnp.int32, (RET_CHUNK, RET_CHUNK), 0)
    ret_col = lax.broadcasted_iota(jnp.int32, (RET_CHUNK, RET_CHUNK), 1)
    ret_causal = ret_col <= ret_row

    def retention_head(sub, h):
        rows = slice(sub * RET_CHUNK, (sub + 1) * RET_CHUNK)
        vcols = slice(h * RET_V_DIM, (h + 1) * RET_V_DIM)
        qd = qd_ref[rows, h * RET_QK_DIM:(h + 1) * RET_QK_DIM]
        kdt = kdt_ref[h * RET_QK_DIM:(h + 1) * RET_QK_DIM, rows]
        v = rv_ref[rows, vcols]
        state = state_ref[h]
        s = jnp.where(ret_causal, _dot(qd, kdt), 0.0).astype(BF16)
        o = _dot(s, v) + _dot(qd, state.astype(BF16))
        state_ref[h] = RET_CHUNK_DECAY[h] * (state + _dot(kdt, v))
        ms = jnp.mean(o * o, axis=-1, keepdims=True)
        gate = rg_ref[rows, vcols].astype(F32)
        yr_ref[rows, vcols] = (o * lax.rsqrt(ms + EPS) * rn_ref[:, vcols]
                               * (gate * _sigmoid(gate))).astype(BF16)

    row = lax.broadcasted_iota(jnp.int32, (CHUNK, CHUNK), 0)
    col = lax.broadcasted_iota(jnp.int32, (CHUNK, CHUNK), 1)
    own = col <= row
    own2 = jnp.concatenate([own, own], axis=1)
    own4 = jnp.concatenate([own2, own2], axis=0)
    meta_seg = [col < N_META, (col >= N_META) & (col < 2 * N_META)]
    low_half = col < SWA_HEAD_DIM
    neg_inf = jnp.float32(-jnp.inf)

    first_chunk = (CHUNKS_PER_BLOCK * block_in_batch).astype(F32)
    chunks = []
    for c in range(CHUNKS_PER_BLOCK):
        slot = c if c < CHUNKS_PER_BLOCK - 1 else last_slot
        prev_slot = c - 1 if c > 0 else carried_slot
        bias_sel = jnp.minimum(block_in_batch, 1) if c == 0 else 1
        chunks.append((slice(c * CHUNK, (c + 1) * CHUNK), slot, prev_slot, bias_sel, first_chunk + float(c)))

    for rows, slot, _, _, _ in chunks:
        for g in range(SWA_KV_HEADS):
            kg = sk_ref[rows, g * SWA_HEAD_DIM:(g + 1) * SWA_HEAD_DIM]
            bd_ref[slot, g, 0:CHUNK, :] = jnp.concatenate([kg, zeros_head], axis=1)
            bd_ref[slot, g, CHUNK:2 * CHUNK, :] = jnp.concatenate([zeros_head, kg], axis=1)
            vg = sv_ref[rows, g * SWA_HEAD_DIM:(g + 1) * SWA_HEAD_DIM]
            vbuf_ref[slot, g] = jnp.concatenate([vg, vg], axis=1)

    def scores(chunk, g):
        rows, slot, prev_slot, bias_sel, chunk_idx = chunk
        q2 = jnp.concatenate(
            [sq_ref[rows, (2 * g + p) * LANES:(2 * g + p + 1) * LANES] for p in range(2)],
            axis=0)
        s_own = _dot_nt(q2, bd_ref[slot, g])
        s_prev = _dot_nt(q2, bd_ref[prev_slot, g])
        dist = band_dist_ref[bias_sel]
        bias = jnp.concatenate(
            [jnp.concatenate([ALIBI_SLOPES[g * SWA_GROUP + 2 * p + e] * dist for e in range(2)], axis=1)
             for p in range(2)], axis=0)
        s = jnp.where(own4, s_own, s_prev) - bias
        dist_meta = meta_dist_ref[...] + float(CHUNK) * chunk_idx
        bias_meta = jnp.concatenate(
            [jnp.where(meta_seg[0], ALIBI_SLOPES[g * SWA_GROUP + 2 * p],
                       ALIBI_SLOPES[g * SWA_GROUP + 2 * p + 1]) * dist_meta for p in range(2)], axis=0)
        s_meta = _dot_nt(q2, bdm_ref[g]) - bias_meta
        return s, s_meta

    def softmax(g, s, s_meta):
        lhs = []
        lhs_meta = []
        inv = []
        for p in range(2):
            sm_p = s_meta[p * CHUNK:(p + 1) * CHUNK]
            pm_pair = None
            for e in range(2):
                h = g * SWA_GROUP + 2 * p + e
                sink = sink_ref[h]
                sb = s[p * CHUNK:(p + 1) * CHUNK, e * CHUNK:(e + 1) * CHUNK]
                sm = jnp.where(meta_seg[e], sm_p, neg_inf)
                m = jnp.maximum(jnp.max(jnp.maximum(sb, sm), axis=-1, keepdims=True), sink)
                pb = jnp.exp(sb - m)
                pm = jnp.exp(sm - m)
                denom = jnp.sum(pb + pm, axis=-1, keepdims=True) + jnp.exp(sink - m)
                inv.append(1.0 / denom)
                lhs.append(jnp.concatenate(
                    [jnp.where(own, pb, 0.0).astype(BF16), jnp.where(own, 0.0, pb).astype(BF16)], axis=1))
                pm_pair = pm if pm_pair is None else pm_pair + pm
            lhs_meta.append(pm_pair.astype(BF16))
        return jnp.concatenate(lhs, axis=0), jnp.concatenate(lhs_meta, axis=0), inv

    def weighted_values(chunk, g, lhs, lhs_meta, inv):
        rows, slot, prev_slot, _, _ = chunk
        vv = jnp.concatenate([vbuf_ref[slot, g], vbuf_ref[prev_slot, g]], axis=0)
        acc = _dot(lhs, vv)
        acc_meta = _dot(lhs_meta, vm_ref[g])
        for p in range(2):
            first, second = 2 * p, 2 * p + 1
            pair = jnp.where(low_half, acc[first * CHUNK:(first + 1) * CHUNK],
                             acc[second * CHUNK:(second + 1) * CHUNK])
            scale = jnp.where(low_half, inv[first], inv[second])
            tile = (pair + acc_meta[p * CHUNK:(p + 1) * CHUNK]) * scale
            ys_ref[rows, (2 * g + p) * LANES:(2 * g + p + 1) * LANES] = tile.astype(BF16)

    tasks = [(chunk, g) for chunk in chunks for g in range(SWA_KV_HEADS)]
    ret_tasks = [(sub, h) for sub in range(RET_PER_BLOCK) for h in range(RET_HEADS)]
    assert len(tasks) == 2 * len(ret_tasks)

    def mlp_phase(k):
        c = k * FF_CHUNK
        for rs in slabs:
            u = jnp.maximum(_dot(hn_ref[rs, :], wu_ref[:, c:c + FF_CHUNK]), 0.0)
            o_ref[rs, :] += _dot((u * u).astype(BF16), wd_ref[c:c + FF_CHUNK, :])

    n_mlp_phases = D_FF // FF_CHUNK
    phase_at_task = {i * len(tasks) // n_mlp_phases: i for i in range(n_mlp_phases)}

    pending_scores = scores(*tasks[0])
    pending_probs = None
    for i, (chunk, g) in enumerate(tasks):
        current_scores = pending_scores
        if i + 1 < len(tasks):
            pending_scores = scores(*tasks[i + 1])
        if i in phase_at_task:
            k = phase_at_task[i]
            acts = [(jnp.square(jnp.maximum(_dot(hn_ref[rs, :], wu_ref[:, k * FF_CHUNK:(k + 1) * FF_CHUNK]), 0.0))
                     ).astype(BF16) for rs in slabs]
        elif i - 1 in phase_at_task:
            k = phase_at_task[i - 1]
            for rs, act in zip(slabs, acts):
                o_ref[rs, :] += _dot(act, wd_ref[k * FF_CHUNK:(k + 1) * FF_CHUNK, :])
        probs = softmax(g, *current_scores)
        if i % 2 == 0:
            retention_head(*ret_tasks[i // 2])
        if pending_probs is not None:
            weighted_values(*tasks[i - 1], *pending_probs)
        pending_probs = probs
    weighted_values(*tasks[-1], *pending_probs)


def _back(x2d, mix2d, kdt, gates2d, meta_mix, meta_kdt, sinks, norm_gains, w_branch_out, w_up, w_down,
          blocks_per_batch):
    rows = x2d.shape[0]
    assert rows % BACK_ROWS == 0
    n_blocks = rows // BACK_ROWS
    _, band_dist, meta_dist = _swa_consts()

    cur = lambda t: jnp.minimum(t, n_blocks - 1)
    prev = lambda t: jnp.maximum(t - 1, 0)

    def resident(shape):
        zeros = (0,) * len(shape)
        return pl.BlockSpec(shape, lambda t, s: zeros, pipeline_mode=pl.Buffered(1))

    grid_spec = pltpu.PrefetchScalarGridSpec(
        num_scalar_prefetch=1,
        grid=(n_blocks + 1,),
        in_specs=[
            pl.BlockSpec((BACK_ROWS, MIX_W), lambda t, s: (cur(t), 0)),
            pl.BlockSpec((None, RET_QK_W, BACK_ROWS),
                         lambda t, s: (cur(t) // blocks_per_batch, 0, cur(t) % blocks_per_batch)),
            pl.BlockSpec((BACK_ROWS, RET_V_W), lambda t, s: (cur(t), GATE_RG // RET_V_W)),
            resident((1, CHUNK, MIX_W)),
            resident((1, RET_QK_W, CHUNK)),
            resident((3, CHUNK, CHUNK)),
            resident((2, D_MODEL)),
            pl.BlockSpec((BACK_ROWS, D_MODEL), lambda t, s: (prev(t), 0)),
            pl.BlockSpec((BACK_ROWS, 2 * D_MODEL), lambda t, s: (prev(t), 0)),
            resident((3, D_MODEL, D_MODEL)),
            resident((D_MODEL, D_FF)),
            resident((D_FF, D_MODEL)),
        ],
        out_specs=pl.BlockSpec((BACK_ROWS, D_MODEL), lambda t, s: (prev(t), 0)),
        scratch_shapes=[
            pltpu.VMEM((KV_SLOTS, SWA_KV_HEADS, PAIR_ROWS, LANES), BF16),
            pltpu.VMEM((KV_SLOTS, SWA_KV_HEADS, CHUNK, LANES), BF16),
            pltpu.VMEM((SWA_KV_HEADS, CHUNK, LANES), BF16),
            pltpu.VMEM((SWA_KV_HEADS, CHUNK, LANES), BF16),
            pltpu.VMEM((RET_HEADS, RET_QK_DIM, RET_V_DIM), F32),
            pltpu.VMEM((BACK_ROWS, SWA_Q_W), BF16),
            pltpu.VMEM((BACK_ROWS, RET_V_W), BF16),
            pltpu.VMEM((BACK_ROWS, D_MODEL), BF16),
        ],
    )
    return pl.pallas_call(
        functools.partial(_back_kernel, blocks_per_batch),
        out_shape=jax.ShapeDtypeStruct((rows, D_MODEL), F32),
        grid_spec=grid_spec,
        compiler_params=pltpu.CompilerParams(
            dimension_semantics=("arbitrary",),
            vmem_limit_bytes=V7X_VMEM_LIMIT_BYTES,
        ),
        name="back",
    )(sinks.astype(F32), mix2d, kdt, gates2d, meta_mix, meta_kdt,
      jnp.asarray(np.concatenate([band_dist, meta_dist[None]])), norm_gains,
      x2d, gates2d, w_branch_out, w_up, w_down)


def kernel(x, meta_tokens, mix_norm_g, w_in, ret_norm_g, q_norm_g, k_norm_g, sinks,
           w_ret_branch, w_swa_branch, w_out, mlp_norm_g, w_up, w_down):
    b, l, d = x.shape
    assert d == D_MODEL and l % BLOCK_ROWS == 0 and mix_norm_g.shape[0] == 1
    mix_g = mix_norm_g[0].reshape(1, d).astype(F32)
    w_in_b = w_in[0].astype(BF16)
    qg2 = jnp.tile(q_norm_g[0].astype(F32) * (SWA_HEAD_DIM ** -0.5), 2).reshape(1, LANES)
    kg2 = jnp.tile(k_norm_g[0].astype(F32), 2).reshape(1, LANES)

    mix, kdt, gates, w_branch_out, w_up_b, w_down_b = _inproj(
        x, mix_g, qg2, kg2, jnp.asarray(_decay_tables(BLOCK_ROWS)), w_in_b, BLOCK_ROWS,
        cast_weights=(w_ret_branch, w_swa_branch, w_out, w_up, w_down))
    meta_chunk = jnp.pad(meta_tokens.astype(F32), ((PAD, 0), (0, 0)))[None]
    meta_mix, meta_kdt, _ = _inproj(meta_chunk, mix_g, qg2, kg2, jnp.asarray(_meta_decay_tables()),
                                    w_in_b, CHUNK)

    norm_gains = jnp.stack([ret_norm_g[0].reshape(RET_V_W), mlp_norm_g[0]]).astype(F32)
    out = _back(
        x.reshape(b * l, d), mix.reshape(b * l, MIX_W), kdt, gates.reshape(b * l, GATE_W),
        meta_mix, meta_kdt, sinks[0], norm_gains, w_branch_out, w_up_b, w_down_b, l // BACK_ROWS)
    return out.reshape(b, l, d)
```

```python
import functools

import numpy as np
import jax
import jax.numpy as jnp
from jax import lax
from jax.experimental import pallas as pl
from jax.experimental.pallas import tpu as pltpu

D_MODEL = 1024
N_META = 16
CHUNK = 128
PAD = CHUNK - N_META
RET_HEADS = 4
RET_QK_DIM = 128
RET_V_DIM = 256
SWA_Q_HEADS = 16
SWA_KV_HEADS = 4
SWA_GROUP = SWA_Q_HEADS // SWA_KV_HEADS
SWA_HEAD_DIM = 64
D_FF = 4 * D_MODEL
EPS = 1e-6
RET_QK_W = RET_HEADS * RET_QK_DIM
RET_V_W = RET_HEADS * RET_V_DIM
SWA_Q_W = SWA_Q_HEADS * SWA_HEAD_DIM
SWA_KV_W = SWA_KV_HEADS * SWA_HEAD_DIM
D_IN = 2 * RET_QK_W + 2 * RET_V_W + SWA_Q_W + 2 * SWA_KV_W + 2 * D_MODEL

OFF_RQ = 0
OFF_RK = OFF_RQ + RET_QK_W
OFF_RV = OFF_RK + RET_QK_W
OFF_RG = OFF_RV + RET_V_W
OFF_SQ = OFF_RG + RET_V_W
OFF_SK = OFF_SQ + SWA_Q_W
OFF_SV = OFF_SK + SWA_KV_W
OFF_GR = OFF_SV + SWA_KV_W
OFF_GS = OFF_GR + D_MODEL
assert OFF_GS + D_MODEL == D_IN

MIX_SQ = 0
MIX_RV = MIX_SQ + SWA_Q_W
MIX_RQ = MIX_RV + RET_V_W
MIX_SK = MIX_RQ + RET_QK_W
MIX_SV = MIX_SK + SWA_KV_W
MIX_W = MIX_SV + SWA_KV_W
GATE_RG = 2 * D_MODEL
GATE_W = GATE_RG + RET_V_W

V7X_VMEM_LIMIT_BYTES = 64 * 1024 * 1024
LANES = 128
assert LANES == 2 * SWA_HEAD_DIM and CHUNK == LANES and RET_V_W == D_MODEL

F32 = jnp.float32
BF16 = jnp.bfloat16

BLOCK_ROWS = 1024
RET_CHUNK = 256
assert BLOCK_ROWS % RET_CHUNK == 0 and BLOCK_ROWS % CHUNK == 0


def _sigmoid(x):
    return 1.0 / (1.0 + jnp.exp(-x))


def _dot(a, b):
    return jnp.dot(a, b, preferred_element_type=F32)


def _dot_nt(a, b):
    return lax.dot_general(a, b, (((1,), (1,)), ((), ())), preferred_element_type=F32)


def _retention_log_gamma():
    return np.log1p(-(2.0 ** (-5.0 - np.arange(RET_HEADS, dtype=np.float64))))


IN_N_CHUNK = 512


def _head_pair_rms(y, g2):
    rows, width = y.shape
    low = lax.broadcasted_iota(jnp.int32, (rows, LANES), 1) < SWA_HEAD_DIM
    out = []
    for c in range(0, width, LANES):
        yb = y[:, c:c + LANES]
        y2 = yb * yb
        ms_lo = jnp.sum(jnp.where(low, y2, 0.0), axis=-1, keepdims=True) * (1.0 / SWA_HEAD_DIM)
        ms_hi = jnp.sum(jnp.where(low, 0.0, y2), axis=-1, keepdims=True) * (1.0 / SWA_HEAD_DIM)
        scale = jnp.where(low, lax.rsqrt(ms_lo + EPS), lax.rsqrt(ms_hi + EPS))
        out.append(yb * scale * g2)
    return jnp.concatenate(out, axis=1) if len(out) > 1 else out[0]


def _inproj_kernel(x_ref, g_ref, qg_ref, kg_ref, dec_ref, w_ref, mix_ref, kdt_ref, gate_ref):
    x = x_ref[...]
    ms = jnp.mean(x * x, axis=-1, keepdims=True)
    hn = (x * lax.rsqrt(ms + EPS) * g_ref[...]).astype(BF16)

    def proj(lo, width):
        return _dot(hn, w_ref[:, lo:lo + width])

    for c in range(0, SWA_Q_W, IN_N_CHUNK):
        mix_ref[:, MIX_SQ + c:MIX_SQ + c + IN_N_CHUNK] = _head_pair_rms(
            proj(OFF_SQ + c, IN_N_CHUNK), qg_ref[...]).astype(BF16)
    for c in range(0, RET_V_W, IN_N_CHUNK):
        mix_ref[:, MIX_RV + c:MIX_RV + c + IN_N_CHUNK] = proj(OFF_RV + c, IN_N_CHUNK).astype(BF16)
    mix_ref[:, MIX_RQ:MIX_RQ + RET_QK_W] = (proj(OFF_RQ, RET_QK_W) * dec_ref[:, 0:RET_QK_W]).astype(BF16)
    kd = proj(OFF_RK, RET_QK_W) * dec_ref[:, RET_QK_W:2 * RET_QK_W]
    kdt_ref[...] = kd.T.astype(BF16)
    mix_ref[:, MIX_SK:MIX_SK + SWA_KV_W] = _head_pair_rms(proj(OFF_SK, SWA_KV_W), kg_ref[...]).astype(BF16)
    mix_ref[:, MIX_SV:MIX_SV + SWA_KV_W] = proj(OFF_SV, SWA_KV_W).astype(BF16)
    for c in range(0, 2 * D_MODEL, IN_N_CHUNK):
        gate_ref[:, c:c + IN_N_CHUNK] = proj(OFF_GR + c, IN_N_CHUNK).astype(BF16)
    for c in range(0, RET_V_W, IN_N_CHUNK):
        gate_ref[:, GATE_RG + c:GATE_RG + c + IN_N_CHUNK] = proj(OFF_RG + c, IN_N_CHUNK).astype(BF16)


def _inproj_cast_kernel(x_ref, g_ref, qg_ref, kg_ref, dec_ref, w_ref,
                        wr_ref, ws_ref, wo_ref, wu_ref, wd_ref,
                        mix_ref, kdt_ref, gate_ref, wbo_out_ref, wu_out_ref, wd_out_ref):
    _inproj_kernel(x_ref, g_ref, qg_ref, kg_ref, dec_ref, w_ref, mix_ref, kdt_ref, gate_ref)
    for k, src in enumerate((wr_ref, ws_ref, wo_ref)):
        wbo_out_ref[k] = src[...].astype(BF16)
    wu_out_ref[...] = wu_ref[...].astype(BF16)
    wd_out_ref[...] = wd_ref[...].astype(BF16)


BF16_ROW_TILE = 16


def _inproj(x3d, g, qg2, kg2, dec, w_bf16, block_rows, cast_weights=None):
    b, seq, _ = x3d.shape
    assert seq % block_rows == 0 and dec.shape == (block_rows, 2 * RET_QK_W)
    steps = seq // block_rows
    n_steps = b * steps
    const = lambda shape: pl.BlockSpec(shape, lambda i: (0, 0))
    out_shape = [jax.ShapeDtypeStruct((b, seq, MIX_W), BF16),
                 jax.ShapeDtypeStruct((b, RET_QK_W, seq), BF16),
                 jax.ShapeDtypeStruct((b, seq, GATE_W), BF16)]
    out_specs = [pl.BlockSpec((None, block_rows, MIX_W), lambda i: (i // steps, i % steps, 0)),
                 pl.BlockSpec((None, RET_QK_W, block_rows), lambda i: (i // steps, 0, i % steps)),
                 pl.BlockSpec((None, block_rows, GATE_W), lambda i: (i // steps, i % steps, 0))]
    extra_in, extra_specs = [], []
    if cast_weights is not None:
        band = lambda w: w.shape[1] // n_steps
        for w in cast_weights:
            assert w.shape[1] % n_steps == 0 and band(w) % BF16_ROW_TILE == 0
            extra_in.append(w)
            extra_specs.append(pl.BlockSpec((None, band(w), w.shape[2]), lambda i: (0, i, 0)))
        w_r, _, _, w_u, w_d = cast_weights
        out_shape += [jax.ShapeDtypeStruct((3,) + w_r.shape[1:], BF16),
                      jax.ShapeDtypeStruct(w_u.shape[1:], BF16),
                      jax.ShapeDtypeStruct(w_d.shape[1:], BF16)]
        out_specs += [pl.BlockSpec((3, band(w_r), w_r.shape[2]), lambda i: (0, i, 0)),
                      pl.BlockSpec((band(w_u), w_u.shape[2]), lambda i: (i, 0)),
                      pl.BlockSpec((band(w_d), w_d.shape[2]), lambda i: (i, 0))]
    return pl.pallas_call(
        _inproj_kernel if cast_weights is None else _inproj_cast_kernel,
        out_shape=tuple(out_shape),
        grid=(n_steps,),
        in_specs=[
            pl.BlockSpec((None, block_rows, D_MODEL), lambda i: (i // steps, i % steps, 0)),
            const((1, D_MODEL)),
            const((1, LANES)),
            const((1, LANES)),
            const((block_rows, 2 * RET_QK_W)),
            pl.BlockSpec((D_MODEL, D_IN), lambda i: (0, 0), pipeline_mode=pl.Buffered(1)),
        ] + extra_specs,
        out_specs=tuple(out_specs),
        compiler_params=pltpu.CompilerParams(
            dimension_semantics=("arbitrary",),
            vmem_limit_bytes=V7X_VMEM_LIMIT_BYTES,
        ),
        name="inproj",
    )(x3d, g, qg2, kg2, dec, w_bf16, *extra_in)


def _decay_tables(block_rows):
    lg = _retention_log_gamma()
    pos = (np.arange(block_rows) % RET_CHUNK).astype(np.float64) + 1.0
    qd = np.exp(lg[None, :] * pos[:, None])
    kd = np.exp(-lg[None, :] * pos[:, None]) * (RET_QK_DIM ** -0.5)
    rep = lambda t: np.repeat(t, RET_QK_DIM, axis=1)
    return np.concatenate([rep(qd), rep(kd)], axis=1).astype(np.float32)


def _meta_decay_tables():
    lg = _retention_log_gamma()
    pos = np.arange(CHUNK, dtype=np.float64)
    kd = np.exp(lg[None, :] * (CHUNK - 1.0 - pos[:, None])) * (RET_QK_DIM ** -0.5)
    rep = lambda t: np.repeat(t, RET_QK_DIM, axis=1)
    return np.concatenate([np.ones((CHUNK, RET_QK_W)), rep(kd)], axis=1).astype(np.float32)


RET_CHUNK_DECAY = [float(c) for c in np.exp(_retention_log_gamma() * RET_CHUNK)]
PAIR_ROWS = 2 * CHUNK
BACK_ROWS = 512
DENSE_ROWS = 512
assert BACK_ROWS % RET_CHUNK == 0 and BLOCK_ROWS % BACK_ROWS == 0 and BACK_ROWS % DENSE_ROWS == 0
CHUNKS_PER_BLOCK = BACK_ROWS // CHUNK
RET_PER_BLOCK = BACK_ROWS // RET_CHUNK
KV_SLOTS = CHUNKS_PER_BLOCK + 1
FF_CHUNK = 512


def _swa_consts():
    slopes = 2.0 ** (-8.0 * np.arange(1, SWA_Q_HEADS + 1, dtype=np.float64) / SWA_Q_HEADS)
    slopes = slopes.astype(np.float32)
    i = np.arange(CHUNK)[:, None]
    j = np.arange(CHUNK)[None, :]
    band_dist = np.where(j <= i, i - j, i - j + CHUNK).astype(np.float32)
    meta_dist = np.where(j < 2 * N_META, i - (j % N_META) + N_META, np.inf).astype(np.float32)
    band_dist = np.stack([np.where(j <= i, band_dist, np.inf), band_dist]).astype(np.float32)
    return [float(s) for s in slopes], band_dist, meta_dist


ALIBI_SLOPES = _swa_consts()[0]


def _back_kernel(blocks_per_batch,
                 sink_ref,
                 mix_ref, kdt_ref, rg_ref, meta_ref, mkdt_ref, dist_ref, gains_ref,
                 x_ref, gate_ref, wbo_ref, wu_ref, wd_ref,
                 o_ref,
                 bd_ref, vbuf_ref, bdm_ref, vm_ref, state_ref, ys_ref, yr_ref, hn_ref):
    sq_ref = mix_ref.at[:, MIX_SQ:MIX_SQ + SWA_Q_W]
    rv_ref = mix_ref.at[:, MIX_RV:MIX_RV + RET_V_W]
    qd_ref = mix_ref.at[:, MIX_RQ:MIX_RQ + RET_QK_W]
    sk_ref = mix_ref.at[:, MIX_SK:MIX_SK + SWA_KV_W]
    sv_ref = mix_ref.at[:, MIX_SV:MIX_SV + SWA_KV_W]
    mk_ref = meta_ref.at[0, PAD:CHUNK, MIX_SK:MIX_SK + SWA_KV_W]
    mv_ref = meta_ref.at[0, PAD:CHUNK, MIX_SV:MIX_SV + SWA_KV_W]
    mrv_ref = meta_ref.at[0, :, MIX_RV:MIX_RV + RET_V_W]
    mkdt_ref = mkdt_ref.at[0]
    band_dist_ref = dist_ref.at[0:2]
    meta_dist_ref = dist_ref.at[2]
    rn_ref = gains_ref.at[0:1, :]
    g_ref = gains_ref.at[1:2, :]
    gr_ref = gate_ref.at[:, 0:D_MODEL]
    gs_ref = gate_ref.at[:, D_MODEL:2 * D_MODEL]
    wr_ref, ws_ref, wo_ref = wbo_ref.at[0], wbo_ref.at[1], wbo_ref.at[2]
    t = pl.program_id(0)
    block = jnp.minimum(t, pl.num_programs(0) - 2)
    block_in_batch = block % blocks_per_batch
    parity = t & 1
    last_slot = CHUNKS_PER_BLOCK - 1 + parity
    carried_slot = CHUNKS_PER_BLOCK - parity
    zeros_head = jnp.zeros((CHUNK, SWA_HEAD_DIM), BF16)

    @pl.when(t == 0)
    def _():
        ys_ref[...] = jnp.zeros_like(ys_ref)
        yr_ref[...] = jnp.zeros_like(yr_ref)

    @pl.when(block_in_batch == 0)
    def _():
        bd_ref[...] = jnp.zeros_like(bd_ref)
        vbuf_ref[...] = jnp.zeros_like(vbuf_ref)
        bdm_ref[...] = jnp.zeros_like(bdm_ref)
        vm_ref[...] = jnp.zeros_like(vm_ref)
        zm = jnp.zeros((N_META, SWA_HEAD_DIM), BF16)
        for g in range(SWA_KV_HEADS):
            cols = slice(g * SWA_HEAD_DIM, (g + 1) * SWA_HEAD_DIM)
            km = mk_ref[:, cols]
            bdm_ref[g, 0:N_META, :] = jnp.concatenate([km, zm], axis=1)
            bdm_ref[g, N_META:2 * N_META, :] = jnp.concatenate([zm, km], axis=1)
            vmeta = mv_ref[:, cols]
            vm_ref[g, 0:N_META, :] = jnp.concatenate([vmeta, zm], axis=1)
            vm_ref[g, N_META:2 * N_META, :] = jnp.concatenate([zm, vmeta], axis=1)
        for h in range(RET_HEADS):
            state_ref[h] = _dot(mkdt_ref[h * RET_QK_DIM:(h + 1) * RET_QK_DIM, :],
                                mrv_ref[:, h * RET_V_DIM:(h + 1) * RET_V_DIM])

    slabs = [slice(r, r + DENSE_ROWS) for r in range(0, BACK_ROWS, DENSE_ROWS)]
    for rs in slabs:
        merged = (_sigmoid(gr_ref[rs, :].astype(F32)) * _dot(yr_ref[rs, :], wr_ref[...])
                  + _sigmoid(gs_ref[rs, :].astype(F32)) * _dot(ys_ref[rs, :], ws_ref[...]))
        o_ref[rs, :] = x_ref[rs, :] + _dot(merged.astype(BF16), wo_ref[...])
        h1 = o_ref[rs, :]
        hn_ref[rs, :] = (h1 * lax.rsqrt(jnp.mean(h1 * h1, axis=-1, keepdims=True) + EPS)
                         * g_ref[...]).astype(BF16)

    ret_row = lax.broadcasted_iota(jnp.int32, (RET_CHUNK, RET_CHUNK), 0)
    ret_col = lax.broadcasted_iota(jnp.int32, (RET_CHUNK, RET_CHUNK), 1)
    ret_causal = ret_col <= ret_row

    def retention_head(sub, h):
        rows = slice(sub * RET_CHUNK, (sub + 1) * RET_CHUNK)
        vcols = slice(h * RET_V_DIM, (h + 1) * RET_V_DIM)
        qd = qd_ref[rows, h * RET_QK_DIM:(h + 1) * RET_QK_DIM]
        kdt = kdt_ref[h * RET_QK_DIM:(h + 1) * RET_QK_DIM, rows]
        v = rv_ref[rows, vcols]
        state = state_ref[h]
        s = jnp.where(ret_causal, _dot(qd, kdt), 0.0).astype(BF16)
        o = _dot(s, v) + _dot(qd, state.astype(BF16))
        state_ref[h] = RET_CHUNK_DECAY[h] * (state + _dot(kdt, v))
        ms = jnp.mean(o * o, axis=-1, keepdims=True)
        gate = rg_ref[rows, vcols].astype(F32)
        yr_ref[rows, vcols] = (o * lax.rsqrt(ms + EPS) * rn_ref[:, vcols]
                               * (gate * _sigmoid(gate))).astype(BF16)

    row = lax.broadcasted_iota(jnp.int32, (CHUNK, CHUNK), 0)
    col = lax.broadcasted_iota(jnp.int32, (CHUNK, CHUNK), 1)
    own = col <= row
    own2 = jnp.concatenate([own, own], axis=1)
    own4 = jnp.concatenate([own2, own2], axis=0)
    meta_seg = [col < N_META, (col >= N_META) & (col < 2 * N_META)]
    low_half = col < SWA_HEAD_DIM
    neg_inf = jnp.float32(-jnp.inf)

    first_chunk = (CHUNKS_PER_BLOCK * block_in_batch).astype(F32)
    chunks = []
    for c in range(CHUNKS_PER_BLOCK):
        slot = c if c < CHUNKS_PER_BLOCK - 1 else last_slot
        prev_slot = c - 1 if c > 0 else carried_slot
        bias_sel = jnp.minimum(block_in_batch, 1) if c == 0 else 1
        chunks.append((slice(c * CHUNK, (c + 1) * CHUNK), slot, prev_slot, bias_sel, first_chunk + float(c)))

    for rows, slot, _, _, _ in chunks:
        for g in range(SWA_KV_HEADS):
            kg = sk_ref[rows, g * SWA_HEAD_DIM:(g + 1) * SWA_HEAD_DIM]
            bd_ref[slot, g, 0:CHUNK, :] = jnp.concatenate([kg, zeros_head], axis=1)
            bd_ref[slot, g, CHUNK:2 * CHUNK, :] = jnp.concatenate([zeros_head, kg], axis=1)
            vg = sv_ref[rows, g * SWA_HEAD_DIM:(g + 1) * SWA_HEAD_DIM]
            vbuf_ref[slot, g] = jnp.concatenate([vg, vg], axis=1)

    def scores(chunk, g):
        rows, slot, prev_slot, bias_sel, chunk_idx = chunk
        q2 = jnp.concatenate(
            [sq_ref[rows, (2 * g + p) * LANES:(2 * g + p + 1) * LANES] for p in range(2)],
            axis=0)
        s_own = _dot_nt(q2, bd_ref[slot, g])
        s_prev = _dot_nt(q2, bd_ref[prev_slot, g])
        dist = band_dist_ref[bias_sel]
        bias = jnp.concatenate(
            [jnp.concatenate([ALIBI_SLOPES[g * SWA_GROUP + 2 * p + e] * dist for e in range(2)], axis=1)
             for p in range(2)], axis=0)
        s = jnp.where(own4, s_own, s_prev) - bias
        dist_meta = meta_dist_ref[...] + float(CHUNK) * chunk_idx
        bias_meta = jnp.concatenate(
            [jnp.where(meta_seg[0], ALIBI_SLOPES[g * SWA_GROUP + 2 * p],
                       ALIBI_SLOPES[g * SWA_GROUP + 2 * p + 1]) * dist_meta for p in range(2)], axis=0)
        s_meta = _dot_nt(q2, bdm_ref[g]) - bias_meta
        return s, s_meta

    def softmax(g, s, s_meta):
        lhs = []
        lhs_meta = []
        inv = []
        for p in range(2):
            sm_p = s_meta[p * CHUNK:(p + 1) * CHUNK]
            pm_pair = None
            for e in range(2):
                h = g * SWA_GROUP + 2 * p + e
                sink = sink_ref[h]
                sb = s[p * CHUNK:(p + 1) * CHUNK, e * CHUNK:(e + 1) * CHUNK]
                sm = jnp.where(meta_seg[e], sm_p, neg_inf)
                m = jnp.maximum(jnp.max(jnp.maximum(sb, sm), axis=-1, keepdims=True), sink)
                pb = jnp.exp(sb - m)
                pm = jnp.exp(sm - m)
                denom = jnp.sum(pb + pm, axis=-1, keepdims=True) + jnp.exp(sink - m)
                inv.append(1.0 / denom)
                lhs.append(jnp.concatenate(
                    [jnp.where(own, pb, 0.0).astype(BF16), jnp.where(own, 0.0, pb).astype(BF16)], axis=1))
                pm_pair = pm if pm_pair is None else pm_pair + pm
            lhs_meta.append(pm_pair.astype(BF16))
        return jnp.concatenate(lhs, axis=0), jnp.concatenate(lhs_meta, axis=0), inv

    def weighted_values(chunk, g, lhs, lhs_meta, inv):
        rows, slot, prev_slot, _, _ = chunk
        vv = jnp.concatenate([vbuf_ref[slot, g], vbuf_ref[prev_slot, g]], axis=0)
        acc = _dot(lhs, vv)
        acc_meta = _dot(lhs_meta, vm_ref[g])
        for p in range(2):
            first, second = 2 * p, 2 * p + 1
            pair = jnp.where(low_half, acc[first * CHUNK:(first + 1) * CHUNK],
                             acc[second * CHUNK:(second + 1) * CHUNK])
            scale = jnp.where(low_half, inv[first], inv[second])
            tile = (pair + acc_meta[p * CHUNK:(p + 1) * CHUNK]) * scale
            ys_ref[rows, (2 * g + p) * LANES:(2 * g + p + 1) * LANES] = tile.astype(BF16)

    tasks = [(chunk, g) for chunk in chunks for g in range(SWA_KV_HEADS)]
    ret_tasks = [(sub, h) for sub in range(RET_PER_BLOCK) for h in range(RET_HEADS)]
    assert len(tasks) == 2 * len(ret_tasks)

    def mlp_up(k):
        cols = slice(k * FF_CHUNK, (k + 1) * FF_CHUNK)
        return [jnp.square(jnp.maximum(_dot(hn_ref[rs, :], wu_ref[:, cols]), 0.0)).astype(BF16)
                for rs in slabs]

    def mlp_down(k, acts):
        for rs, act in zip(slabs, acts):
            o_ref[rs, :] += _dot(act, wd_ref[k * FF_CHUNK:(k + 1) * FF_CHUNK, :])

    n_mlp_phases = D_FF // FF_CHUNK
    phase_at_task = {i * len(tasks) // n_mlp_phases: i for i in range(n_mlp_phases)}

    pending_scores = scores(*tasks[0])
    pending_probs = None
    for i, (chunk, g) in enumerate(tasks):
        current_scores = pending_scores
        if i + 1 < len(tasks):
            pending_scores = scores(*tasks[i + 1])
        if i in phase_at_task:
            acts = mlp_up(phase_at_task[i])
        elif i - 1 in phase_at_task:
            mlp_down(phase_at_task[i - 1], acts)
        probs = softmax(g, *current_scores)
        if i % 2 == 0:
            retention_head(*ret_tasks[i // 2])
        if pending_probs is not None:
            weighted_values(*tasks[i - 1], *pending_probs)
        pending_probs = probs
    weighted_values(*tasks[-1], *pending_probs)


def _back(x2d, mix2d, kdt, gates2d, meta_mix, meta_kdt, sinks, norm_gains, w_branch_out, w_up, w_down,
          blocks_per_batch):
    rows = x2d.shape[0]
    assert rows % BACK_ROWS == 0
    n_blocks = rows // BACK_ROWS
    _, band_dist, meta_dist = _swa_consts()

    cur = lambda t: jnp.minimum(t, n_blocks - 1)
    prev = lambda t: jnp.maximum(t - 1, 0)

    def resident(shape):
        zeros = (0,) * len(shape)
        return pl.BlockSpec(shape, lambda t, s: zeros, pipeline_mode=pl.Buffered(1))

    grid_spec = pltpu.PrefetchScalarGridSpec(
        num_scalar_prefetch=1,
        grid=(n_blocks + 1,),
        in_specs=[
            pl.BlockSpec((BACK_ROWS, MIX_W), lambda t, s: (cur(t), 0)),
            pl.BlockSpec((None, RET_QK_W, BACK_ROWS),
                         lambda t, s: (cur(t) // blocks_per_batch, 0, cur(t) % blocks_per_batch)),
            pl.BlockSpec((BACK_ROWS, RET_V_W), lambda t, s: (cur(t), GATE_RG // RET_V_W)),
            resident((1, CHUNK, MIX_W)),
            resident((1, RET_QK_W, CHUNK)),
            resident((3, CHUNK, CHUNK)),
            resident((2, D_MODEL)),
            pl.BlockSpec((BACK_ROWS, D_MODEL), lambda t, s: (prev(t), 0)),
            pl.BlockSpec((BACK_ROWS, 2 * D_MODEL), lambda t, s: (prev(t), 0)),
            resident((3, D_MODEL, D_MODEL)),
            resident((D_MODEL, D_FF)),
            resident((D_FF, D_MODEL)),
        ],
        out_specs=pl.BlockSpec((BACK_ROWS, D_MODEL), lambda t, s: (prev(t), 0)),
        scratch_shapes=[
            pltpu.VMEM((KV_SLOTS, SWA_KV_HEADS, PAIR_ROWS, LANES), BF16),
            pltpu.VMEM((KV_SLOTS, SWA_KV_HEADS, CHUNK, LANES), BF16),
            pltpu.VMEM((SWA_KV_HEADS, CHUNK, LANES), BF16),
            pltpu.VMEM((SWA_KV_HEADS, CHUNK, LANES), BF16),
            pltpu.VMEM((RET_HEADS, RET_QK_DIM, RET_V_DIM), F32),
            pltpu.VMEM((BACK_ROWS, SWA_Q_W), BF16),
            pltpu.VMEM((BACK_ROWS, RET_V_W), BF16),
            pltpu.VMEM((BACK_ROWS, D_MODEL), BF16),
        ],
    )
    return pl.pallas_call(
        functools.partial(_back_kernel, blocks_per_batch),
        out_shape=jax.ShapeDtypeStruct((rows, D_MODEL), F32),
        grid_spec=grid_spec,
        compiler_params=pltpu.CompilerParams(
            dimension_semantics=("arbitrary",),
            vmem_limit_bytes=V7X_VMEM_LIMIT_BYTES,
        ),
        name="back",
    )(sinks.astype(F32), mix2d, kdt, gates2d, meta_mix, meta_kdt,
      jnp.asarray(np.concatenate([band_dist, meta_dist[None]])), norm_gains,
      x2d, gates2d, w_branch_out, w_up, w_down)


def kernel(x, meta_tokens, mix_norm_g, w_in, ret_norm_g, q_norm_g, k_norm_g, sinks,
           w_ret_branch, w_swa_branch, w_out, mlp_norm_g, w_up, w_down):
    b, l, d = x.shape
    assert d == D_MODEL and l % BLOCK_ROWS == 0 and mix_norm_g.shape[0] == 1
    mix_g = mix_norm_g[0].reshape(1, d).astype(F32)
    w_in_b = w_in[0].astype(BF16)
    qg2 = jnp.tile(q_norm_g[0].astype(F32) * (SWA_HEAD_DIM ** -0.5), 2).reshape(1, LANES)
    kg2 = jnp.tile(k_norm_g[0].astype(F32), 2).reshape(1, LANES)

    mix, kdt, gates, w_branch_out, w_up_b, w_down_b = _inproj(
        x, mix_g, qg2, kg2, jnp.asarray(_decay_tables(BLOCK_ROWS)), w_in_b, BLOCK_ROWS,
        cast_weights=(w_ret_branch, w_swa_branch, w_out, w_up, w_down))
    meta_chunk = jnp.pad(meta_tokens.astype(F32), ((PAD, 0), (0, 0)))[None]
    meta_mix, meta_kdt, _ = _inproj(meta_chunk, mix_g, qg2, kg2, jnp.asarray(_meta_decay_tables()),
                                    w_in_b, CHUNK)

    norm_gains = jnp.stack([ret_norm_g[0].reshape(RET_V_W), mlp_norm_g[0]]).astype(F32)
    out = _back(
        x.reshape(b * l, d), mix.reshape(b * l, MIX_W), kdt, gates.reshape(b * l, GATE_W),
        meta_mix, meta_kdt, sinks[0], norm_gains, w_branch_out, w_up_b, w_down_b, l // BACK_ROWS)
    return out.reshape(b, l, d)
```
